```python
import jax, jax.numpy as jnp
from jax import lax
import numpy as np

D_MODEL = 1024
BATCH = 4
SEQ = 4096
DEPTH = 2

N_BRANCH = 4
BRANCH_W = 256
SGU_GROUPS = 4
SGU_GD = BRANCH_W // SGU_GROUPS
SGU_CHUNK = 128
ATT_HEADS = 4
ATT_HD = 64
IDX_HEADS = 8
IDX_HD = 64
TOPK_MAX = 256
QBLOCK = 128
ROPE_THETA = 10000.0
MLSTM_HEADS = 4
MLSTM_HD = 64
MLSTM_CHUNK = 128
CONV_WIDTH = 3
D_FF = 4 * D_MODEL
EPS = 1e-6

SPLIT_SIZES = (
    BRANCH_W, BRANCH_W,
    ATT_HEADS * ATT_HD, ATT_HEADS * ATT_HD, ATT_HEADS * ATT_HD,
    IDX_HEADS * IDX_HD, IDX_HD, IDX_HEADS,
    MLSTM_HEADS * MLSTM_HD, MLSTM_HEADS * MLSTM_HD, MLSTM_HEADS * MLSTM_HD, MLSTM_HEADS * MLSTM_HD,
    MLSTM_HEADS, MLSTM_HEADS,
    BRANCH_W, BRANCH_W, BRANCH_W,
    N_BRANCH * D_MODEL,
)
IN_W = sum(SPLIT_SIZES)

kernel_name = "hybrid_gated_parallel_mixers"


def rms_norm(x, g):
    xf = x.astype(jnp.float32)
    y = xf * lax.rsqrt(jnp.mean(xf * xf, axis=-1, keepdims=True) + EPS)
    return (y * g.astype(jnp.float32)).astype(x.dtype)


def layer_norm_nobias(x, g):
    xf = x.astype(jnp.float32)
    mu = jnp.mean(xf, axis=-1, keepdims=True)
    var = jnp.mean(jnp.square(xf - mu), axis=-1, keepdims=True)
    return ((xf - mu) * lax.rsqrt(var + EPS) * g.astype(jnp.float32)).astype(x.dtype)


def rotary(x, pos):
    d = x.shape[-1]
    half = d // 2
    inv = jnp.float32(ROPE_THETA) ** (-jnp.arange(half, dtype=jnp.float32) * 2.0 / d)
    ang = pos.astype(jnp.float32)[:, None] * inv[None, :]
    cos = jnp.cos(ang)[None, :, None, :]
    sin = jnp.sin(ang)[None, :, None, :]
    xf = x.astype(jnp.float32)
    x1, x2 = xf[..., :half], xf[..., half:]
    return jnp.concatenate([x1 * cos - x2 * sin, x2 * cos + x1 * sin], axis=-1).astype(x.dtype)


def sgu_mix(u, v, norm_g, w_s, b_s):
    bn, s, w = v.shape
    v = rms_norm(v, norm_g).reshape(bn, s // SGU_CHUNK, SGU_CHUNK, SGU_GROUPS, SGU_GD)
    mask = jnp.tril(jnp.ones((SGU_CHUNK, SGU_CHUNK), dtype=w_s.dtype))
    mixed = jnp.einsum('gts,bcsgd->bctgd', w_s * mask, v) + b_s.T[:, :, None]
    return u * mixed.reshape(bn, s, w)


def dsa_attention(q, k, v, q_idx, k_idx, w_idx):
    bn, s = q.shape[:2]
    n_sel = min(TOPK_MAX, s // 4)
    nb = s // QBLOCK
    kpos = jnp.arange(s)

    def to_blocks(a):
        return jnp.moveaxis(a.reshape((bn, nb, QBLOCK) + a.shape[2:]), 1, 0)

    def one_block(args):
        blk, qb, qib, wb = args
        tpos = blk * QBLOCK + jnp.arange(QBLOCK)
        rel = jax.nn.relu(jnp.einsum('bqhd,bsd->bqhs', qib, k_idx).astype(jnp.float32) * (IDX_HD ** -0.5))
        score = jnp.einsum('bqh,bqhs->bqs', wb.astype(jnp.float32) * (IDX_HEADS ** -0.5), rel)
        causal = kpos[None, :] <= tpos[:, None]
        score = jnp.where(causal[None], score, -jnp.inf)
        _, idx = lax.top_k(score, n_sel)
        valid = idx <= tpos[None, :, None]
        ks = jax.vmap(lambda a, i: a[i])(k, idx)
        vs = jax.vmap(lambda a, i: a[i])(v, idx)
        logits = jnp.einsum('bqhd,bqkhd->bhqk', qb, ks).astype(jnp.float32) * (ATT_HD ** -0.5)
        logits = jnp.where(valid[:, None], logits, -jnp.inf)
        p = jax.nn.softmax(logits, axis=-1).astype(vs.dtype)
        return jnp.einsum('bhqk,bqkhd->bqhd', p, vs)

    out = lax.map(one_block, (jnp.arange(nb), to_blocks(q), to_blocks(q_idx), to_blocks(w_idx)))
    return jnp.moveaxis(out, 0, 1).reshape(bn, s, -1)


def mlstm_chunkwise(q, k, v, i_pre, f_pre):
    bn, s, nh, d = q.shape
    nc = s // MLSTM_CHUNK
    L = MLSTM_CHUNK

    def chunks(a):
        a = a.astype(jnp.float32).reshape((bn, nc, L) + a.shape[2:])
        return jnp.moveaxis(a, (1, 3), (0, 2))

    xs = (chunks(q), chunks(k * (d ** -0.5)), chunks(v), chunks(i_pre), chunks(f_pre))
    tril = jnp.tril(jnp.ones((L, L), dtype=bool))

    def step(carry, inp):
        c_st, n_st, m_st = carry
        qc, kc, vc, ig, fg = inp
        b = jnp.cumsum(jax.nn.log_sigmoid(fg), axis=-1)
        dmat = jnp.where(tril, b[..., :, None] - b[..., None, :] + ig[..., None, :], -jnp.inf)
        inter = b + m_st[..., None]
        mj = jnp.maximum(inter, jnp.max(dmat, axis=-1))
        a = jnp.exp(dmat - mj[..., None]) * jnp.einsum('bhjd,bhsd->bhjs', qc, kc)
        w_inter = jnp.exp(inter - mj)
        num = w_inter[..., None] * jnp.einsum('bhjd,bhde->bhje', qc, c_st) + jnp.einsum('bhjs,bhse->bhje', a, vc)
        den = w_inter * jnp.einsum('bhjd,bhd->bhj', qc, n_st) + jnp.sum(a, axis=-1)
        h = num / jnp.maximum(jnp.abs(den), jnp.exp(-mj))[..., None]
        b_last = b[..., -1]
        m_new = mj[..., -1]
        wk = jnp.exp(b_last[..., None] - b + ig - m_new[..., None])
        decay = jnp.exp(b_last + m_st - m_new)
        c_new = decay[..., None, None] * c_st + jnp.einsum('bhs,bhsd,bhse->bhde', wk, kc, vc)
        n_new = decay[..., None] * n_st + jnp.einsum('bhs,bhsd->bhd', wk, kc)
        return (c_new, n_new, m_new), h

    init = (jnp.zeros((bn, nh, d, d), jnp.float32), jnp.zeros((bn, nh, d), jnp.float32),
            jnp.zeros((bn, nh), jnp.float32))
    _, hs = lax.scan(step, init, xs)
    return jnp.moveaxis(hs, (0, 2), (1, 3)).reshape(bn, s, nh, d)


def short_conv(b_gate, c_gate, x_in, w):
    z = c_gate * x_in
    y = lax.conv_general_dilated(z, w[:, None, :], window_strides=(1,), padding=[(CONV_WIDTH - 1, 0)],
                                 dimension_numbers=('NWC', 'WIO', 'NWC'), feature_group_count=z.shape[-1])
    return b_gate * y


def hybrid_mixer(h, pos, w_in, sgu_norm, sgu_w, sgu_b, q_norm, k_norm, kidx_norm,
                 i_bias, f_bias, mlstm_norm, conv_w, w_branch, w_out):
    bn, s, _ = h.shape
    proj = h @ w_in
    offsets = np.cumsum(SPLIT_SIZES)[:-1].tolist()
    (a_u, a_v, b_q, b_k, b_v, b_qi, b_ki, b_wi, c_q, c_k, c_v, c_o, c_i, c_f,
     d_b, d_c, d_x, g) = jnp.split(proj, offsets, axis=-1)

    def heads(a, nh):
        return a.reshape(bn, s, nh, -1)

    y_a = sgu_mix(jax.nn.gelu(a_u), jax.nn.gelu(a_v), sgu_norm, sgu_w, sgu_b)
    q = rotary(rms_norm(heads(b_q, ATT_HEADS), q_norm), pos)
    k = rotary(rms_norm(heads(b_k, ATT_HEADS), k_norm), pos)
    v = heads(b_v, ATT_HEADS)
    qi = rotary(heads(b_qi, IDX_HEADS), pos)
    ki = rotary(layer_norm_nobias(b_ki, kidx_norm)[:, :, None, :], pos)[:, :, 0, :]
    y_b = dsa_attention(q, k, v, qi, ki, b_wi)
    hc = mlstm_chunkwise(heads(c_q, MLSTM_HEADS), heads(c_k, MLSTM_HEADS), heads(c_v, MLSTM_HEADS),
                         c_i + i_bias, c_f + f_bias).astype(h.dtype)
    hc = rms_norm(hc, mlstm_norm.reshape(MLSTM_HEADS, MLSTM_HD)).reshape(bn, s, -1)
    y_c = jax.nn.sigmoid(c_o) * hc
    y_d = short_conv(d_b, d_c, d_x, conv_w)
    gates = jax.nn.sigmoid(g.reshape(bn, s, N_BRANCH, D_MODEL))
    merged = gates[:, :, 0] * (y_a @ w_branch[0])
    merged = merged + gates[:, :, 1] * (y_b @ w_branch[1])
    merged = merged + gates[:, :, 2] * (y_c @ w_branch[2])
    merged = merged + gates[:, :, 3] * (y_d @ w_branch[3])
    return merged @ w_out


def squared_relu_mlp(h, w_up, w_down):
    return jnp.square(jax.nn.relu(h @ w_up)) @ w_down


def setup_inputs(seed: int = 0) -> dict:
    key = jax.random.key(seed)
    ks = jax.random.split(key, 20)
    f32 = jnp.float32

    def nrm(k, shape, scale):
        return jax.random.normal(k, shape, f32) * scale

    def gain(k, shape):
        return 1.0 + 0.02 * jax.random.normal(k, shape, f32)

    f_bias = jnp.linspace(3.0, 6.0, MLSTM_HEADS, dtype=f32)[None, :] + 0.1 * jax.random.normal(ks[11], (DEPTH, MLSTM_HEADS), f32)
    return {
        "x": nrm(ks[0], (BATCH, SEQ, D_MODEL), 1.0),
        "ln_mix": gain(ks[1], (DEPTH, D_MODEL)),
        "w_in": nrm(ks[2], (DEPTH, D_MODEL, IN_W), D_MODEL ** -0.5),
        "sgu_norm": gain(ks[3], (DEPTH, BRANCH_W)),
        "sgu_w": nrm(ks[4], (DEPTH, SGU_GROUPS, SGU_CHUNK, SGU_CHUNK), SGU_CHUNK ** -0.5),
        "sgu_b": gain(ks[5], (DEPTH, SGU_GROUPS, SGU_CHUNK)),
        "q_norm": gain(ks[6], (DEPTH, ATT_HD)),
        "k_norm": gain(ks[7], (DEPTH, ATT_HD)),
        "kidx_norm": gain(ks[8], (DEPTH, IDX_HD)),
        "mlstm_i_bias": nrm(ks[10], (DEPTH, MLSTM_HEADS), 0.1),
        "mlstm_f_bias": f_bias,
        "mlstm_norm": gain(ks[12], (DEPTH, MLSTM_HEADS * MLSTM_HD)),
        "conv_w": nrm(ks[13], (DEPTH, CONV_WIDTH, BRANCH_W), CONV_WIDTH ** -0.5),
        "w_branch": nrm(ks[14], (DEPTH, N_BRANCH, BRANCH_W, D_MODEL), BRANCH_W ** -0.5),
        "w_out": nrm(ks[15], (DEPTH, D_MODEL, D_MODEL), D_MODEL ** -0.5),
        "ln_mlp": gain(ks[16], (DEPTH, D_MODEL)),
        "w_up": nrm(ks[17], (DEPTH, D_MODEL, D_FF), D_MODEL ** -0.5),
        "w_down": nrm(ks[18], (DEPTH, D_FF, D_MODEL), D_FF ** -0.5),
    }


def reference(x, ln_mix, w_in, sgu_norm, sgu_w, sgu_b, q_norm, k_norm, kidx_norm,
              mlstm_i_bias, mlstm_f_bias, mlstm_norm, conv_w, w_branch, w_out,
              ln_mlp, w_up, w_down):
    pos = jnp.arange(x.shape[1], dtype=jnp.int32)
    for l in range(DEPTH):
        h = rms_norm(x, ln_mix[l])
        x = x + hybrid_mixer(h, pos, w_in[l], sgu_norm[l], sgu_w[l], sgu_b[l], q_norm[l], k_norm[l],
                             kidx_norm[l], mlstm_i_bias[l], mlstm_f_bias[l], mlstm_norm[l],
                             conv_w[l], w_branch[l], w_out[l])
        h = rms_norm(x, ln_mlp[l])
        x = x + squared_relu_mlp(h, w_up[l], w_down[l])
    return x
```

```python
import functools

import jax
import jax.numpy as jnp
from jax import lax
from jax.experimental import pallas as pl
from jax.experimental.pallas import tpu as pltpu

F32 = jnp.float32
BF16 = jnp.bfloat16

D_MODEL = 1024
N_BRANCH = 4
BRANCH_W = 256
SGU_GROUPS = 4
SGU_GD = BRANCH_W // SGU_GROUPS
SGU_CHUNK = 128
ATT_HEADS = 4
ATT_HD = 64
IDX_HEADS = 8
IDX_HD = 64
TOPK_MAX = 256
ROPE_THETA = 10000.0
MLSTM_HEADS = 4
MLSTM_HD = 64
MLSTM_CHUNK = 128
CONV_WIDTH = 3
D_FF = 4 * D_MODEL
EPS = 1e-6

LANES = 128
SUBLANES = 8

COL_QI = 0
COL_AU = 512
COL_AV = 768
COL_Q = 1024
COL_K = 1280
COL_V = 1536
COL_CQ = 1792
COL_CK = 2048
COL_CV = 2304
COL_CO = 2560
COL_DB = 2816
COL_DC = 3072
COL_DX = 3328
COL_SMALL = 3584
COL_G = 4096
PROJ_W = 8192
SM_WI = IDX_HD
SM_CI = SM_WI + IDX_HEADS
SM_CF = SM_CI + MLSTM_HEADS

NEG_BIG = -1e30
M_INIT = -1e29
F32_MIN = -3.4028234663852886e38
F32_ABOVE_MIN = -3.4028232635611926e38
F32_TINY = 1.1754943508222875e-38
BISECT_PASSES_PER_ROUND = 4
MAX_BISECT_ROUNDS = 80
SCORE_UNROLL = 2


def _vmem_params(sem, mib):
    return pltpu.CompilerParams(dimension_semantics=sem, vmem_limit_bytes=mib * 1024 * 1024)


def _inproj_kernel(x_ref, g_ref, w_ref, o_ref, h_ref):
    @pl.when(pl.program_id(1) == 0)
    def _():
        x = x_ref[...]
        ms = jnp.mean(x * x, axis=-1, keepdims=True)
        h_ref[...] = (x * lax.rsqrt(ms + EPS) * g_ref[...]).astype(BF16)

    o_ref[...] = jnp.dot(h_ref[...], w_ref[...], preferred_element_type=F32)


def _inproj(x2, g, w, tm=1024, tn=512):
    t = x2.shape[0]
    tm = min(tm, t)
    return pl.pallas_call(
        _inproj_kernel,
        grid=(t // tm, PROJ_W // tn),
        in_specs=[
            pl.BlockSpec((tm, D_MODEL), lambda i, j: (i, 0)),
            pl.BlockSpec((1, D_MODEL), lambda i, j: (0, 0)),
            pl.BlockSpec((D_MODEL, tn), lambda i, j: (0, j)),
        ],
        out_specs=pl.BlockSpec((tm, tn), lambda i, j: (i, j)),
        out_shape=jax.ShapeDtypeStruct((t, PROJ_W), F32),
        scratch_shapes=[pltpu.VMEM((tm, D_MODEL), BF16)],
        compiler_params=_vmem_params(("parallel", "arbitrary"), 48),
        name="inproj",
    )(x2, g, w)


def _rope_slab(x, cos, sin_signed, first_half):
    x_hi = pltpu.roll(x, LANES - ATT_HD // 2, axis=1)
    x_lo = pltpu.roll(x, ATT_HD // 2, axis=1)
    rot = jnp.where(first_half, x_hi, x_lo)
    return x * cos + rot * sin_signed


def _head_sumsq(x2, lane, n_heads):
    out = []
    for h in range(n_heads):
        m = (lane >= h * ATT_HD) & (lane < (h + 1) * ATT_HD)
        out.append(jnp.sum(jnp.where(m, x2, 0.0), axis=-1, keepdims=True))
    return out


def _attprep_kernel(q_ref, k_ref, v_ref, qi_ref, sm_ref, cos_ref, sin_ref, qn_ref, kn_ref, kin_ref,
                    qo_ref, ko_ref, vo_ref, qio_ref, kio_ref):
    cos = cos_ref[...]
    sin = sin_ref[...]
    rows = cos.shape[0]
    lane = lax.broadcasted_iota(jnp.int32, (rows, LANES), 1)
    first_half = (lane % ATT_HD) < (ATT_HD // 2)
    head0 = lane < ATT_HD

    def norm_rope(ref, g_ref, scale):
        slabs = []
        for s in range(ref.shape[1] // LANES):
            x = ref[:, s * LANES:(s + 1) * LANES]
            ss = _head_sumsq(x * x, lane, 2)
            r0 = lax.rsqrt(ss[0] * (1.0 / ATT_HD) + EPS)
            r1 = lax.rsqrt(ss[1] * (1.0 / ATT_HD) + EPS)
            y = x * jnp.where(head0, r0, r1) * g_ref[...]
            y = _rope_slab(y, cos, sin, first_half)
            if scale != 1.0:
                y = y * scale
            slabs.append(y)
        return slabs

    for s, y in enumerate(norm_rope(q_ref, qn_ref, ATT_HD ** -0.5)):
        qo_ref[s * LANES:(s + 1) * LANES, :] = y.T.astype(BF16)
    for s, y in enumerate(norm_rope(k_ref, kn_ref, 1.0)):
        ko_ref[2 * s] = y[:, :ATT_HD].astype(BF16)
        ko_ref[2 * s + 1] = y[:, ATT_HD:].astype(BF16)
    vo_ref[...] = v_ref[...].T.astype(BF16)

    for s in range(qi_ref.shape[1] // LANES):
        y = _rope_slab(qi_ref[:, s * LANES:(s + 1) * LANES], cos, sin, first_half)
        qio_ref[s * LANES:(s + 1) * LANES, :] = (y * (IDX_HD ** -0.5)).T.astype(BF16)

    sm = sm_ref[...]
    mu = jnp.sum(jnp.where(head0, sm, 0.0), axis=-1, keepdims=True) * (1.0 / IDX_HD)
    d = sm - mu
    var = jnp.sum(jnp.where(head0, d * d, 0.0), axis=-1, keepdims=True) * (1.0 / IDX_HD)
    y = d * lax.rsqrt(var + EPS) * kin_ref[...]
    y = _rope_slab(y, cos, sin, first_half)
    kio_ref[...] = y[:, :IDX_HD].astype(BF16)


def _attprep(proj, cos, sin, qn, kn, kin, seq, tm=512):
    t = proj.shape[0]
    tm = min(tm, seq)
    npos = seq // tm

    def col(width, off):
        return pl.BlockSpec((tm, width), lambda i: (i, off // width))

    vec = pl.BlockSpec((1, LANES), lambda i: (0, 0))
    tab = pl.BlockSpec((tm, LANES), lambda i: (i % npos, 0))
    return pl.pallas_call(
        _attprep_kernel,
        grid=(t // tm,),
        in_specs=[col(256, COL_Q), col(256, COL_K), col(256, COL_V), col(512, COL_QI), col(128, COL_SMALL),
                  tab, tab, vec, vec, vec],
        out_specs=[pl.BlockSpec((256, tm), lambda i: (0, i)),
                   pl.BlockSpec((ATT_HEADS, tm, ATT_HD), lambda i: (0, i, 0)),
                   pl.BlockSpec((256, tm), lambda i: (0, i)), pl.BlockSpec((512, tm), lambda i: (0, i)),
                   pl.BlockSpec((tm, IDX_HD), lambda i: (i, 0))],
        out_shape=[jax.ShapeDtypeStruct((256, t), BF16), jax.ShapeDtypeStruct((ATT_HEADS, t, ATT_HD), BF16),
                   jax.ShapeDtypeStruct((256, t), BF16), jax.ShapeDtypeStruct((512, t), BF16),
                   jax.ShapeDtypeStruct((t, IDX_HD), BF16)],
        compiler_params=_vmem_params(("parallel",), 32),
        name="attprep",
    )(proj, proj, proj, proj, proj, cos, sin, qn, kn, kin)


def _dsa_kernel(qi_ref, sm_ref, q_ref, ki_ref, k_ref, vt_ref, o_ref, sc_ref, acc_ref, *, tq, n_sel, seq):
    i = pl.program_id(1)
    tk1 = min(128, tq)
    tk2 = tq
    tk3 = tq
    n_keys = (i + 1) * tq
    qpos = i * tq + lax.broadcasted_iota(jnp.int32, (1, tq), 1)

    wt = (sm_ref[...] * (IDX_HEADS ** -0.5)).T
    qi_h = [qi_ref[h * IDX_HD:(h + 1) * IDX_HD, :] for h in range(IDX_HEADS)]
    w_h = [wt[SM_WI + h:SM_WI + h + 1, :] for h in range(IDX_HEADS)]
    kidx1 = lax.broadcasted_iota(jnp.int32, (tk1, tq), 0)

    def score_chunk(c, amax):
        for u in range(SCORE_UNROLL):
            start = pl.multiple_of((c * SCORE_UNROLL + u) * tk1, tk1)
            kc = ki_ref[pl.ds(start, tk1), :]
            acc = jnp.zeros((tk1, tq), F32)
            for h in range(IDX_HEADS):
                s = jnp.dot(kc, qi_h[h], preferred_element_type=F32)
                acc = acc + w_h[h] * jnp.maximum(s, 0.0)
            amax = jnp.maximum(amax, jnp.max(jnp.abs(acc), axis=0, keepdims=True))
            sc_ref[pl.ds(start, tk1), :] = jnp.where(kidx1 + start <= qpos, acc, F32_MIN)
        return amax

    amax = lax.fori_loop(0, n_keys // (tk1 * SCORE_UNROLL), score_chunk, jnp.zeros((1, tq), F32))

    def count(pred_fn):
        def body(c, acc):
            start = pl.multiple_of(c * tk2, tk2)
            x = sc_ref[pl.ds(start, tk2), :]
            return acc + jnp.sum(pred_fn(x).reshape(tk2 // 32, 32, tq), axis=0)
        acc = lax.fori_loop(0, n_keys // tk2, body, jnp.zeros((32, tq), F32))
        return jnp.sum(acc, axis=0, keepdims=True)

    n_pos = count(lambda x: jnp.where(x > 0.0, 1.0, 0.0))
    n_nn = count(lambda x: jnp.where(x >= 0.0, 1.0, 0.0))
    n_causal = (qpos + 1).astype(F32)
    k_sel = jnp.float32(n_sel)
    take_all = n_causal <= k_sel
    at_zero = (n_pos < k_sel) & (n_nn >= k_sel)
    wide = 2.0 * amax + F32_TINY
    positive = n_pos >= k_sel
    lo0 = jnp.where(positive, 0.0, -wide)
    hi0 = jnp.where(positive, wide, 0.0)
    nhi0 = jnp.where(positive, 0.0, n_nn)
    thr0 = jnp.where(take_all, F32_ABOVE_MIN, 0.0)
    tie0 = jnp.where(at_zero & (n_nn > k_sel) & jnp.logical_not(take_all), 1.0, 0.0)
    need0 = k_sel - n_pos
    act0 = jnp.where(take_all | at_zero, 0.0, 1.0)

    def bisect_pass(state):
        lo, hi, n_hi, thr, tie, need, act = state
        cand = 0.5 * lo + 0.5 * hi
        cnt = count(lambda x: jnp.where(x >= cand, 1.0, 0.0))
        live = act > 0.0
        stuck = live & ((cand <= lo) | (cand >= hi))
        hit = live & jnp.logical_not(stuck) & (cnt == k_sel)
        up = live & jnp.logical_not(stuck) & (cnt > k_sel)
        down = live & jnp.logical_not(stuck) & (cnt < k_sel)
        thr = jnp.where(hit, cand, jnp.where(stuck, lo, thr))
        tie = jnp.where(stuck, 1.0, tie)
        need = jnp.where(stuck, k_sel - n_hi, need)
        lo = jnp.where(up, cand, lo)
        hi = jnp.where(down, cand, hi)
        n_hi = jnp.where(down, cnt, n_hi)
        act = jnp.where(hit | stuck, 0.0, act)
        return lo, hi, n_hi, thr, tie, need, act

    def search_cond(carry):
        it, state = carry
        return (it < MAX_BISECT_ROUNDS) & (jnp.max(state[6]) > 0.0)

    def search_body(carry):
        it, state = carry
        for _ in range(BISECT_PASSES_PER_ROUND):
            state = bisect_pass(state)
        return it + 1, state

    _, state = lax.while_loop(search_cond, search_body,
                              (jnp.int32(0), (lo0, hi0, nhi0, thr0, tie0, need0, act0)))
    thr, tie, need = state[3], state[4], state[5]

    tri = (lax.broadcasted_iota(jnp.int32, (tk2, tk2), 0) >= lax.broadcasted_iota(jnp.int32, (tk2, tk2), 1))
    tri = jnp.where(tri, 1.0, 0.0).astype(BF16)

    def drop_chunk(c, run):
        start = pl.multiple_of(c * tk2, tk2)
        x = sc_ref[pl.ds(start, tk2), :]
        eq = jnp.where(tie > 0.0, jnp.where(x == thr, 1.0, 0.0), 0.0)
        rank = run + jnp.dot(tri, eq.astype(BF16), preferred_element_type=F32)
        sc_ref[pl.ds(start, tk2), :] = jnp.where(eq > 0.0, jnp.where(rank > need, F32_MIN, x), x)
        return rank[tk2 - 1:tk2, :]

    @pl.when(jnp.max(tie) > 0.0)
    def _():
        lax.fori_loop(0, n_keys // tk2, drop_chunk, jnp.zeros((1, tq), F32))

    q_h = [q_ref[h * ATT_HD:(h + 1) * ATT_HD, :] for h in range(ATT_HEADS)]
    acc_ref[...] = jnp.zeros(acc_ref.shape, F32)

    def attn_chunk(c, carry):
        m_run, l_run = carry
        start = pl.multiple_of(c * tk3, tk3)
        sel = sc_ref[pl.ds(start, tk3), :] >= thr
        m_out, l_out = [], []
        for h in range(ATT_HEADS):
            s = jnp.dot(k_ref[h, pl.ds(start, tk3), :], q_h[h], preferred_element_type=F32)
            s = jnp.where(sel, s, NEG_BIG)
            m_new = jnp.maximum(m_run[h], jnp.max(s, axis=0, keepdims=True))
            alpha = jnp.exp(m_run[h] - m_new)
            p = jnp.exp(s - m_new)
            l_out.append(alpha * l_run[h] + jnp.sum(p, axis=0, keepdims=True))
            m_out.append(m_new)
            vt = vt_ref[h * ATT_HD:(h + 1) * ATT_HD, pl.ds(start, tk3)]
            pv = jnp.dot(vt, p.astype(BF16), preferred_element_type=F32)
            acc_ref[h * ATT_HD:(h + 1) * ATT_HD, :] = alpha * acc_ref[h * ATT_HD:(h + 1) * ATT_HD, :] + pv
        return tuple(m_out), tuple(l_out)

    init = (tuple(jnp.full((1, tq), M_INIT, F32) for _ in range(ATT_HEADS)),
            tuple(jnp.zeros((1, tq), F32) for _ in range(ATT_HEADS)))
    _, l_fin = lax.fori_loop(0, n_keys // tk3, attn_chunk, init)
    outs = [acc_ref[h * ATT_HD:(h + 1) * ATT_HD, :] / l_fin[h] for h in range(ATT_HEADS)]
    o_ref[...] = jnp.concatenate(outs, axis=0).T


def _dsa(qi, proj, q, ki, k, vt, batch, seq, tq=256):
    t = q.shape[1]
    tq = min(tq, seq)
    nq = seq // tq
    n_sel = min(TOPK_MAX, seq // 4)
    assert tq >= n_sel and seq % tq == 0 and tq % (128 * SCORE_UNROLL) == 0
    kern = functools.partial(_dsa_kernel, tq=tq, n_sel=n_sel, seq=seq)
    return pl.pallas_call(
        kern,
        grid=(batch, nq),
        in_specs=[
            pl.BlockSpec((512, tq), lambda b, i: (0, b * nq + i)),
            pl.BlockSpec((tq, LANES), lambda b, i: (b * nq + i, COL_SMALL // LANES)),
            pl.BlockSpec((256, tq), lambda b, i: (0, b * nq + i)),
            pl.BlockSpec((seq, IDX_HD), lambda b, i: (b, 0)),
            pl.BlockSpec((ATT_HEADS, seq, ATT_HD), lambda b, i: (0, b, 0)),
            pl.BlockSpec((256, seq), lambda b, i: (0, b)),
        ],
        out_specs=pl.BlockSpec((tq, 256), lambda b, i: (b * nq + i, 0)),
        out_shape=jax.ShapeDtypeStruct((t, 256), F32),
        scratch_shapes=[pltpu.VMEM((seq, tq), F32), pltpu.VMEM((ATT_HEADS * ATT_HD, tq), F32)],
        compiler_params=_vmem_params(("parallel", "arbitrary"), 48),
        name="dsa",
    )(qi, proj, q, ki, k, vt)


def _mlstm_kernel(q_ref, k_ref, v_ref, o_ref, sm_ref, gb_ref, ng_ref, y_ref, c_ref, m_ref):
    L = MLSTM_CHUNK
    hd = MLSTM_HD

    @pl.when(pl.program_id(1) == 0)
    def _():
        c_ref[...] = jnp.zeros_like(c_ref)
        m_ref[...] = jnp.zeros_like(m_ref)

    r_i = lax.broadcasted_iota(jnp.int32, (L, L), 0)
    c_i = lax.broadcasted_iota(jnp.int32, (L, L), 1)
    tril = r_i >= c_i
    g = sm_ref[...] + gb_ref[...]
    lf = jax.nn.log_sigmoid(g)
    bcum = jnp.dot(tril.astype(F32), lf, precision=lax.Precision.HIGHEST, preferred_element_type=F32)
    is_f = (c_i >= SM_CF) & (c_i < SM_CF + MLSTM_HEADS)
    xc = jnp.where(is_f, bcum, g)
    xr = xc.T
    ones = jnp.ones((L, hd), F32)
    outs = []
    for h in range(MLSTM_HEADS):
        qh = q_ref[:, h * hd:(h + 1) * hd].astype(BF16)
        kh = k_ref[:, h * hd:(h + 1) * hd] * (hd ** -0.5)
        v_ext = jnp.concatenate([v_ref[:, h * hd:(h + 1) * hd], ones], axis=1).astype(BF16)
        b_col = xc[:, SM_CF + h:SM_CF + h + 1]
        ig_col = xc[:, SM_CI + h:SM_CI + h + 1]
        b_row = xr[SM_CF + h:SM_CF + h + 1, :]
        ig_row = xr[SM_CI + h:SM_CI + h + 1, :]
        m_st = m_ref[h][0:1, 0:1]
        c_st = c_ref[h]
        dm = jnp.where(tril, b_col - b_row + ig_row, -jnp.inf)
        inter = b_col + m_st
        mj = jnp.maximum(inter, jnp.max(dm, axis=1, keepdims=True))
        qk = lax.dot_general(qh, kh.astype(BF16), (((1,), (1,)), ((), ())), preferred_element_type=F32)
        a = jnp.exp(dm - mj) * qk
        w_inter = jnp.exp(inter - mj)
        r = w_inter * jnp.dot(qh, c_st.astype(BF16), preferred_element_type=F32) \
            + jnp.dot(a.astype(BF16), v_ext, preferred_element_type=F32)
        num = r[:, :hd]
        den = r[:, hd:hd + 1]
        hout = num / jnp.maximum(jnp.abs(den), jnp.exp(-mj))
        b_last = b_col[L - 1:L, :]
        m_new = mj[L - 1:L, :]
        wk = jnp.exp(b_last - b_col + ig_col - m_new)
        decay = jnp.exp(b_last + m_st - m_new)
        kw = (kh * wk).astype(BF16)
        c_ref[h] = decay * c_st + lax.dot_general(kw, v_ext, (((0,), (0,)), ((), ())),
                                                  preferred_element_type=F32)
        m_ref[h] = jnp.broadcast_to(m_new, m_ref.shape[1:])
        ms = jnp.mean(hout * hout, axis=-1, keepdims=True)
        hn = hout * lax.rsqrt(ms + EPS) * ng_ref[:, h * hd:(h + 1) * hd]
        outs.append(jax.nn.sigmoid(o_ref[:, h * hd:(h + 1) * hd]) * hn)
    y_ref[...] = jnp.concatenate(outs, axis=1)


def _mlstm(proj, gbias, norm_g, batch, seq):
    t = proj.shape[0]
    L = MLSTM_CHUNK
    nc = seq // L

    def col(width, off):
        return pl.BlockSpec((L, width), lambda b, c: (b * nc + c, off // width))

    return pl.pallas_call(
        _mlstm_kernel,
        grid=(batch, nc),
        in_specs=[col(256, COL_CQ), col(256, COL_CK), col(256, COL_CV), col(256, COL_CO), col(128, COL_SMALL),
                  pl.BlockSpec((1, LANES), lambda b, c: (0, 0)),
                  pl.BlockSpec((1, 256), lambda b, c: (0, 0))],
        out_specs=pl.BlockSpec((L, 256), lambda b, c: (b * nc + c, 0)),
        out_shape=jax.ShapeDtypeStruct((t, 256), F32),
        scratch_shapes=[pltpu.VMEM((MLSTM_HEADS, MLSTM_HD, LANES), F32),
                        pltpu.VMEM((MLSTM_HEADS, SUBLANES, LANES), F32)],
        compiler_params=_vmem_params(("parallel", "arbitrary"), 32),
        name="mlstm",
    )(proj, proj, proj, proj, proj, gbias, norm_g)


def _merge_kernel(x_ref, uv_ref, db_ref, dc_ref, dx_ref, dcp_ref, dxp_ref, g_ref, yb_ref, yc_ref,
                  sn_ref, sw_ref, sb_ref, cw_ref, wb_ref, wo_ref, o_ref, *, tm, tiles_per_seq):
    i = pl.program_id(0)
    L = SGU_CHUNK
    u = jax.nn.gelu(uv_ref[:, :BRANCH_W])
    v = jax.nn.gelu(uv_ref[:, BRANCH_W:])
    ms = jnp.mean(v * v, axis=-1, keepdims=True)
    vn = (v * lax.rsqrt(ms + EPS) * sn_ref[...]).astype(BF16)
    r_i = lax.broadcasted_iota(jnp.int32, (L, L), 0)
    c_i = lax.broadcasted_iota(jnp.int32, (L, L), 1)
    tril = r_i >= c_i
    wm = [jnp.where(tril, sw_ref[g], 0.0).astype(BF16) for g in range(SGU_GROUPS)]
    sb = sb_ref[...]
    chunks = []
    for c in range(tm // L):
        parts = []
        for g in range(SGU_GROUPS):
            vg = vn[c * L:(c + 1) * L, g * SGU_GD:(g + 1) * SGU_GD]
            parts.append(jnp.dot(wm[g], vg, preferred_element_type=F32) + sb[:, g:g + 1])
        chunks.append(jnp.concatenate(parts, axis=1))
    y_a = u * jnp.concatenate(chunks, axis=0)

    z = dc_ref[...] * dx_ref[...]
    zp = dcp_ref[...] * dxp_ref[...]
    zp = jnp.where(i % tiles_per_seq == 0, jnp.zeros_like(zp), zp)
    zz = jnp.concatenate([zp, z], axis=0)
    cw = cw_ref[...]
    conv = cw[0:1, :] * zz[SUBLANES - 2:SUBLANES - 2 + tm, :] + cw[1:2, :] * zz[SUBLANES - 1:SUBLANES - 1 + tm, :] \
        + cw[2:3, :] * z
    y_d = db_ref[...] * conv

    ys = (y_a, yb_ref[...], yc_ref[...], y_d)
    merged = jnp.zeros((tm, D_MODEL), F32)
    for n in range(N_BRANCH):
        gate = jax.nn.sigmoid(g_ref[:, n * D_MODEL:(n + 1) * D_MODEL])
        merged = merged + gate * jnp.dot(ys[n].astype(BF16), wb_ref[n], preferred_element_type=F32)
    o_ref[...] = x_ref[...] + jnp.dot(merged.astype(BF16), wo_ref[...], preferred_element_type=F32)


def _merge(x2, proj, y_b, y_c, sgu_norm, sgu_w, sgu_bt, conv_w, w_branch, w_out, seq, tm=256):
    t = x2.shape[0]
    tm = min(tm, seq)
    tiles_per_seq = seq // tm
    rb = tm // SUBLANES

    def col(width, off):
        return pl.BlockSpec((tm, width), lambda i: (i, off // width))

    def prev(off):
        return pl.BlockSpec((SUBLANES, 256), lambda i: (jnp.maximum(i * rb - 1, 0), off // 256))

    def full(shape):
        return pl.BlockSpec(shape, lambda i: (0,) * len(shape))

    kern = functools.partial(_merge_kernel, tm=tm, tiles_per_seq=tiles_per_seq)
    return pl.pallas_call(
        kern,
        grid=(t // tm,),
        in_specs=[pl.BlockSpec((tm, D_MODEL), lambda i: (i, 0)),
                  col(512, COL_AU), col(256, COL_DB), col(256, COL_DC), col(256, COL_DX),
                  prev(COL_DC), prev(COL_DX), col(4096, COL_G),
                  pl.BlockSpec((tm, 256), lambda i: (i, 0)), pl.BlockSpec((tm, 256), lambda i: (i, 0)),
                  full((1, BRANCH_W)), full((SGU_GROUPS, SGU_CHUNK, SGU_CHUNK)), full((SGU_CHUNK, LANES)),
                  full((SUBLANES, BRANCH_W)), full((N_BRANCH, BRANCH_W, D_MODEL)), full((D_MODEL, D_MODEL))],
        out_specs=pl.BlockSpec((tm, D_MODEL), lambda i: (i, 0)),
        out_shape=jax.ShapeDtypeStruct((t, D_MODEL), F32),
        compiler_params=_vmem_params(("parallel",), 48),
        name="merge",
    )(x2, proj, proj, proj, proj, proj, proj, proj, y_b, y_c, sgu_norm, sgu_w, sgu_bt, conv_w, w_branch, w_out)


def _mlp_kernel(x_ref, g_ref, wu_ref, wd_ref, o_ref, h_ref):
    f = pl.program_id(1)

    @pl.when(f == 0)
    def _():
        x = x_ref[...]
        ms = jnp.mean(x * x, axis=-1, keepdims=True)
        h_ref[...] = (x * lax.rsqrt(ms + EPS) * g_ref[...]).astype(BF16)
        o_ref[...] = x

    up = jnp.maximum(jnp.dot(h_ref[...], wu_ref[...], preferred_element_type=F32), 0.0)
    o_ref[...] += jnp.dot((up * up).astype(BF16), wd_ref[...], preferred_element_type=F32)


def _mlp(x2, g, w_up, w_down, tm=1024, tf=1024):
    t = x2.shape[0]
    tm = min(tm, t)
    return pl.pallas_call(
        _mlp_kernel,
        grid=(t // tm, D_FF // tf),
        in_specs=[pl.BlockSpec((tm, D_MODEL), lambda i, f: (i, 0)),
                  pl.BlockSpec((1, D_MODEL), lambda i, f: (0, 0)),
                  pl.BlockSpec((D_MODEL, tf), lambda i, f: (0, f)),
                  pl.BlockSpec((tf, D_MODEL), lambda i, f: (f, 0))],
        out_specs=pl.BlockSpec((tm, D_MODEL), lambda i, f: (i, 0)),
        out_shape=jax.ShapeDtypeStruct((t, D_MODEL), F32),
        scratch_shapes=[pltpu.VMEM((tm, D_MODEL), BF16)],
        compiler_params=_vmem_params(("parallel", "arbitrary"), 48),
        name="mlp",
    )(x2, g, w_up, w_down)


def _relayout_w_in(w_in):
    sizes = (256, 256, 256, 256, 256, 512, 64, 8, 256, 256, 256, 256, 4, 4, 256, 256, 256, 4096)
    offs = [0]
    for s in sizes:
        offs.append(offs[-1] + s)
    (a_u, a_v, b_q, b_k, b_v, b_qi, b_ki, b_wi, c_q, c_k, c_v, c_o, c_i, c_f, d_b, d_c, d_x, g) = [
        w_in[:, offs[n]:offs[n + 1]] for n in range(len(sizes))]
    d = w_in.shape[0]
    small = jnp.concatenate([b_ki, b_wi, c_i, c_f, jnp.zeros((d, LANES - SM_CF - MLSTM_HEADS), w_in.dtype)], axis=1)
    pad = jnp.zeros((d, COL_G - COL_SMALL - LANES), w_in.dtype)
    out = jnp.concatenate([b_qi, a_u, a_v, b_q, b_k, b_v, c_q, c_k, c_v, c_o, d_b, d_c, d_x, small, pad, g], axis=1)
    assert out.shape[1] == PROJ_W
    return out.astype(BF16)


def _rope_tables(seq):
    half = ATT_HD // 2
    inv = jnp.float32(ROPE_THETA) ** (-jnp.arange(half, dtype=jnp.float32) * 2.0 / ATT_HD)
    ang = jnp.arange(seq, dtype=jnp.int32).astype(jnp.float32)[:, None] * inv[None, :]
    cos = jnp.cos(ang)
    sin = jnp.sin(ang)
    cos_t = jnp.concatenate([cos, cos, cos, cos], axis=1)
    sin_t = jnp.concatenate([-sin, sin, -sin, sin], axis=1)
    return cos_t, sin_t


def _tile2(v):
    return jnp.concatenate([v, v]).reshape(1, LANES)


def _forward(x, ln_mix, w_in, sgu_norm, sgu_w, sgu_b, q_norm, k_norm, kidx_norm, mlstm_i_bias, mlstm_f_bias,
             mlstm_norm, conv_w, w_branch, w_out, ln_mlp, w_up, w_down):
    batch, seq, d = x.shape
    depth = w_in.shape[0]
    x2 = x.reshape(batch * seq, d)
    cos_t, sin_t = _rope_tables(seq)
    for l in range(depth):
        proj = _inproj(x2, ln_mix[l].reshape(1, d), _relayout_w_in(w_in[l]))
        q, k, v, qi, ki = _attprep(proj, cos_t, sin_t, _tile2(q_norm[l]), _tile2(k_norm[l]),
                                   _tile2(kidx_norm[l]), seq)
        y_b = _dsa(qi, proj, q, ki, k, v, batch, seq)
        gbias = jnp.zeros((1, LANES), F32)
        gbias = gbias.at[0, SM_CI:SM_CI + MLSTM_HEADS].set(mlstm_i_bias[l])
        gbias = gbias.at[0, SM_CF:SM_CF + MLSTM_HEADS].set(mlstm_f_bias[l])
        y_c = _mlstm(proj, gbias, mlstm_norm[l].reshape(1, -1), batch, seq)
        sgu_bt = jnp.zeros((SGU_CHUNK, LANES), F32).at[:, :SGU_GROUPS].set(sgu_b[l].T)
        conv_p = jnp.zeros((SUBLANES, BRANCH_W), F32).at[:CONV_WIDTH].set(conv_w[l])
        x2 = _merge(x2, proj, y_b, y_c, sgu_norm[l].reshape(1, -1), sgu_w[l], sgu_bt, conv_p,
                    w_branch[l].astype(BF16), w_out[l].astype(BF16), seq)
        x2 = _mlp(x2, ln_mlp[l].reshape(1, d), w_up[l].astype(BF16), w_down[l].astype(BF16))
    return x2.reshape(batch, seq, d)


def kernel(x, ln_mix, w_in, sgu_norm, sgu_w, sgu_b, q_norm, k_norm, kidx_norm, mlstm_i_bias, mlstm_f_bias,
           mlstm_norm, conv_w, w_branch, w_out, ln_mlp, w_up, w_down):
    return _forward(x, ln_mix, w_in, sgu_norm, sgu_w, sgu_b, q_norm, k_norm, kidx_norm, mlstm_i_bias,
                    mlstm_f_bias, mlstm_norm, conv_w, w_branch, w_out, ln_mlp, w_up, w_down)
```

```python
import functools

import jax
import jax.numpy as jnp
from jax import lax
from jax.experimental import pallas as pl
from jax.experimental.pallas import tpu as pltpu

F32 = jnp.float32
BF16 = jnp.bfloat16

D_MODEL = 1024
N_BRANCH = 4
BRANCH_W = 256
SGU_GROUPS = 4
SGU_GD = BRANCH_W // SGU_GROUPS
SGU_CHUNK = 128
ATT_HEADS = 4
ATT_HD = 64
IDX_HEADS = 8
IDX_HD = 64
TOPK_MAX = 256
ROPE_THETA = 10000.0
MLSTM_HEADS = 4
MLSTM_HD = 64
MLSTM_CHUNK = 128
CONV_WIDTH = 3
D_FF = 4 * D_MODEL
EPS = 1e-6

LANES = 128
SUBLANES = 8
PREV_ROWS = 16

COL_QI = 0
COL_AU = 512
COL_AV = 768
COL_Q = 1024
COL_K = 1280
COL_V = 1536
COL_CQ = 1792
COL_CK = 2048
COL_CV = 2304
COL_CO = 2560
COL_DB = 2816
COL_DC = 3072
COL_DX = 3328
COL_SMALL = 3584
COL_G = 4096
PROJ_W = 8192
SM_WI = IDX_HD
SM_CI = SM_WI + IDX_HEADS
SM_CF = SM_CI + MLSTM_HEADS

LOG2E = 1.4426950408889634
NEG_BIG = -1e30
M_INIT = -1e29
F32_MIN = -3.4028234663852886e38
F32_ABOVE_MIN = -3.4028232635611926e38
F32_TINY = 1.1754943508222875e-38
BISECT_FIRST_PASSES = 16
BISECT_PASSES_PER_ROUND = 2
MAX_BISECT_ROUNDS = 160
SOFTMAX_BOUND_MAX = 60.0
SCORE_UNROLL = 2


def _vmem_params(sem, mib):
    return pltpu.CompilerParams(dimension_semantics=sem, vmem_limit_bytes=mib * 1024 * 1024)


def _inproj_kernel(x_ref, g_ref, w_ref, o_ref, sm_ref, h_ref, *, small_tile):
    j = pl.program_id(1)

    @pl.when(j == 0)
    def _():
        x = x_ref[...]
        ms = jnp.mean(x * x, axis=-1, keepdims=True)
        h_ref[...] = (x * lax.rsqrt(ms + EPS) * g_ref[...]).astype(BF16)

    res = jnp.dot(h_ref[...], w_ref[...], preferred_element_type=F32)
    o_ref[...] = res.astype(BF16)

    @pl.when(j == small_tile)
    def _():
        sm_ref[...] = res[:, :LANES]


def _inproj(x2, g, w, tm=1024, tn=512):
    t = x2.shape[0]
    tm = min(tm, t)
    assert COL_SMALL % tn == 0
    kern = functools.partial(_inproj_kernel, small_tile=COL_SMALL // tn)
    return pl.pallas_call(
        kern,
        grid=(t // tm, PROJ_W // tn),
        in_specs=[
            pl.BlockSpec((tm, D_MODEL), lambda i, j: (i, 0)),
            pl.BlockSpec((1, D_MODEL), lambda i, j: (0, 0)),
            pl.BlockSpec((D_MODEL, tn), lambda i, j: (0, j)),
        ],
        out_specs=[pl.BlockSpec((tm, tn), lambda i, j: (i, j)), pl.BlockSpec((tm, LANES), lambda i, j: (i, 0))],
        out_shape=[jax.ShapeDtypeStruct((t, PROJ_W), BF16), jax.ShapeDtypeStruct((t, LANES), F32)],
        scratch_shapes=[pltpu.VMEM((tm, D_MODEL), BF16)],
        compiler_params=_vmem_params(("parallel", "arbitrary"), 48),
        name="inproj",
    )(x2, g, w)


def _rope_slab(x, cos, sin_signed, first_half):
    x_hi = pltpu.roll(x, LANES - ATT_HD // 2, axis=1)
    x_lo = pltpu.roll(x, ATT_HD // 2, axis=1)
    rot = jnp.where(first_half, x_hi, x_lo)
    return x * cos + rot * sin_signed


def _head_sumsq(x2, lane, n_heads):
    out = []
    for h in range(n_heads):
        m = (lane >= h * ATT_HD) & (lane < (h + 1) * ATT_HD)
        out.append(jnp.sum(jnp.where(m, x2, 0.0), axis=-1, keepdims=True))
    return out


def _attprep_kernel(q_ref, k_ref, v_ref, qi_ref, sm_ref, cos_ref, sin_ref, qn_ref, kn_ref, kin_ref,
                    qo_ref, ko_ref, vo_ref, qio_ref, kio_ref):
    cos = cos_ref[...]
    sin = sin_ref[...]
    rows = cos.shape[0]
    lane = lax.broadcasted_iota(jnp.int32, (rows, LANES), 1)
    first_half = (lane % ATT_HD) < (ATT_HD // 2)
    head0 = lane < ATT_HD

    def norm_rope(ref, g_ref, scale):
        slabs = []
        for s in range(ref.shape[1] // LANES):
            x = ref[:, s * LANES:(s + 1) * LANES].astype(F32)
            ss = _head_sumsq(x * x, lane, 2)
            r0 = lax.rsqrt(ss[0] * (1.0 / ATT_HD) + EPS)
            r1 = lax.rsqrt(ss[1] * (1.0 / ATT_HD) + EPS)
            y = x * jnp.where(head0, r0, r1) * g_ref[...]
            y = _rope_slab(y, cos, sin, first_half)
            if scale != 1.0:
                y = y * scale
            slabs.append(y)
        return slabs

    for s, y in enumerate(norm_rope(q_ref, qn_ref, ATT_HD ** -0.5 * LOG2E)):
        qo_ref[s * LANES:(s + 1) * LANES, :] = y.T.astype(BF16)
    for s, y in enumerate(norm_rope(k_ref, kn_ref, 1.0)):
        ko_ref[2 * s] = y[:, :ATT_HD].astype(BF16)
        ko_ref[2 * s + 1] = y[:, ATT_HD:].astype(BF16)
    vo_ref[...] = v_ref[...].astype(F32).T.astype(BF16)

    for s in range(qi_ref.shape[1] // LANES):
        y = _rope_slab(qi_ref[:, s * LANES:(s + 1) * LANES].astype(F32), cos, sin, first_half)
        qio_ref[s * LANES:(s + 1) * LANES, :] = (y * (IDX_HD ** -0.5)).T.astype(BF16)

    sm = sm_ref[...]
    mu = jnp.sum(jnp.where(head0, sm, 0.0), axis=-1, keepdims=True) * (1.0 / IDX_HD)
    d = sm - mu
    var = jnp.sum(jnp.where(head0, d * d, 0.0), axis=-1, keepdims=True) * (1.0 / IDX_HD)
    y = d * lax.rsqrt(var + EPS) * kin_ref[...]
    y = _rope_slab(y, cos, sin, first_half)
    kio_ref[...] = y[:, :IDX_HD].astype(BF16)


def _attprep(proj, small, cos, sin, qn, kn, kin, seq, tm=512):
    t = proj.shape[0]
    tm = min(tm, seq)
    npos = seq // tm

    def col(width, off):
        return pl.BlockSpec((tm, width), lambda i: (i, off // width))

    vec = pl.BlockSpec((1, LANES), lambda i: (0, 0))
    tab = pl.BlockSpec((tm, LANES), lambda i: (i % npos, 0))
    return pl.pallas_call(
        _attprep_kernel,
        grid=(t // tm,),
        in_specs=[col(256, COL_Q), col(256, COL_K), col(256, COL_V), col(512, COL_QI),
                  pl.BlockSpec((tm, LANES), lambda i: (i, 0)), tab, tab, vec, vec, vec],
        out_specs=[pl.BlockSpec((256, tm), lambda i: (0, i)),
                   pl.BlockSpec((ATT_HEADS, tm, ATT_HD), lambda i: (0, i, 0)),
                   pl.BlockSpec((256, tm), lambda i: (0, i)), pl.BlockSpec((512, tm), lambda i: (0, i)),
                   pl.BlockSpec((tm, IDX_HD), lambda i: (i, 0))],
        out_shape=[jax.ShapeDtypeStruct((256, t), BF16), jax.ShapeDtypeStruct((ATT_HEADS, t, ATT_HD), BF16),
                   jax.ShapeDtypeStruct((256, t), BF16), jax.ShapeDtypeStruct((512, t), BF16),
                   jax.ShapeDtypeStruct((t, IDX_HD), BF16)],
        compiler_params=_vmem_params(("parallel",), 32),
        name="attprep",
    )(proj, proj, proj, proj, small, cos, sin, qn, kn, kin)


def _dsa_kernel(mb_ref, qi_ref, sm_ref, q_ref, ki_ref, k_ref, vt_ref, o_ref, sc_ref, acc_ref, *, tq, n_sel, seq):
    i = pl.program_id(1)
    tk1 = min(128, tq)
    tk2 = tq
    tk3 = tq
    n_keys = (i + 1) * tq
    qpos = i * tq + lax.broadcasted_iota(jnp.int32, (1, tq), 1)

    wt = (sm_ref[...] * (IDX_HEADS ** -0.5)).T
    qi_h = [qi_ref[h * IDX_HD:(h + 1) * IDX_HD, :] for h in range(IDX_HEADS)]
    w_h = [wt[SM_WI + h:SM_WI + h + 1, :] for h in range(IDX_HEADS)]
    kidx1 = lax.broadcasted_iota(jnp.int32, (tk1, tq), 0)

    def score_chunk(c, amax):
        for u in range(SCORE_UNROLL):
            start = pl.multiple_of((c * SCORE_UNROLL + u) * tk1, tk1)
            kc = ki_ref[pl.ds(start, tk1), :]
            acc = jnp.zeros((tk1, tq), F32)
            for h in range(IDX_HEADS):
                s = jnp.dot(kc, qi_h[h], preferred_element_type=F32)
                acc = acc + w_h[h] * jnp.maximum(s, 0.0)
            amax = jnp.maximum(amax, jnp.max(jnp.abs(acc), axis=0, keepdims=True))
            sc_ref[pl.ds(start, tk1), :] = jnp.where(kidx1 + start <= qpos, acc, F32_MIN)
        return amax

    amax = lax.fori_loop(0, n_keys // (tk1 * SCORE_UNROLL), score_chunk, jnp.zeros((1, tq), F32))

    def count(pred_fn):
        def body(c, acc):
            start = pl.multiple_of(c * tk2, tk2)
            x = sc_ref[pl.ds(start, tk2), :]
            return acc + jnp.sum(pred_fn(x).reshape(tk2 // 32, 32, tq), axis=0)
        acc = lax.fori_loop(0, n_keys // tk2, body, jnp.zeros((32, tq), F32))
        return jnp.sum(acc, axis=0, keepdims=True)

    n_pos = count(lambda x: jnp.where(x > 0.0, 1.0, 0.0))
    n_nn = count(lambda x: jnp.where(x >= 0.0, 1.0, 0.0))
    n_causal = (qpos + 1).astype(F32)
    k_sel = jnp.float32(n_sel)
    take_all = n_causal <= k_sel
    at_zero = (n_pos < k_sel) & (n_nn >= k_sel)
    wide = amax + amax * 2.0 ** -20 + F32_TINY
    positive = n_pos >= k_sel
    lo0 = jnp.where(positive, 0.0, -wide)
    hi0 = jnp.where(positive, wide, 0.0)
    nhi0 = jnp.where(positive, 0.0, n_nn)
    thr0 = jnp.where(take_all, F32_ABOVE_MIN, 0.0)
    tie0 = jnp.where(at_zero & (n_nn > k_sel) & jnp.logical_not(take_all), 1.0, 0.0)
    need0 = k_sel - n_pos
    act0 = jnp.where(take_all | at_zero, 0.0, 1.0)

    def bisect_pass(state):
        lo, hi, n_hi, thr, tie, need, act = state
        cand = 0.5 * lo + 0.5 * hi
        cnt = count(lambda x: jnp.where(x >= cand, 1.0, 0.0))
        live = act > 0.0
        stuck = live & ((cand <= lo) | (cand >= hi))
        hit = live & jnp.logical_not(stuck) & (cnt == k_sel)
        up = live & jnp.logical_not(stuck) & (cnt > k_sel)
        down = live & jnp.logical_not(stuck) & (cnt < k_sel)
        thr = jnp.where(hit, cand, jnp.where(stuck, lo, thr))
        tie = jnp.where(stuck, 1.0, tie)
        need = jnp.where(stuck, k_sel - n_hi, need)
        lo = jnp.where(up, cand, lo)
        hi = jnp.where(down, cand, hi)
        n_hi = jnp.where(down, cnt, n_hi)
        act = jnp.where(hit | stuck, 0.0, act)
        return lo, hi, n_hi, thr, tie, need, act

    def search_cond(carry):
        it, state = carry
        return (it < MAX_BISECT_ROUNDS) & (jnp.max(state[6]) > 0.0)

    def search_body(carry):
        it, state = carry
        for _ in range(BISECT_PASSES_PER_ROUND):
            state = bisect_pass(state)
        return it + 1, state

    state = lax.fori_loop(0, BISECT_FIRST_PASSES, lambda _, st: bisect_pass(st),
                          (lo0, hi0, nhi0, thr0, tie0, need0, act0))
    _, state = lax.while_loop(search_cond, search_body, (jnp.int32(0), state))
    thr, tie, need = state[3], state[4], state[5]

    tri = (lax.broadcasted_iota(jnp.int32, (tk2, tk2), 0) >= lax.broadcasted_iota(jnp.int32, (tk2, tk2), 1))
    tri = jnp.where(tri, 1.0, 0.0).astype(BF16)

    def drop_chunk(c, run):
        start = pl.multiple_of(c * tk2, tk2)
        x = sc_ref[pl.ds(start, tk2), :]
        eq = jnp.where(tie > 0.0, jnp.where(x == thr, 1.0, 0.0), 0.0)
        rank = run + jnp.dot(tri, eq.astype(BF16), preferred_element_type=F32)
        sc_ref[pl.ds(start, tk2), :] = jnp.where(eq > 0.0, jnp.where(rank > need, F32_MIN, x), x)
        return rank[tk2 - 1:tk2, :]

    @pl.when(jnp.max(tie) > 0.0)
    def _():
        lax.fori_loop(0, n_keys // tk2, drop_chunk, jnp.zeros((1, tq), F32))

    q_h = [q_ref[h * ATT_HD:(h + 1) * ATT_HD, :] for h in range(ATT_HEADS)]
    acc_ref[...] = jnp.zeros(acc_ref.shape, F32)
    bound = mb_ref[0, 0]

    def attn_bounded(c, l_run):
        start = pl.multiple_of(c * tk3, tk3)
        sel = sc_ref[pl.ds(start, tk3), :] >= thr
        s_all = [jnp.dot(k_ref[h, pl.ds(start, tk3), :], q_h[h], preferred_element_type=F32)
                 for h in range(ATT_HEADS)]
        l_out, ps = [], []
        for h in range(ATT_HEADS):
            p = jnp.where(sel, jnp.exp2(s_all[h] - bound), 0.0)
            l_out.append(l_run[h] + jnp.sum(p, axis=0, keepdims=True))
            ps.append(p.astype(BF16))
        for h in range(ATT_HEADS):
            vt = vt_ref[h * ATT_HD:(h + 1) * ATT_HD, pl.ds(start, tk3)]
            acc_ref[h * ATT_HD:(h + 1) * ATT_HD, :] += jnp.dot(vt, ps[h], preferred_element_type=F32)
        return tuple(l_out)

    def attn_online(c, carry):
        m_run, l_run = carry
        start = pl.multiple_of(c * tk3, tk3)
        sel = sc_ref[pl.ds(start, tk3), :] >= thr
        s_all = [jnp.dot(k_ref[h, pl.ds(start, tk3), :], q_h[h], preferred_element_type=F32)
                 for h in range(ATT_HEADS)]
        m_out, l_out, alphas, ps = [], [], [], []
        for h in range(ATT_HEADS):
            s = jnp.where(sel, s_all[h], NEG_BIG)
            m_new = jnp.maximum(m_run[h], jnp.max(s, axis=0, keepdims=True))
            alpha = jnp.exp2(m_run[h] - m_new)
            p = jnp.exp2(s - m_new)
            l_out.append(alpha * l_run[h] + jnp.sum(p, axis=0, keepdims=True))
            m_out.append(m_new)
            alphas.append(alpha)
            ps.append(p.astype(BF16))
        for h in range(ATT_HEADS):
            vt = vt_ref[h * ATT_HD:(h + 1) * ATT_HD, pl.ds(start, tk3)]
            pv = jnp.dot(vt, ps[h], preferred_element_type=F32)
            acc_ref[h * ATT_HD:(h + 1) * ATT_HD, :] = alphas[h] * acc_ref[h * ATT_HD:(h + 1) * ATT_HD, :] + pv
        return tuple(m_out), tuple(l_out)

    zeros = tuple(jnp.zeros((1, tq), F32) for _ in range(ATT_HEADS))

    def run_bounded():
        return lax.fori_loop(0, n_keys // tk3, attn_bounded, zeros)

    def run_online():
        init = (tuple(jnp.full((1, tq), M_INIT, F32) for _ in range(ATT_HEADS)), zeros)
        return lax.fori_loop(0, n_keys // tk3, attn_online, init)[1]

    l_fin = lax.cond(bound < SOFTMAX_BOUND_MAX, run_bounded, run_online)
    outs = [acc_ref[h * ATT_HD:(h + 1) * ATT_HD, :] / l_fin[h] for h in range(ATT_HEADS)]
    o_ref[...] = jnp.concatenate(outs, axis=0).T.astype(BF16)


def _dsa(logit_bound, qi, small, q, ki, k, vt, batch, seq, tq=256):
    t = q.shape[1]
    tq = min(tq, seq)
    nq = seq // tq
    n_sel = min(TOPK_MAX, seq // 4)
    assert tq >= n_sel and seq % tq == 0 and tq % (128 * SCORE_UNROLL) == 0
    kern = functools.partial(_dsa_kernel, tq=tq, n_sel=n_sel, seq=seq)
    return pl.pallas_call(
        kern,
        grid=(batch, nq),
        in_specs=[
            pl.BlockSpec(memory_space=pltpu.SMEM),
            pl.BlockSpec((512, tq), lambda b, i: (0, b * nq + i)),
            pl.BlockSpec((tq, LANES), lambda b, i: (b * nq + i, 0)),
            pl.BlockSpec((256, tq), lambda b, i: (0, b * nq + i)),
            pl.BlockSpec((seq, IDX_HD), lambda b, i: (b, 0)),
            pl.BlockSpec((ATT_HEADS, seq, ATT_HD), lambda b, i: (0, b, 0)),
            pl.BlockSpec((256, seq), lambda b, i: (0, b)),
        ],
        out_specs=pl.BlockSpec((tq, 256), lambda b, i: (b * nq + i, 0)),
        out_shape=jax.ShapeDtypeStruct((t, 256), BF16),
        scratch_shapes=[pltpu.VMEM((seq, tq), F32), pltpu.VMEM((ATT_HEADS * ATT_HD, tq), F32)],
        compiler_params=_vmem_params(("parallel", "arbitrary"), 48),
        name="dsa",
    )(logit_bound, qi, small, q, ki, k, vt)


def _mlstm_kernel(q_ref, k_ref, v_ref, o_ref, sm_ref, gb_ref, ng_ref, y_ref, c_ref, m_ref):
    L = MLSTM_CHUNK
    hd = MLSTM_HD

    @pl.when(pl.program_id(1) == 0)
    def _():
        c_ref[...] = jnp.zeros_like(c_ref)
        m_ref[...] = jnp.zeros_like(m_ref)

    r_i = lax.broadcasted_iota(jnp.int32, (L, L), 0)
    c_i = lax.broadcasted_iota(jnp.int32, (L, L), 1)
    tril = r_i >= c_i
    g = sm_ref[...] + gb_ref[...]
    lf = jax.nn.log_sigmoid(g)
    bcum = jnp.dot(tril.astype(F32), lf, precision=lax.Precision.HIGHEST, preferred_element_type=F32)
    is_f = (c_i >= SM_CF) & (c_i < SM_CF + MLSTM_HEADS)
    xc = jnp.where(is_f, bcum, g)
    xr = xc.T
    ones = jnp.ones((L, hd), BF16)
    outs = []
    for h in range(MLSTM_HEADS):
        qh = q_ref[:, h * hd:(h + 1) * hd]
        kh = k_ref[:, h * hd:(h + 1) * hd].astype(F32) * (hd ** -0.5)
        v_ext = jnp.concatenate([v_ref[:, h * hd:(h + 1) * hd], ones], axis=1)
        b_col = xc[:, SM_CF + h:SM_CF + h + 1]
        ig_col = xc[:, SM_CI + h:SM_CI + h + 1]
        b_row = xr[SM_CF + h:SM_CF + h + 1, :]
        ig_row = xr[SM_CI + h:SM_CI + h + 1, :]
        m_st = m_ref[h][0:1, 0:1]
        c_st = c_ref[h]
        dm = jnp.where(tril, b_col - b_row + ig_row, -jnp.inf)
        inter = b_col + m_st
        mj = jnp.maximum(inter, jnp.max(dm, axis=1, keepdims=True))
        qk = lax.dot_general(qh, kh.astype(BF16), (((1,), (1,)), ((), ())), preferred_element_type=F32)
        a = jnp.exp(dm - mj) * qk
        w_inter = jnp.exp(inter - mj)
        r = w_inter * jnp.dot(qh, c_st.astype(BF16), preferred_element_type=F32) \
            + jnp.dot(a.astype(BF16), v_ext, preferred_element_type=F32)
        num = r[:, :hd]
        den = r[:, hd:hd + 1]
        hout = num / jnp.maximum(jnp.abs(den), jnp.exp(-mj))
        b_last = b_col[L - 1:L, :]
        m_new = mj[L - 1:L, :]
        wk = jnp.exp(b_last - b_col + ig_col - m_new)
        decay = jnp.exp(b_last + m_st - m_new)
        kw = (kh * wk).astype(BF16)
        c_ref[h] = decay * c_st + lax.dot_general(kw, v_ext, (((0,), (0,)), ((), ())),
                                                  preferred_element_type=F32)
        m_ref[h] = jnp.broadcast_to(m_new, m_ref.shape[1:])
        ms = jnp.mean(hout * hout, axis=-1, keepdims=True)
        hn = hout * lax.rsqrt(ms + EPS) * ng_ref[:, h * hd:(h + 1) * hd]
        outs.append(jax.nn.sigmoid(o_ref[:, h * hd:(h + 1) * hd].astype(F32)) * hn)
    y_ref[...] = jnp.concatenate(outs, axis=1).astype(BF16)


def _mlstm(proj, small, gbias, norm_g, batch, seq):
    t = proj.shape[0]
    L = MLSTM_CHUNK
    nc = seq // L

    def col(width, off):
        return pl.BlockSpec((L, width), lambda b, c: (b * nc + c, off // width))

    return pl.pallas_call(
        _mlstm_kernel,
        grid=(batch, nc),
        in_specs=[col(256, COL_CQ), col(256, COL_CK), col(256, COL_CV), col(256, COL_CO),
                  pl.BlockSpec((L, LANES), lambda b, c: (b * nc + c, 0)),
                  pl.BlockSpec((1, LANES), lambda b, c: (0, 0)),
                  pl.BlockSpec((1, 256), lambda b, c: (0, 0))],
        out_specs=pl.BlockSpec((L, 256), lambda b, c: (b * nc + c, 0)),
        out_shape=jax.ShapeDtypeStruct((t, 256), BF16),
        scratch_shapes=[pltpu.VMEM((MLSTM_HEADS, MLSTM_HD, LANES), F32),
                        pltpu.VMEM((MLSTM_HEADS, SUBLANES, LANES), F32)],
        compiler_params=_vmem_params(("parallel", "arbitrary"), 32),
        name="mlstm",
    )(proj, proj, proj, proj, small, gbias, norm_g)


def _merge_kernel(x_ref, uv_ref, db_ref, dc_ref, dx_ref, dcp_ref, dxp_ref, g_ref, yb_ref, yc_ref,
                  sn_ref, sw_ref, sb_ref, cw_ref, wb_ref, wo_ref, o_ref, *, tm, tiles_per_seq):
    i = pl.program_id(0)
    L = SGU_CHUNK
    u = jax.nn.gelu(uv_ref[:, :BRANCH_W].astype(F32))
    v = jax.nn.gelu(uv_ref[:, BRANCH_W:].astype(F32))
    ms = jnp.mean(v * v, axis=-1, keepdims=True)
    vn = (v * lax.rsqrt(ms + EPS) * sn_ref[...]).astype(BF16)
    r_i = lax.broadcasted_iota(jnp.int32, (L, L), 0)
    c_i = lax.broadcasted_iota(jnp.int32, (L, L), 1)
    tril = r_i >= c_i
    wm = [jnp.where(tril, sw_ref[g], 0.0).astype(BF16) for g in range(SGU_GROUPS)]
    sb = sb_ref[...]
    chunks = []
    for c in range(tm // L):
        parts = []
        for g in range(SGU_GROUPS):
            vg = vn[c * L:(c + 1) * L, g * SGU_GD:(g + 1) * SGU_GD]
            parts.append(jnp.dot(wm[g], vg, preferred_element_type=F32) + sb[:, g:g + 1])
        chunks.append(jnp.concatenate(parts, axis=1))
    y_a = u * jnp.concatenate(chunks, axis=0)

    z = dc_ref[...].astype(F32) * dx_ref[...].astype(F32)
    zp = dcp_ref[...].astype(F32) * dxp_ref[...].astype(F32)
    zp = jnp.where(i % tiles_per_seq == 0, jnp.zeros_like(zp), zp)
    zz = jnp.concatenate([zp, z], axis=0)
    cw = cw_ref[...]
    conv = cw[0:1, :] * zz[PREV_ROWS - 2:PREV_ROWS - 2 + tm, :] + cw[1:2, :] * zz[PREV_ROWS - 1:PREV_ROWS - 1 + tm, :] \
        + cw[2:3, :] * z
    y_d = db_ref[...].astype(F32) * conv

    ys = (y_a, yb_ref[...], yc_ref[...], y_d)
    merged = jnp.zeros((tm, D_MODEL), F32)
    for n in range(N_BRANCH):
        gate = jax.nn.sigmoid(g_ref[:, n * D_MODEL:(n + 1) * D_MODEL].astype(F32))
        merged = merged + gate * jnp.dot(ys[n].astype(BF16), wb_ref[n], preferred_element_type=F32)
    o_ref[...] = x_ref[...] + jnp.dot(merged.astype(BF16), wo_ref[...], preferred_element_type=F32)


def _merge(x2, proj, y_b, y_c, sgu_norm, sgu_w, sgu_bt, conv_w, w_branch, w_out, seq, tm=256):
    t = x2.shape[0]
    tm = min(tm, seq)
    tiles_per_seq = seq // tm
    rb = tm // PREV_ROWS

    def col(width, off):
        return pl.BlockSpec((tm, width), lambda i: (i, off // width))

    def prev(off):
        return pl.BlockSpec((PREV_ROWS, 256), lambda i: (jnp.maximum(i * rb - 1, 0), off // 256))

    def full(shape):
        return pl.BlockSpec(shape, lambda i: (0,) * len(shape))

    kern = functools.partial(_merge_kernel, tm=tm, tiles_per_seq=tiles_per_seq)
    return pl.pallas_call(
        kern,
        grid=(t // tm,),
        in_specs=[pl.BlockSpec((tm, D_MODEL), lambda i: (i, 0)),
                  col(512, COL_AU), col(256, COL_DB), col(256, COL_DC), col(256, COL_DX),
                  prev(COL_DC), prev(COL_DX), col(4096, COL_G),
                  pl.BlockSpec((tm, 256), lambda i: (i, 0)), pl.BlockSpec((tm, 256), lambda i: (i, 0)),
                  full((1, BRANCH_W)), full((SGU_GROUPS, SGU_CHUNK, SGU_CHUNK)), full((SGU_CHUNK, LANES)),
                  full((SUBLANES, BRANCH_W)), full((N_BRANCH, BRANCH_W, D_MODEL)), full((D_MODEL, D_MODEL))],
        out_specs=pl.BlockSpec((tm, D_MODEL), lambda i: (i, 0)),
        out_shape=jax.ShapeDtypeStruct((t, D_MODEL), F32),
        compiler_params=_vmem_params(("parallel",), 48),
        name="merge",
    )(x2, proj, proj, proj, proj, proj, proj, proj, y_b, y_c, sgu_norm, sgu_w, sgu_bt, conv_w, w_branch, w_out)


def _mlp_kernel(x_ref, g_ref, wu_ref, wd_ref, o_ref, h_ref):
    f = pl.program_id(1)

    @pl.when(f == 0)
    def _():
        x = x_ref[...]
        ms = jnp.mean(x * x, axis=-1, keepdims=True)
        h_ref[...] = (x * lax.rsqrt(ms + EPS) * g_ref[...]).astype(BF16)
        o_ref[...] = x

    up = jnp.maximum(jnp.dot(h_ref[...], wu_ref[...], preferred_element_type=F32), 0.0)
    o_ref[...] += jnp.dot((up * up).astype(BF16), wd_ref[...], preferred_element_type=F32)


def _mlp(x2, g, w_up, w_down, tm=1024, tf=1024):
    t = x2.shape[0]
    tm = min(tm, t)
    return pl.pallas_call(
        _mlp_kernel,
        grid=(t // tm, D_FF // tf),
        in_specs=[pl.BlockSpec((tm, D_MODEL), lambda i, f: (i, 0)),
                  pl.BlockSpec((1, D_MODEL), lambda i, f: (0, 0)),
                  pl.BlockSpec((D_MODEL, tf), lambda i, f: (0, f)),
                  pl.BlockSpec((tf, D_MODEL), lambda i, f: (f, 0))],
        out_specs=pl.BlockSpec((tm, D_MODEL), lambda i, f: (i, 0)),
        out_shape=jax.ShapeDtypeStruct((t, D_MODEL), F32),
        scratch_shapes=[pltpu.VMEM((tm, D_MODEL), BF16)],
        compiler_params=_vmem_params(("parallel", "arbitrary"), 48),
        name="mlp",
    )(x2, g, w_up, w_down)


def _relayout_w_in(w_in):
    sizes = (256, 256, 256, 256, 256, 512, 64, 8, 256, 256, 256, 256, 4, 4, 256, 256, 256, 4096)
    offs = [0]
    for s in sizes:
        offs.append(offs[-1] + s)
    (a_u, a_v, b_q, b_k, b_v, b_qi, b_ki, b_wi, c_q, c_k, c_v, c_o, c_i, c_f, d_b, d_c, d_x, g) = [
        w_in[:, offs[n]:offs[n + 1]] for n in range(len(sizes))]
    d = w_in.shape[0]
    small = jnp.concatenate([b_ki, b_wi, c_i, c_f, jnp.zeros((d, LANES - SM_CF - MLSTM_HEADS), w_in.dtype)], axis=1)
    pad = jnp.zeros((d, COL_G - COL_SMALL - LANES), w_in.dtype)
    out = jnp.concatenate([b_qi, a_u, a_v, b_q, b_k, b_v, c_q, c_k, c_v, c_o, d_b, d_c, d_x, small, pad, g], axis=1)
    assert out.shape[1] == PROJ_W
    return out.astype(BF16)


def _rope_tables(seq):
    half = ATT_HD // 2
    inv = jnp.float32(ROPE_THETA) ** (-jnp.arange(half, dtype=jnp.float32) * 2.0 / ATT_HD)
    ang = jnp.arange(seq, dtype=jnp.int32).astype(jnp.float32)[:, None] * inv[None, :]
    cos = jnp.cos(ang)
    sin = jnp.sin(ang)
    cos_t = jnp.concatenate([cos, cos, cos, cos], axis=1)
    sin_t = jnp.concatenate([-sin, sin, -sin, sin], axis=1)
    return cos_t, sin_t


def _tile2(v):
    return jnp.concatenate([v, v]).reshape(1, LANES)


def _forward(x, ln_mix, w_in, sgu_norm, sgu_w, sgu_b, q_norm, k_norm, kidx_norm, mlstm_i_bias, mlstm_f_bias,
             mlstm_norm, conv_w, w_branch, w_out, ln_mlp, w_up, w_down):
    batch, seq, d = x.shape
    depth = w_in.shape[0]
    x2 = x.reshape(batch * seq, d)
    cos_t, sin_t = _rope_tables(seq)
    for l in range(depth):
        proj, small = _inproj(x2, ln_mix[l].reshape(1, d), _relayout_w_in(w_in[l]))
        q, k, v, qi, ki = _attprep(proj, small, cos_t, sin_t, _tile2(q_norm[l]), _tile2(k_norm[l]),
                                   _tile2(kidx_norm[l]), seq)
        logit_bound = (ATT_HD ** 0.5 * LOG2E * jnp.max(jnp.abs(q_norm[l])) * jnp.max(jnp.abs(k_norm[l]))).reshape(1, 1)
        y_b = _dsa(logit_bound, qi, small, q, ki, k, v, batch, seq)
        gbias = jnp.zeros((1, LANES), F32)
        gbias = gbias.at[0, SM_CI:SM_CI + MLSTM_HEADS].set(mlstm_i_bias[l])
        gbias = gbias.at[0, SM_CF:SM_CF + MLSTM_HEADS].set(mlstm_f_bias[l])
        y_c = _mlstm(proj, small, gbias, mlstm_norm[l].reshape(1, -1), batch, seq)
        sgu_bt = jnp.zeros((SGU_CHUNK, LANES), F32).at[:, :SGU_GROUPS].set(sgu_b[l].T)
        conv_p = jnp.zeros((SUBLANES, BRANCH_W), F32).at[:CONV_WIDTH].set(conv_w[l])
        x2 = _merge(x2, proj, y_b, y_c, sgu_norm[l].reshape(1, -1), sgu_w[l], sgu_bt, conv_p,
                    w_branch[l].astype(BF16), w_out[l].astype(BF16), seq)
        x2 = _mlp(x2, ln_mlp[l].reshape(1, d), w_up[l].astype(BF16), w_down[l].astype(BF16))
    return x2.reshape(batch, seq, d)


def kernel(x, ln_mix, w_in, sgu_norm, sgu_w, sgu_b, q_norm, k_norm, kidx_norm, mlstm_i_bias, mlstm_f_bias,
           mlstm_norm, conv_w, w_branch, w_out, ln_mlp, w_up, w_down):
    return _forward(x, ln_mix, w_in, sgu_norm, sgu_w, sgu_b, q_norm, k_norm, kidx_norm, mlstm_i_bias,
                    mlstm_f_bias, mlstm_norm, conv_w, w_branch, w_out, ln_mlp, w_up, w_down)
```

```python
import functools

import jax
import jax.numpy as jnp
from jax import lax
from jax.experimental import pallas as pl
from jax.experimental.pallas import tpu as pltpu

F32 = jnp.float32
BF16 = jnp.bfloat16

D_MODEL = 1024
N_BRANCH = 4
BRANCH_W = 256
SGU_GROUPS = 4
SGU_GD = BRANCH_W // SGU_GROUPS
SGU_CHUNK = 128
ATT_HEADS = 4
ATT_HD = 64
IDX_HEADS = 8
IDX_HD = 64
TOPK_MAX = 256
ROPE_THETA = 10000.0
MLSTM_HEADS = 4
MLSTM_HD = 64
MLSTM_CHUNK = 128
CONV_WIDTH = 3
D_FF = 4 * D_MODEL
EPS = 1e-6

LANES = 128
SUBLANES = 8
PREV_ROWS = 16

COL_QI = 0
COL_AU = 512
COL_AV = 768
COL_Q = 1024
COL_K = 1280
COL_V = 1536
COL_CQ = 1792
COL_CK = 2048
COL_CV = 2304
COL_CO = 2560
COL_DB = 2816
COL_DC = 3072
COL_DX = 3328
COL_SMALL = 3584
COL_G = 4096
PROJ_W = 8192
SM_WI = IDX_HD
SM_CI = SM_WI + IDX_HEADS
SM_CF = SM_CI + MLSTM_HEADS

LOG2E = 1.4426950408889634
NEG_BIG = -1e30
M_INIT = -1e29
F32_MIN = -3.4028234663852886e38
F32_ABOVE_MIN = -3.4028232635611926e38
F32_TINY = 1.1754943508222875e-38
BISECT_FIRST_PASSES = 16
BISECT_PASSES_PER_ROUND = 2
MAX_BISECT_ROUNDS = 160
SOFTMAX_BOUND_MAX = 60.0
SCORE_UNROLL = 2
MLSTM_SEQS_PER_STEP = 2


def _vmem_params(sem, mib):
    return pltpu.CompilerParams(dimension_semantics=sem, vmem_limit_bytes=mib * 1024 * 1024)


def _inproj_kernel(x_ref, g_ref, w_ref, o_ref, sm_ref, h_ref, *, small_tile, small_off):
    j = pl.program_id(1)

    @pl.when(j == 0)
    def _():
        x = x_ref[...]
        ms = jnp.mean(x * x, axis=-1, keepdims=True)
        h_ref[...] = (x * lax.rsqrt(ms + EPS) * g_ref[...]).astype(BF16)

    res = jnp.dot(h_ref[...], w_ref[...], preferred_element_type=F32)
    o_ref[...] = res.astype(BF16)

    @pl.when(j == small_tile)
    def _():
        sm_ref[...] = res[:, small_off:small_off + LANES]


def _inproj(x2, g, w, tm=2048, tn=1024):
    t = x2.shape[0]
    tm = min(tm, t)
    kern = functools.partial(_inproj_kernel, small_tile=COL_SMALL // tn, small_off=COL_SMALL % tn)
    return pl.pallas_call(
        kern,
        grid=(t // tm, PROJ_W // tn),
        in_specs=[
            pl.BlockSpec((tm, D_MODEL), lambda i, j: (i, 0)),
            pl.BlockSpec((1, D_MODEL), lambda i, j: (0, 0)),
            pl.BlockSpec((D_MODEL, tn), lambda i, j: (0, j)),
        ],
        out_specs=[pl.BlockSpec((tm, tn), lambda i, j: (i, j)), pl.BlockSpec((tm, LANES), lambda i, j: (i, 0))],
        out_shape=[jax.ShapeDtypeStruct((t, PROJ_W), BF16), jax.ShapeDtypeStruct((t, LANES), F32)],
        scratch_shapes=[pltpu.VMEM((tm, D_MODEL), BF16)],
        compiler_params=_vmem_params(("parallel", "arbitrary"), 56),
        name="inproj",
    )(x2, g, w)


def _rope_slab(x, cos, sin_signed, first_half):
    x_hi = pltpu.roll(x, LANES - ATT_HD // 2, axis=1)
    x_lo = pltpu.roll(x, ATT_HD // 2, axis=1)
    rot = jnp.where(first_half, x_hi, x_lo)
    return x * cos + rot * sin_signed


def _head_sumsq(x2, lane, n_heads):
    out = []
    for h in range(n_heads):
        m = (lane >= h * ATT_HD) & (lane < (h + 1) * ATT_HD)
        out.append(jnp.sum(jnp.where(m, x2, 0.0), axis=-1, keepdims=True))
    return out


def _attprep_kernel(q_ref, k_ref, v_ref, qi_ref, sm_ref, cos_ref, sin_ref, qn_ref, kn_ref, kin_ref,
                    qo_ref, ko_ref, vo_ref, qio_ref, kio_ref):
    cos = cos_ref[...]
    sin = sin_ref[...]
    rows = cos.shape[0]
    lane = lax.broadcasted_iota(jnp.int32, (rows, LANES), 1)
    first_half = (lane % ATT_HD) < (ATT_HD // 2)
    head0 = lane < ATT_HD

    def norm_rope(ref, g_ref, scale):
        slabs = []
        for s in range(ref.shape[1] // LANES):
            x = ref[:, s * LANES:(s + 1) * LANES].astype(F32)
            ss = _head_sumsq(x * x, lane, 2)
            r0 = lax.rsqrt(ss[0] * (1.0 / ATT_HD) + EPS)
            r1 = lax.rsqrt(ss[1] * (1.0 / ATT_HD) + EPS)
            y = x * jnp.where(head0, r0, r1) * g_ref[...]
            y = _rope_slab(y, cos, sin, first_half)
            if scale != 1.0:
                y = y * scale
            slabs.append(y)
        return slabs

    for s, y in enumerate(norm_rope(q_ref, qn_ref, ATT_HD ** -0.5 * LOG2E)):
        qo_ref[s * LANES:(s + 1) * LANES, :] = y.T.astype(BF16)
    for s, y in enumerate(norm_rope(k_ref, kn_ref, 1.0)):
        ko_ref[2 * s] = y[:, :ATT_HD].astype(BF16)
        ko_ref[2 * s + 1] = y[:, ATT_HD:].astype(BF16)
    vo_ref[...] = v_ref[...].astype(F32).T.astype(BF16)

    for s in range(qi_ref.shape[1] // LANES):
        y = _rope_slab(qi_ref[:, s * LANES:(s + 1) * LANES].astype(F32), cos, sin, first_half)
        qio_ref[s * LANES:(s + 1) * LANES, :] = (y * (IDX_HD ** -0.5)).T.astype(BF16)

    sm = sm_ref[...]
    mu = jnp.sum(jnp.where(head0, sm, 0.0), axis=-1, keepdims=True) * (1.0 / IDX_HD)
    d = sm - mu
    var = jnp.sum(jnp.where(head0, d * d, 0.0), axis=-1, keepdims=True) * (1.0 / IDX_HD)
    y = d * lax.rsqrt(var + EPS) * kin_ref[...]
    y = _rope_slab(y, cos, sin, first_half)
    kio_ref[...] = y[:, :IDX_HD].astype(BF16)


def _attprep(proj, small, cos, sin, qn, kn, kin, seq, tm=512):
    t = proj.shape[0]
    tm = min(tm, seq)
    npos = seq // tm

    def col(width, off):
        return pl.BlockSpec((tm, width), lambda i: (i, off // width))

    vec = pl.BlockSpec((1, LANES), lambda i: (0, 0))
    tab = pl.BlockSpec((tm, LANES), lambda i: (i % npos, 0))
    return pl.pallas_call(
        _attprep_kernel,
        grid=(t // tm,),
        in_specs=[col(256, COL_Q), col(256, COL_K), col(256, COL_V), col(512, COL_QI),
                  pl.BlockSpec((tm, LANES), lambda i: (i, 0)), tab, tab, vec, vec, vec],
        out_specs=[pl.BlockSpec((256, tm), lambda i: (0, i)),
                   pl.BlockSpec((ATT_HEADS, tm, ATT_HD), lambda i: (0, i, 0)),
                   pl.BlockSpec((256, tm), lambda i: (0, i)), pl.BlockSpec((512, tm), lambda i: (0, i)),
                   pl.BlockSpec((tm, IDX_HD), lambda i: (i, 0))],
        out_shape=[jax.ShapeDtypeStruct((256, t), BF16), jax.ShapeDtypeStruct((ATT_HEADS, t, ATT_HD), BF16),
                   jax.ShapeDtypeStruct((256, t), BF16), jax.ShapeDtypeStruct((512, t), BF16),
                   jax.ShapeDtypeStruct((t, IDX_HD), BF16)],
        compiler_params=_vmem_params(("parallel",), 32),
        name="attprep",
    )(proj, proj, proj, proj, small, cos, sin, qn, kn, kin)


def _dsa_kernel(mb_ref, qi_ref, sm_ref, q_ref, ki_ref, k_ref, vt_ref, o_ref, sc_ref, acc_ref, s0_ref, s1_ref,
                *, tq, n_sel, seq):
    i = pl.program_id(1)
    tk1 = min(128, tq)
    tk2 = tq
    tk3 = tq
    n_keys = (i + 1) * tq
    qpos = i * tq + lax.broadcasted_iota(jnp.int32, (1, tq), 1)

    wt = (sm_ref[...] * (IDX_HEADS ** -0.5)).T
    qi_h = [qi_ref[h * IDX_HD:(h + 1) * IDX_HD, :] for h in range(IDX_HEADS)]
    w_h = [wt[SM_WI + h:SM_WI + h + 1, :] for h in range(IDX_HEADS)]
    kidx1 = lax.broadcasted_iota(jnp.int32, (tk1, tq), 0)

    def score_chunk(c, amax):
        for u in range(SCORE_UNROLL):
            start = pl.multiple_of((c * SCORE_UNROLL + u) * tk1, tk1)
            kc = ki_ref[pl.ds(start, tk1), :]
            acc = jnp.zeros((tk1, tq), F32)
            for h in range(IDX_HEADS):
                s = jnp.dot(kc, qi_h[h], preferred_element_type=F32)
                acc = acc + w_h[h] * jnp.maximum(s, 0.0)
            amax = jnp.maximum(amax, jnp.max(jnp.abs(acc), axis=0, keepdims=True))
            sc_ref[pl.ds(start, tk1), :] = jnp.where(kidx1 + start <= qpos, acc, F32_MIN)
        return amax

    amax = lax.fori_loop(0, n_keys // (tk1 * SCORE_UNROLL), score_chunk, jnp.zeros((1, tq), F32))

    def count(pred_fn):
        def body(c, acc):
            start = pl.multiple_of(c * tk2, tk2)
            x = sc_ref[pl.ds(start, tk2), :]
            return acc + jnp.sum(pred_fn(x).reshape(tk2 // 32, 32, tq), axis=0)
        acc = lax.fori_loop(0, n_keys // tk2, body, jnp.zeros((32, tq), F32))
        return jnp.sum(acc, axis=0, keepdims=True)

    n_pos = count(lambda x: jnp.where(x > 0.0, 1.0, 0.0))
    n_nn = count(lambda x: jnp.where(x >= 0.0, 1.0, 0.0))
    n_causal = (qpos + 1).astype(F32)
    k_sel = jnp.float32(n_sel)
    take_all = n_causal <= k_sel
    at_zero = (n_pos < k_sel) & (n_nn >= k_sel)
    wide = amax + amax * 2.0 ** -20 + F32_TINY
    positive = n_pos >= k_sel
    lo0 = jnp.where(positive, 0.0, -wide)
    hi0 = jnp.where(positive, wide, 0.0)
    nhi0 = jnp.where(positive, 0.0, n_nn)
    thr0 = jnp.where(take_all, F32_ABOVE_MIN, 0.0)
    tie0 = jnp.where(at_zero & (n_nn > k_sel) & jnp.logical_not(take_all), 1.0, 0.0)
    need0 = k_sel - n_pos
    act0 = jnp.where(take_all | at_zero, 0.0, 1.0)

    def bisect_pass(state):
        lo, hi, n_hi, thr, tie, need, act = state
        cand = 0.5 * lo + 0.5 * hi
        cnt = count(lambda x: jnp.where(x >= cand, 1.0, 0.0))
        live = act > 0.0
        stuck = live & ((cand <= lo) | (cand >= hi))
        hit = live & jnp.logical_not(stuck) & (cnt == k_sel)
        up = live & jnp.logical_not(stuck) & (cnt > k_sel)
        down = live & jnp.logical_not(stuck) & (cnt < k_sel)
        thr = jnp.where(hit, cand, jnp.where(stuck, lo, thr))
        tie = jnp.where(stuck, 1.0, tie)
        need = jnp.where(stuck, k_sel - n_hi, need)
        lo = jnp.where(up, cand, lo)
        hi = jnp.where(down, cand, hi)
        n_hi = jnp.where(down, cnt, n_hi)
        act = jnp.where(hit | stuck, 0.0, act)
        return lo, hi, n_hi, thr, tie, need, act

    def search_cond(carry):
        it, state = carry
        return (it < MAX_BISECT_ROUNDS) & (jnp.max(state[6]) > 0.0)

    def search_body(carry):
        it, state = carry
        for _ in range(BISECT_PASSES_PER_ROUND):
            state = bisect_pass(state)
        return it + 1, state

    state = lax.fori_loop(0, BISECT_FIRST_PASSES, lambda _, st: bisect_pass(st),
                          (lo0, hi0, nhi0, thr0, tie0, need0, act0))
    _, state = lax.while_loop(search_cond, search_body, (jnp.int32(0), state))
    thr, tie, need = state[3], state[4], state[5]

    tri = (lax.broadcasted_iota(jnp.int32, (tk2, tk2), 0) >= lax.broadcasted_iota(jnp.int32, (tk2, tk2), 1))
    tri = jnp.where(tri, 1.0, 0.0).astype(BF16)

    def drop_chunk(c, run):
        start = pl.multiple_of(c * tk2, tk2)
        x = sc_ref[pl.ds(start, tk2), :]
        eq = jnp.where(tie > 0.0, jnp.where(x == thr, 1.0, 0.0), 0.0)
        rank = run + jnp.dot(tri, eq.astype(BF16), preferred_element_type=F32)
        sc_ref[pl.ds(start, tk2), :] = jnp.where(eq > 0.0, jnp.where(rank > need, F32_MIN, x), x)
        return rank[tk2 - 1:tk2, :]

    @pl.when(jnp.max(tie) > 0.0)
    def _():
        lax.fori_loop(0, n_keys // tk2, drop_chunk, jnp.zeros((1, tq), F32))

    q_h = [q_ref[h * ATT_HD:(h + 1) * ATT_HD, :] for h in range(ATT_HEADS)]
    acc_ref[...] = jnp.zeros(acc_ref.shape, F32)
    bound = mb_ref[0, 0]

    def scores_into(buf_ref, c):
        start = pl.multiple_of(c * tk3, tk3)
        for h in range(ATT_HEADS):
            buf_ref[h] = jnp.dot(k_ref[h, pl.ds(start, tk3), :], q_h[h], preferred_element_type=F32)

    def attend_bounded(buf_ref, c, live, l_run):
        start = pl.multiple_of(c * tk3, tk3)
        sel = (sc_ref[pl.ds(start, tk3), :] >= thr) & live
        l_out = []
        for h in range(ATT_HEADS):
            p = jnp.where(sel, jnp.exp2(buf_ref[h] - bound), 0.0)
            l_out.append(l_run[h] + jnp.sum(p, axis=0, keepdims=True))
            vt = vt_ref[h * ATT_HD:(h + 1) * ATT_HD, pl.ds(start, tk3)]
            acc_ref[h * ATT_HD:(h + 1) * ATT_HD, :] += jnp.dot(vt, p.astype(BF16), preferred_element_type=F32)
        return tuple(l_out)

    n_chunks3 = n_keys // tk3
    last = n_chunks3 - 1

    def attn_bounded_pair(j, l_run):
        c_a = 2 * j
        c_b = jnp.minimum(c_a + 1, last)
        scores_into(s1_ref, c_b)
        l_run = attend_bounded(s0_ref, c_a, True, l_run)
        scores_into(s0_ref, jnp.minimum(c_a + 2, last))
        return attend_bounded(s1_ref, c_b, c_a + 1 <= last, l_run)

    def attn_online(c, carry):
        m_run, l_run = carry
        start = pl.multiple_of(c * tk3, tk3)
        sel = sc_ref[pl.ds(start, tk3), :] >= thr
        s_all = [jnp.dot(k_ref[h, pl.ds(start, tk3), :], q_h[h], preferred_element_type=F32)
                 for h in range(ATT_HEADS)]
        m_out, l_out, alphas, ps = [], [], [], []
        for h in range(ATT_HEADS):
            s = jnp.where(sel, s_all[h], NEG_BIG)
            m_new = jnp.maximum(m_run[h], jnp.max(s, axis=0, keepdims=True))
            alpha = jnp.exp2(m_run[h] - m_new)
            p = jnp.exp2(s - m_new)
            l_out.append(alpha * l_run[h] + jnp.sum(p, axis=0, keepdims=True))
            m_out.append(m_new)
            alphas.append(alpha)
            ps.append(p.astype(BF16))
        for h in range(ATT_HEADS):
            vt = vt_ref[h * ATT_HD:(h + 1) * ATT_HD, pl.ds(start, tk3)]
            pv = jnp.dot(vt, ps[h], preferred_element_type=F32)
            acc_ref[h * ATT_HD:(h + 1) * ATT_HD, :] = alphas[h] * acc_ref[h * ATT_HD:(h + 1) * ATT_HD, :] + pv
        return tuple(m_out), tuple(l_out)

    zeros = tuple(jnp.zeros((1, tq), F32) for _ in range(ATT_HEADS))

    def run_bounded():
        scores_into(s0_ref, 0)
        return lax.fori_loop(0, (n_chunks3 + 1) // 2, attn_bounded_pair, zeros)

    def run_online():
        init = (tuple(jnp.full((1, tq), M_INIT, F32) for _ in range(ATT_HEADS)), zeros)
        return lax.fori_loop(0, n_keys // tk3, attn_online, init)[1]

    l_fin = lax.cond(bound < SOFTMAX_BOUND_MAX, run_bounded, run_online)
    outs = [acc_ref[h * ATT_HD:(h + 1) * ATT_HD, :] / l_fin[h] for h in range(ATT_HEADS)]
    o_ref[...] = jnp.concatenate(outs, axis=0).T.astype(BF16)


def _dsa(logit_bound, qi, small, q, ki, k, vt, batch, seq, tq=256):
    t = q.shape[1]
    tq = min(tq, seq)
    nq = seq // tq
    n_sel = min(TOPK_MAX, seq // 4)
    assert tq >= n_sel and seq % tq == 0 and tq % (128 * SCORE_UNROLL) == 0
    kern = functools.partial(_dsa_kernel, tq=tq, n_sel=n_sel, seq=seq)
    return pl.pallas_call(
        kern,
        grid=(batch, nq),
        in_specs=[
            pl.BlockSpec(memory_space=pltpu.SMEM),
            pl.BlockSpec((512, tq), lambda b, i: (0, b * nq + i)),
            pl.BlockSpec((tq, LANES), lambda b, i: (b * nq + i, 0)),
            pl.BlockSpec((256, tq), lambda b, i: (0, b * nq + i)),
            pl.BlockSpec((seq, IDX_HD), lambda b, i: (b, 0)),
            pl.BlockSpec((ATT_HEADS, seq, ATT_HD), lambda b, i: (0, b, 0)),
            pl.BlockSpec((256, seq), lambda b, i: (0, b)),
        ],
        out_specs=pl.BlockSpec((tq, 256), lambda b, i: (b * nq + i, 0)),
        out_shape=jax.ShapeDtypeStruct((t, 256), BF16),
        scratch_shapes=[pltpu.VMEM((seq, tq), F32), pltpu.VMEM((ATT_HEADS * ATT_HD, tq), F32),
                        pltpu.VMEM((ATT_HEADS, tq, tq), F32), pltpu.VMEM((ATT_HEADS, tq, tq), F32)],
        compiler_params=_vmem_params(("parallel", "arbitrary"), 48),
        name="dsa",
    )(logit_bound, qi, small, q, ki, k, vt)


def _mlstm_kernel(q_ref, k_ref, v_ref, o_ref, sm_ref, gb_ref, ng_ref, y_ref, c_ref, m_ref, *, nb):
    L = MLSTM_CHUNK
    hd = MLSTM_HD

    @pl.when(pl.program_id(1) == 0)
    def _():
        c_ref[...] = jnp.zeros_like(c_ref)
        m_ref[...] = jnp.zeros_like(m_ref)

    r_i = lax.broadcasted_iota(jnp.int32, (L, L), 0)
    c_i = lax.broadcasted_iota(jnp.int32, (L, L), 1)
    tril = r_i >= c_i
    tril_f = tril.astype(F32)
    is_f = (c_i >= SM_CF) & (c_i < SM_CF + MLSTM_HEADS)
    ones = jnp.ones((L, hd), BF16)
    for bb in range(nb):
        g = sm_ref[bb] + gb_ref[...]
        lf = jax.nn.log_sigmoid(g)
        bcum = jnp.dot(tril_f, lf, precision=lax.Precision.HIGHEST, preferred_element_type=F32)
        xc = jnp.where(is_f, bcum, g)
        xr = xc.T
        outs = []
        for h in range(MLSTM_HEADS):
            st = bb * MLSTM_HEADS + h
            qh = q_ref[bb, :, h * hd:(h + 1) * hd]
            kh = k_ref[bb, :, h * hd:(h + 1) * hd].astype(F32) * (hd ** -0.5)
            v_ext = jnp.concatenate([v_ref[bb, :, h * hd:(h + 1) * hd], ones], axis=1)
            b_col = xc[:, SM_CF + h:SM_CF + h + 1]
            ig_col = xc[:, SM_CI + h:SM_CI + h + 1]
            b_row = xr[SM_CF + h:SM_CF + h + 1, :]
            ig_row = xr[SM_CI + h:SM_CI + h + 1, :]
            m_st = m_ref[st][0:1, 0:1]
            c_st = c_ref[st]
            dm = jnp.where(tril, b_col - b_row + ig_row, -jnp.inf)
            inter = b_col + m_st
            mj = jnp.maximum(inter, jnp.max(dm, axis=1, keepdims=True))
            qk = lax.dot_general(qh, kh.astype(BF16), (((1,), (1,)), ((), ())), preferred_element_type=F32)
            a = jnp.exp(dm - mj) * qk
            w_inter = jnp.exp(inter - mj)
            r = w_inter * jnp.dot(qh, c_st.astype(BF16), preferred_element_type=F32) \
                + jnp.dot(a.astype(BF16), v_ext, preferred_element_type=F32)
            num = r[:, :hd]
            den = r[:, hd:hd + 1]
            hout = num / jnp.maximum(jnp.abs(den), jnp.exp(-mj))
            b_last = b_col[L - 1:L, :]
            m_new = mj[L - 1:L, :]
            wk = jnp.exp(b_last - b_col + ig_col - m_new)
            decay = jnp.exp(b_last + m_st - m_new)
            kw = (kh * wk).astype(BF16)
            c_ref[st] = decay * c_st + lax.dot_general(kw, v_ext, (((0,), (0,)), ((), ())),
                                                       preferred_element_type=F32)
            m_ref[st] = jnp.broadcast_to(m_new, m_ref.shape[1:])
            ms = jnp.mean(hout * hout, axis=-1, keepdims=True)
            hn = hout * lax.rsqrt(ms + EPS) * ng_ref[:, h * hd:(h + 1) * hd]
            outs.append(jax.nn.sigmoid(o_ref[bb, :, h * hd:(h + 1) * hd].astype(F32)) * hn)
        y_ref[bb] = jnp.concatenate(outs, axis=1).astype(BF16)


def _mlstm(proj, small, gbias, norm_g, batch, seq, nb=MLSTM_SEQS_PER_STEP):
    t = proj.shape[0]
    L = MLSTM_CHUNK
    nc = seq // L
    nb = min(nb, batch)
    assert batch % nb == 0
    proj3 = proj.reshape(batch, seq, PROJ_W)
    small3 = small.reshape(batch, seq, LANES)

    def col(off):
        return pl.BlockSpec((nb, L, 256), lambda b, c: (b, c, off // 256))

    kern = functools.partial(_mlstm_kernel, nb=nb)
    y = pl.pallas_call(
        kern,
        grid=(batch // nb, nc),
        in_specs=[col(COL_CQ), col(COL_CK), col(COL_CV), col(COL_CO),
                  pl.BlockSpec((nb, L, LANES), lambda b, c: (b, c, 0)),
                  pl.BlockSpec((1, LANES), lambda b, c: (0, 0)),
                  pl.BlockSpec((1, 256), lambda b, c: (0, 0))],
        out_specs=pl.BlockSpec((nb, L, 256), lambda b, c: (b, c, 0)),
        out_shape=jax.ShapeDtypeStruct((batch, seq, 256), BF16),
        scratch_shapes=[pltpu.VMEM((nb * MLSTM_HEADS, MLSTM_HD, LANES), F32),
                        pltpu.VMEM((nb * MLSTM_HEADS, SUBLANES, LANES), F32)],
        compiler_params=_vmem_params(("parallel", "arbitrary"), 32),
        name="mlstm",
    )(proj3, proj3, proj3, proj3, small3, gbias, norm_g)
    return y.reshape(t, 256)


def _merge_kernel(x_ref, uv_ref, db_ref, dc_ref, dx_ref, dcp_ref, dxp_ref, g_ref, yb_ref, yc_ref,
                  sn_ref, sw_ref, sb_ref, cw_ref, wb_ref, wo_ref, o_ref, *, tm, tiles_per_seq):
    i = pl.program_id(0)
    L = SGU_CHUNK
    u = jax.nn.gelu(uv_ref[:, :BRANCH_W].astype(F32))
    v = jax.nn.gelu(uv_ref[:, BRANCH_W:].astype(F32))
    ms = jnp.mean(v * v, axis=-1, keepdims=True)
    vn = (v * lax.rsqrt(ms + EPS) * sn_ref[...]).astype(BF16)
    r_i = lax.broadcasted_iota(jnp.int32, (L, L), 0)
    c_i = lax.broadcasted_iota(jnp.int32, (L, L), 1)
    tril = r_i >= c_i
    wm = [jnp.where(tril, sw_ref[g], 0.0).astype(BF16) for g in range(SGU_GROUPS)]
    sb = sb_ref[...]
    chunks = []
    for c in range(tm // L):
        parts = []
        for g in range(SGU_GROUPS):
            vg = vn[c * L:(c + 1) * L, g * SGU_GD:(g + 1) * SGU_GD]
            parts.append(jnp.dot(wm[g], vg, preferred_element_type=F32) + sb[:, g:g + 1])
        chunks.append(jnp.concatenate(parts, axis=1))
    y_a = u * jnp.concatenate(chunks, axis=0)

    z = dc_ref[...].astype(F32) * dx_ref[...].astype(F32)
    zp = dcp_ref[...].astype(F32) * dxp_ref[...].astype(F32)
    zp = jnp.where(i % tiles_per_seq == 0, jnp.zeros_like(zp), zp)
    zz = jnp.concatenate([zp, z], axis=0)
    cw = cw_ref[...]
    conv = cw[0:1, :] * zz[PREV_ROWS - 2:PREV_ROWS - 2 + tm, :] + cw[1:2, :] * zz[PREV_ROWS - 1:PREV_ROWS - 1 + tm, :] \
        + cw[2:3, :] * z
    y_d = db_ref[...].astype(F32) * conv

    ys = (y_a, yb_ref[...], yc_ref[...], y_d)
    merged = jnp.zeros((tm, D_MODEL), F32)
    for n in range(N_BRANCH):
        gate = jax.nn.sigmoid(g_ref[:, n * D_MODEL:(n + 1) * D_MODEL].astype(F32))
        merged = merged + gate * jnp.dot(ys[n].astype(BF16), wb_ref[n], preferred_element_type=F32)
    o_ref[...] = x_ref[...] + jnp.dot(merged.astype(BF16), wo_ref[...], preferred_element_type=F32)


def _merge(x2, proj, y_b, y_c, sgu_norm, sgu_w, sgu_bt, conv_w, w_branch, w_out, seq, tm=512):
    t = x2.shape[0]
    tm = min(tm, seq)
    tiles_per_seq = seq // tm
    rb = tm // PREV_ROWS

    def col(width, off):
        return pl.BlockSpec((tm, width), lambda i: (i, off // width))

    def prev(off):
        return pl.BlockSpec((PREV_ROWS, 256), lambda i: (jnp.maximum(i * rb - 1, 0), off // 256))

    def full(shape):
        return pl.BlockSpec(shape, lambda i: (0,) * len(shape))

    kern = functools.partial(_merge_kernel, tm=tm, tiles_per_seq=tiles_per_seq)
    return pl.pallas_call(
        kern,
        grid=(t // tm,),
        in_specs=[pl.BlockSpec((tm, D_MODEL), lambda i: (i, 0)),
                  col(512, COL_AU), col(256, COL_DB), col(256, COL_DC), col(256, COL_DX),
                  prev(COL_DC), prev(COL_DX), col(4096, COL_G),
                  pl.BlockSpec((tm, 256), lambda i: (i, 0)), pl.BlockSpec((tm, 256), lambda i: (i, 0)),
                  full((1, BRANCH_W)), full((SGU_GROUPS, SGU_CHUNK, SGU_CHUNK)), full((SGU_CHUNK, LANES)),
                  full((SUBLANES, BRANCH_W)), full((N_BRANCH, BRANCH_W, D_MODEL)), full((D_MODEL, D_MODEL))],
        out_specs=pl.BlockSpec((tm, D_MODEL), lambda i: (i, 0)),
        out_shape=jax.ShapeDtypeStruct((t, D_MODEL), F32),
        compiler_params=_vmem_params(("parallel",), 48),
        name="merge",
    )(x2, proj, proj, proj, proj, proj, proj, proj, y_b, y_c, sgu_norm, sgu_w, sgu_bt, conv_w, w_branch, w_out)


def _mlp_kernel(x_ref, g_ref, wu_ref, wd_ref, o_ref, h_ref):
    f = pl.program_id(1)

    @pl.when(f == 0)
    def _():
        x = x_ref[...]
        ms = jnp.mean(x * x, axis=-1, keepdims=True)
        h_ref[...] = (x * lax.rsqrt(ms + EPS) * g_ref[...]).astype(BF16)
        o_ref[...] = x

    up = jnp.maximum(jnp.dot(h_ref[...], wu_ref[...], preferred_element_type=F32), 0.0)
    o_ref[...] += jnp.dot((up * up).astype(BF16), wd_ref[...], preferred_element_type=F32)


def _mlp(x2, g, w_up, w_down, tm=1024, tf=1024):
    t = x2.shape[0]
    tm = min(tm, t)
    return pl.pallas_call(
        _mlp_kernel,
        grid=(t // tm, D_FF // tf),
        in_specs=[pl.BlockSpec((tm, D_MODEL), lambda i, f: (i, 0)),
                  pl.BlockSpec((1, D_MODEL), lambda i, f: (0, 0)),
                  pl.BlockSpec((D_MODEL, tf), lambda i, f: (0, f)),
                  pl.BlockSpec((tf, D_MODEL), lambda i, f: (f, 0))],
        out_specs=pl.BlockSpec((tm, D_MODEL), lambda i, f: (i, 0)),
        out_shape=jax.ShapeDtypeStruct((t, D_MODEL), F32),
        scratch_shapes=[pltpu.VMEM((tm, D_MODEL), BF16)],
        compiler_params=_vmem_params(("parallel", "arbitrary"), 48),
        name="mlp",
    )(x2, g, w_up, w_down)


def _relayout_w_in(w_in):
    sizes = (256, 256, 256, 256, 256, 512, 64, 8, 256, 256, 256, 256, 4, 4, 256, 256, 256, 4096)
    offs = [0]
    for s in sizes:
        offs.append(offs[-1] + s)
    (a_u, a_v, b_q, b_k, b_v, b_qi, b_ki, b_wi, c_q, c_k, c_v, c_o, c_i, c_f, d_b, d_c, d_x, g) = [
        w_in[:, offs[n]:offs[n + 1]] for n in range(len(sizes))]
    d = w_in.shape[0]
    small = jnp.concatenate([b_ki, b_wi, c_i, c_f, jnp.zeros((d, LANES - SM_CF - MLSTM_HEADS), w_in.dtype)], axis=1)
    pad = jnp.zeros((d, COL_G - COL_SMALL - LANES), w_in.dtype)
    out = jnp.concatenate([b_qi, a_u, a_v, b_q, b_k, b_v, c_q, c_k, c_v, c_o, d_b, d_c, d_x, small, pad, g], axis=1)
    assert out.shape[1] == PROJ_W
    return out.astype(BF16)


def _rope_tables(seq):
    half = ATT_HD // 2
    inv = jnp.float32(ROPE_THETA) ** (-jnp.arange(half, dtype=jnp.float32) * 2.0 / ATT_HD)
    ang = jnp.arange(seq, dtype=jnp.int32).astype(jnp.float32)[:, None] * inv[None, :]
    cos = jnp.cos(ang)
    sin = jnp.sin(ang)
    cos_t = jnp.concatenate([cos, cos, cos, cos], axis=1)
    sin_t = jnp.concatenate([-sin, sin, -sin, sin], axis=1)
    return cos_t, sin_t


def _tile2(v):
    return jnp.concatenate([v, v]).reshape(1, LANES)


def _forward(x, ln_mix, w_in, sgu_norm, sgu_w, sgu_b, q_norm, k_norm, kidx_norm, mlstm_i_bias, mlstm_f_bias,
             mlstm_norm, conv_w, w_branch, w_out, ln_mlp, w_up, w_down):
    batch, seq, d = x.shape
    depth = w_in.shape[0]
    x2 = x.reshape(batch * seq, d)
    cos_t, sin_t = _rope_tables(seq)
    for l in range(depth):
        proj, small = _inproj(x2, ln_mix[l].reshape(1, d), _relayout_w_in(w_in[l]))
        q, k, v, qi, ki = _attprep(proj, small, cos_t, sin_t, _tile2(q_norm[l]), _tile2(k_norm[l]),
                                   _tile2(kidx_norm[l]), seq)
        logit_bound = (ATT_HD ** 0.5 * LOG2E * jnp.max(jnp.abs(q_norm[l])) * jnp.max(jnp.abs(k_norm[l]))).reshape(1, 1)
        y_b = _dsa(logit_bound, qi, small, q, ki, k, v, batch, seq)
        gbias = jnp.zeros((1, LANES), F32)
        gbias = gbias.at[0, SM_CI:SM_CI + MLSTM_HEADS].set(mlstm_i_bias[l])
        gbias = gbias.at[0, SM_CF:SM_CF + MLSTM_HEADS].set(mlstm_f_bias[l])
        y_c = _mlstm(proj, small, gbias, mlstm_norm[l].reshape(1, -1), batch, seq)
        sgu_bt = jnp.zeros((SGU_CHUNK, LANES), F32).at[:, :SGU_GROUPS].set(sgu_b[l].T)
        conv_p = jnp.zeros((SUBLANES, BRANCH_W), F32).at[:CONV_WIDTH].set(conv_w[l])
        x2 = _merge(x2, proj, y_b, y_c, sgu_norm[l].reshape(1, -1), sgu_w[l], sgu_bt, conv_p,
                    w_branch[l].astype(BF16), w_out[l].astype(BF16), seq)
        x2 = _mlp(x2, ln_mlp[l].reshape(1, d), w_up[l].astype(BF16), w_down[l].astype(BF16))
    return x2.reshape(batch, seq, d)


def kernel(x, ln_mix, w_in, sgu_norm, sgu_w, sgu_b, q_norm, k_norm, kidx_norm, mlstm_i_bias, mlstm_f_bias,
           mlstm_norm, conv_w, w_branch, w_out, ln_mlp, w_up, w_down):
    return _forward(x, ln_mix, w_in, sgu_norm, sgu_w, sgu_b, q_norm, k_norm, kidx_norm, mlstm_i_bias,
                    mlstm_f_bias, mlstm_norm, conv_w, w_branch, w_out, ln_mlp, w_up, w_down)
```

```python
import functools

import jax
import jax.numpy as jnp
from jax import lax
from jax.experimental import pallas as pl
from jax.experimental.pallas import tpu as pltpu

F32 = jnp.float32
BF16 = jnp.bfloat16

D_MODEL = 1024
N_BRANCH = 4
BRANCH_W = 256
SGU_GROUPS = 4
SGU_GD = BRANCH_W // SGU_GROUPS
SGU_CHUNK = 128
ATT_HEADS = 4
ATT_HD = 64
IDX_HEADS = 8
IDX_HD = 64
TOPK_MAX = 256
ROPE_THETA = 10000.0
MLSTM_HEADS = 4
MLSTM_HD = 64
MLSTM_CHUNK = 128
CONV_WIDTH = 3
D_FF = 4 * D_MODEL
EPS = 1e-6

LANES = 128
SUBLANES = 8
PREV_ROWS = 16

COL_AU = 0
COL_AV = 256
COL_Q = 512
COL_K = 768
COL_V = 1024
COL_QI = 1280
COL_CQ = 1792
COL_CK = 2048
COL_CV = 2304
COL_CO = 2560
COL_DB = 2816
COL_DC = 3072
COL_DX = 3328
COL_SMALL = 3584
COL_G = 4096
PROJ_W = 8192
SM_WI = IDX_HD
SM_CI = SM_WI + IDX_HEADS
SM_CF = SM_CI + MLSTM_HEADS

LOG2E = 1.4426950408889634
NEG_BIG = -1e30
M_INIT = -1e29
F32_MIN = -3.4028234663852886e38
F32_ABOVE_MIN = -3.4028232635611926e38
F32_TINY = 1.1754943508222875e-38
BISECT_FIRST_PASSES = 16
BISECT_PASSES_PER_ROUND = 2
MAX_BISECT_ROUNDS = 160
SOFTMAX_BOUND_MAX = 60.0
SCORE_UNROLL = 2
MLSTM_SEQS_PER_STEP = 4


def _vmem_params(sem, mib):
    return pltpu.CompilerParams(dimension_semantics=sem, vmem_limit_bytes=mib * 1024 * 1024)


def _inproj_kernel(x_ref, g_ref, w_ref, o_ref, sm_ref, h_ref, *, small_tile, small_off):
    j = pl.program_id(1)

    @pl.when(j == 0)
    def _():
        x = x_ref[...]
        ms = jnp.mean(x * x, axis=-1, keepdims=True)
        h_ref[...] = (x * lax.rsqrt(ms + EPS) * g_ref[...]).astype(BF16)

    res = jnp.dot(h_ref[...], w_ref[...], preferred_element_type=F32)
    o_ref[...] = res.astype(BF16)

    @pl.when(j == small_tile)
    def _():
        sm_ref[...] = res[:, small_off:small_off + LANES]


def _inproj(x2, g, w, layer, tm=2048, tn=1024):
    t = x2.shape[0]
    tm = min(tm, t)
    kern = functools.partial(_inproj_kernel, small_tile=COL_SMALL // tn, small_off=COL_SMALL % tn)
    return pl.pallas_call(
        kern,
        grid=(t // tm, PROJ_W // tn),
        in_specs=[
            pl.BlockSpec((tm, D_MODEL), lambda i, j: (i, 0)),
            pl.BlockSpec((None, 1, D_MODEL), lambda i, j: (layer, 0, 0)),
            pl.BlockSpec((None, D_MODEL, tn), lambda i, j: (layer, 0, j)),
        ],
        out_specs=[pl.BlockSpec((tm, tn), lambda i, j: (i, j)), pl.BlockSpec((tm, LANES), lambda i, j: (i, 0))],
        out_shape=[jax.ShapeDtypeStruct((t, PROJ_W), BF16), jax.ShapeDtypeStruct((t, LANES), F32)],
        scratch_shapes=[pltpu.VMEM((tm, D_MODEL), BF16)],
        compiler_params=_vmem_params(("parallel", "arbitrary"), 56),
        name="inproj",
    )(x2, g, w)


def _rope_slab(x, cos, sin_signed, first_half):
    x_hi = pltpu.roll(x, LANES - ATT_HD // 2, axis=1)
    x_lo = pltpu.roll(x, ATT_HD // 2, axis=1)
    rot = jnp.where(first_half, x_hi, x_lo)
    return x * cos + rot * sin_signed


def _head_sumsq(x2, lane, n_heads):
    out = []
    for h in range(n_heads):
        m = (lane >= h * ATT_HD) & (lane < (h + 1) * ATT_HD)
        out.append(jnp.sum(jnp.where(m, x2, 0.0), axis=-1, keepdims=True))
    return out


def _attprep_kernel(q_ref, k_ref, v_ref, qi0_ref, qi1_ref, sm_ref, cos_ref, sin_ref, qn_ref, kn_ref, kin_ref,
                    qo_ref, ko_ref, vo_ref, qio_ref, kio_ref):
    cos = cos_ref[...]
    sin = sin_ref[...]
    rows = cos.shape[0]
    lane = lax.broadcasted_iota(jnp.int32, (rows, LANES), 1)
    first_half = (lane % ATT_HD) < (ATT_HD // 2)
    head0 = lane < ATT_HD

    def norm_rope(ref, g_ref, scale):
        slabs = []
        for s in range(ref.shape[1] // LANES):
            x = ref[:, s * LANES:(s + 1) * LANES].astype(F32)
            ss = _head_sumsq(x * x, lane, 2)
            r0 = lax.rsqrt(ss[0] * (1.0 / ATT_HD) + EPS)
            r1 = lax.rsqrt(ss[1] * (1.0 / ATT_HD) + EPS)
            y = x * jnp.where(head0, r0, r1) * g_ref[...]
            y = _rope_slab(y, cos, sin, first_half)
            if scale != 1.0:
                y = y * scale
            slabs.append(y)
        return slabs

    for s, y in enumerate(norm_rope(q_ref, qn_ref, ATT_HD ** -0.5 * LOG2E)):
        qo_ref[s * LANES:(s + 1) * LANES, :] = y.T.astype(BF16)
    for s, y in enumerate(norm_rope(k_ref, kn_ref, 1.0)):
        ko_ref[2 * s] = y[:, :ATT_HD].astype(BF16)
        ko_ref[2 * s + 1] = y[:, ATT_HD:].astype(BF16)
    vo_ref[...] = v_ref[...].astype(F32).T.astype(BF16)

    for s in range(IDX_HEADS * IDX_HD // LANES):
        qi_ref = (qi0_ref, qi1_ref)[s // 2]
        y = _rope_slab(qi_ref[:, (s % 2) * LANES:(s % 2 + 1) * LANES].astype(F32), cos, sin, first_half)
        qio_ref[s * LANES:(s + 1) * LANES, :] = (y * (IDX_HD ** -0.5)).T.astype(BF16)

    sm = sm_ref[...]
    mu = jnp.sum(jnp.where(head0, sm, 0.0), axis=-1, keepdims=True) * (1.0 / IDX_HD)
    d = sm - mu
    var = jnp.sum(jnp.where(head0, d * d, 0.0), axis=-1, keepdims=True) * (1.0 / IDX_HD)
    y = d * lax.rsqrt(var + EPS) * kin_ref[...]
    y = _rope_slab(y, cos, sin, first_half)
    kio_ref[...] = y[:, :IDX_HD].astype(BF16)


def _attprep(proj, small, cos, sin, qn, kn, kin, layer, seq, tm=512):
    t = proj.shape[0]
    tm = min(tm, seq)
    npos = seq // tm

    def col(width, off):
        return pl.BlockSpec((tm, width), lambda i: (i, off // width))

    vec = pl.BlockSpec((None, 1, LANES), lambda i: (layer, 0, 0))
    tab = pl.BlockSpec((tm, LANES), lambda i: (i % npos, 0))
    return pl.pallas_call(
        _attprep_kernel,
        grid=(t // tm,),
        in_specs=[col(256, COL_Q), col(256, COL_K), col(256, COL_V), col(256, COL_QI), col(256, COL_QI + 256),
                  pl.BlockSpec((tm, LANES), lambda i: (i, 0)), tab, tab, vec, vec, vec],
        out_specs=[pl.BlockSpec((256, tm), lambda i: (0, i)),
                   pl.BlockSpec((ATT_HEADS, tm, ATT_HD), lambda i: (0, i, 0)),
                   pl.BlockSpec((256, tm), lambda i: (0, i)), pl.BlockSpec((512, tm), lambda i: (0, i)),
                   pl.BlockSpec((tm, IDX_HD), lambda i: (i, 0))],
        out_shape=[jax.ShapeDtypeStruct((256, t), BF16), jax.ShapeDtypeStruct((ATT_HEADS, t, ATT_HD), BF16),
                   jax.ShapeDtypeStruct((256, t), BF16), jax.ShapeDtypeStruct((512, t), BF16),
                   jax.ShapeDtypeStruct((t, IDX_HD), BF16)],
        compiler_params=_vmem_params(("parallel",), 32),
        name="attprep",
    )(proj, proj, proj, proj, proj, small, cos, sin, qn, kn, kin)


def _dsa_kernel(mb_ref, qi_ref, sm_ref, q_ref, ki_ref, k_ref, vt_ref, o_ref, sc_ref, acc_ref, s0_ref, s1_ref,
                *, tq, n_sel, seq, layer):
    i = pl.program_id(1)
    tk1 = min(128, tq)
    tk2 = tq
    tk3 = tq
    n_keys = (i + 1) * tq
    qpos = i * tq + lax.broadcasted_iota(jnp.int32, (1, tq), 1)

    wt = (sm_ref[...] * (IDX_HEADS ** -0.5)).T
    qi_h = [qi_ref[h * IDX_HD:(h + 1) * IDX_HD, :] for h in range(IDX_HEADS)]
    w_h = [wt[SM_WI + h:SM_WI + h + 1, :] for h in range(IDX_HEADS)]
    kidx1 = lax.broadcasted_iota(jnp.int32, (tk1, tq), 0)

    def score_chunk(c, amax):
        for u in range(SCORE_UNROLL):
            start = pl.multiple_of((c * SCORE_UNROLL + u) * tk1, tk1)
            kc = ki_ref[pl.ds(start, tk1), :]
            acc = jnp.zeros((tk1, tq), F32)
            for h in range(IDX_HEADS):
                s = jnp.dot(kc, qi_h[h], preferred_element_type=F32)
                acc = acc + w_h[h] * jnp.maximum(s, 0.0)
            amax = jnp.maximum(amax, jnp.max(jnp.abs(acc), axis=0, keepdims=True))
            sc_ref[pl.ds(start, tk1), :] = jnp.where(kidx1 + start <= qpos, acc, F32_MIN)
        return amax

    amax = lax.fori_loop(0, n_keys // (tk1 * SCORE_UNROLL), score_chunk, jnp.zeros((1, tq), F32))

    def count(pred_fn):
        def body(c, acc):
            start = pl.multiple_of(c * tk2, tk2)
            x = sc_ref[pl.ds(start, tk2), :]
            return acc + jnp.sum(pred_fn(x).reshape(tk2 // 32, 32, tq), axis=0)
        acc = lax.fori_loop(0, n_keys // tk2, body, jnp.zeros((32, tq), F32))
        return jnp.sum(acc, axis=0, keepdims=True)

    n_pos = count(lambda x: jnp.where(x > 0.0, 1.0, 0.0))
    n_nn = count(lambda x: jnp.where(x >= 0.0, 1.0, 0.0))
    n_causal = (qpos + 1).astype(F32)
    k_sel = jnp.float32(n_sel)
    take_all = n_causal <= k_sel
    at_zero = (n_pos < k_sel) & (n_nn >= k_sel)
    wide = amax + amax * 2.0 ** -20 + F32_TINY
    positive = n_pos >= k_sel
    lo0 = jnp.where(positive, 0.0, -wide)
    hi0 = jnp.where(positive, wide, 0.0)
    nhi0 = jnp.where(positive, 0.0, n_nn)
    thr0 = jnp.where(take_all, F32_ABOVE_MIN, 0.0)
    tie0 = jnp.where(at_zero & (n_nn > k_sel) & jnp.logical_not(take_all), 1.0, 0.0)
    need0 = k_sel - n_pos
    act0 = jnp.where(take_all | at_zero, 0.0, 1.0)

    def bisect_pass(state):
        lo, hi, n_hi, thr, tie, need, act = state
        cand = 0.5 * lo + 0.5 * hi
        cnt = count(lambda x: jnp.where(x >= cand, 1.0, 0.0))
        live = act > 0.0
        stuck = live & ((cand <= lo) | (cand >= hi))
        hit = live & jnp.logical_not(stuck) & (cnt == k_sel)
        up = live & jnp.logical_not(stuck) & (cnt > k_sel)
        down = live & jnp.logical_not(stuck) & (cnt < k_sel)
        thr = jnp.where(hit, cand, jnp.where(stuck, lo, thr))
        tie = jnp.where(stuck, 1.0, tie)
        need = jnp.where(stuck, k_sel - n_hi, need)
        lo = jnp.where(up, cand, lo)
        hi = jnp.where(down, cand, hi)
        n_hi = jnp.where(down, cnt, n_hi)
        act = jnp.where(hit | stuck, 0.0, act)
        return lo, hi, n_hi, thr, tie, need, act

    def search_cond(carry):
        it, state = carry
        return (it < MAX_BISECT_ROUNDS) & (jnp.max(state[6]) > 0.0)

    def search_body(carry):
        it, state = carry
        for _ in range(BISECT_PASSES_PER_ROUND):
            state = bisect_pass(state)
        return it + 1, state

    state = lax.fori_loop(0, BISECT_FIRST_PASSES, lambda _, st: bisect_pass(st),
                          (lo0, hi0, nhi0, thr0, tie0, need0, act0))
    _, state = lax.while_loop(search_cond, search_body, (jnp.int32(0), state))
    thr, tie, need = state[3], state[4], state[5]

    tri = (lax.broadcasted_iota(jnp.int32, (tk2, tk2), 0) >= lax.broadcasted_iota(jnp.int32, (tk2, tk2), 1))
    tri = jnp.where(tri, 1.0, 0.0).astype(BF16)

    def drop_chunk(c, run):
        start = pl.multiple_of(c * tk2, tk2)
        x = sc_ref[pl.ds(start, tk2), :]
        eq = jnp.where(tie > 0.0, jnp.where(x == thr, 1.0, 0.0), 0.0)
        rank = run + jnp.dot(tri, eq.astype(BF16), preferred_element_type=F32)
        sc_ref[pl.ds(start, tk2), :] = jnp.where(eq > 0.0, jnp.where(rank > need, F32_MIN, x), x)
        return rank[tk2 - 1:tk2, :]

    @pl.when(jnp.max(tie) > 0.0)
    def _():
        lax.fori_loop(0, n_keys // tk2, drop_chunk, jnp.zeros((1, tq), F32))

    q_h = [q_ref[h * ATT_HD:(h + 1) * ATT_HD, :] for h in range(ATT_HEADS)]
    acc_ref[...] = jnp.zeros(acc_ref.shape, F32)
    bound = mb_ref[layer, 0]

    def scores_into(buf_ref, c):
        start = pl.multiple_of(c * tk3, tk3)
        for h in range(ATT_HEADS):
            buf_ref[h] = jnp.dot(k_ref[h, pl.ds(start, tk3), :], q_h[h], preferred_element_type=F32)

    def attend_bounded(buf_ref, c, live, l_run):
        start = pl.multiple_of(c * tk3, tk3)
        sel = (sc_ref[pl.ds(start, tk3), :] >= thr) & live
        l_out = []
        for h in range(ATT_HEADS):
            p = jnp.where(sel, jnp.exp2(buf_ref[h] - bound), 0.0)
            l_out.append(l_run[h] + jnp.sum(p, axis=0, keepdims=True))
            vt = vt_ref[h * ATT_HD:(h + 1) * ATT_HD, pl.ds(start, tk3)]
            acc_ref[h * ATT_HD:(h + 1) * ATT_HD, :] += jnp.dot(vt, p.astype(BF16), preferred_element_type=F32)
        return tuple(l_out)

    n_chunks3 = n_keys // tk3
    last = n_chunks3 - 1

    def attn_bounded_pair(j, l_run):
        c_a = 2 * j
        c_b = jnp.minimum(c_a + 1, last)
        scores_into(s1_ref, c_b)
        l_run = attend_bounded(s0_ref, c_a, True, l_run)
        scores_into(s0_ref, jnp.minimum(c_a + 2, last))
        return attend_bounded(s1_ref, c_b, c_a + 1 <= last, l_run)

    def attn_online(c, carry):
        m_run, l_run = carry
        start = pl.multiple_of(c * tk3, tk3)
        sel = sc_ref[pl.ds(start, tk3), :] >= thr
        s_all = [jnp.dot(k_ref[h, pl.ds(start, tk3), :], q_h[h], preferred_element_type=F32)
                 for h in range(ATT_HEADS)]
        m_out, l_out, alphas, ps = [], [], [], []
        for h in range(ATT_HEADS):
            s = jnp.where(sel, s_all[h], NEG_BIG)
            m_new = jnp.maximum(m_run[h], jnp.max(s, axis=0, keepdims=True))
            alpha = jnp.exp2(m_run[h] - m_new)
            p = jnp.exp2(s - m_new)
            l_out.append(alpha * l_run[h] + jnp.sum(p, axis=0, keepdims=True))
            m_out.append(m_new)
            alphas.append(alpha)
            ps.append(p.astype(BF16))
        for h in range(ATT_HEADS):
            vt = vt_ref[h * ATT_HD:(h + 1) * ATT_HD, pl.ds(start, tk3)]
            pv = jnp.dot(vt, ps[h], preferred_element_type=F32)
            acc_ref[h * ATT_HD:(h + 1) * ATT_HD, :] = alphas[h] * acc_ref[h * ATT_HD:(h + 1) * ATT_HD, :] + pv
        return tuple(m_out), tuple(l_out)

    zeros = tuple(jnp.zeros((1, tq), F32) for _ in range(ATT_HEADS))

    def run_bounded():
        scores_into(s0_ref, 0)
        return lax.fori_loop(0, (n_chunks3 + 1) // 2, attn_bounded_pair, zeros)

    def run_online():
        init = (tuple(jnp.full((1, tq), M_INIT, F32) for _ in range(ATT_HEADS)), zeros)
        return lax.fori_loop(0, n_keys // tk3, attn_online, init)[1]

    l_fin = lax.cond(bound < SOFTMAX_BOUND_MAX, run_bounded, run_online)
    outs = [acc_ref[h * ATT_HD:(h + 1) * ATT_HD, :] / l_fin[h] for h in range(ATT_HEADS)]
    o_ref[...] = jnp.concatenate(outs, axis=0).T.astype(BF16)


def _dsa(logit_bound, qi, small, q, ki, k, vt, layer, batch, seq, tq=256):
    t = q.shape[1]
    tq = min(tq, seq)
    nq = seq // tq
    n_sel = min(TOPK_MAX, seq // 4)
    assert tq >= n_sel and seq % tq == 0 and tq % (128 * SCORE_UNROLL) == 0
    kern = functools.partial(_dsa_kernel, tq=tq, n_sel=n_sel, seq=seq, layer=layer)
    return pl.pallas_call(
        kern,
        grid=(batch, nq),
        in_specs=[
            pl.BlockSpec(memory_space=pltpu.SMEM),
            pl.BlockSpec((512, tq), lambda b, i: (0, b * nq + i)),
            pl.BlockSpec((tq, LANES), lambda b, i: (b * nq + i, 0)),
            pl.BlockSpec((256, tq), lambda b, i: (0, b * nq + i)),
            pl.BlockSpec((seq, IDX_HD), lambda b, i: (b, 0)),
            pl.BlockSpec((ATT_HEADS, seq, ATT_HD), lambda b, i: (0, b, 0)),
            pl.BlockSpec((256, seq), lambda b, i: (0, b)),
        ],
        out_specs=pl.BlockSpec((tq, 256), lambda b, i: (b * nq + i, 0)),
        out_shape=jax.ShapeDtypeStruct((t, 256), BF16),
        scratch_shapes=[pltpu.VMEM((seq, tq), F32), pltpu.VMEM((ATT_HEADS * ATT_HD, tq), F32),
                        pltpu.VMEM((ATT_HEADS, tq, tq), F32), pltpu.VMEM((ATT_HEADS, tq, tq), F32)],
        compiler_params=_vmem_params(("parallel", "arbitrary"), 48),
        name="dsa",
    )(logit_bound, qi, small, q, ki, k, vt)


def _mlstm_kernel(q_ref, k_ref, v_ref, o_ref, sm_ref, gb_ref, ng_ref, y_ref, c_ref, m_ref, *, nb):
    L = MLSTM_CHUNK
    hd = MLSTM_HD

    @pl.when(pl.program_id(1) == 0)
    def _():
        c_ref[...] = jnp.zeros_like(c_ref)
        m_ref[...] = jnp.zeros_like(m_ref)

    r_i = lax.broadcasted_iota(jnp.int32, (L, L), 0)
    c_i = lax.broadcasted_iota(jnp.int32, (L, L), 1)
    tril_f = (r_i >= c_i).astype(F32)
    causal = r_i <= c_i
    is_f = (c_i >= SM_CF) & (c_i < SM_CF + MLSTM_HEADS)
    ones = jnp.ones((hd, L), BF16)
    chains = [(bb, h) for bb in range(nb) for h in range(MLSTM_HEADS)]
    xc, xr, q_t, v_t, o_t = [], [], [], [], []
    for bb in range(nb):
        g = sm_ref[bb] + gb_ref[...]
        lf = jax.nn.log_sigmoid(g)
        bcum = jnp.dot(tril_f, lf, precision=lax.Precision.HIGHEST, preferred_element_type=F32)
        xc.append(jnp.where(is_f, bcum, g))
        xr.append(xc[bb].T)
        q_t.append(q_ref[bb].astype(F32).T.astype(BF16))
        v_t.append(v_ref[bb].astype(F32).T.astype(BF16))
        o_t.append(o_ref[bb].astype(F32).T)

    qh_t, kh, v_ext_t, qk, gq = {}, {}, {}, {}, {}
    for st, (bb, h) in enumerate(chains):
        qh_t[st] = q_t[bb][h * hd:(h + 1) * hd, :]
        kh[st] = (k_ref[bb, :, h * hd:(h + 1) * hd].astype(F32) * (hd ** -0.5)).astype(BF16)
        v_ext_t[st] = jnp.concatenate([v_t[bb][h * hd:(h + 1) * hd, :], ones], axis=0)
        qk[st] = jnp.dot(kh[st], qh_t[st], preferred_element_type=F32)
        gq[st] = jnp.dot(c_ref[st].astype(BF16), qh_t[st], preferred_element_type=F32)

    a, w_inter, mjs, b_rows, ig_rows, m_sts = {}, {}, {}, {}, {}, {}
    for st, (bb, h) in enumerate(chains):
        b_row = xr[bb][SM_CF + h:SM_CF + h + 1, :]
        ig_row = xr[bb][SM_CI + h:SM_CI + h + 1, :]
        src_col = xc[bb][:, SM_CI + h:SM_CI + h + 1] - xc[bb][:, SM_CF + h:SM_CF + h + 1]
        m_st = m_ref[st][0:1, 0:1]
        dm = jnp.where(causal, b_row + src_col, -jnp.inf)
        inter = b_row + m_st
        mj = jnp.maximum(inter, jnp.max(dm, axis=0, keepdims=True))
        a[st] = (jnp.exp(dm - mj) * qk[st]).astype(BF16)
        w_inter[st] = jnp.exp(inter - mj)
        mjs[st], b_rows[st], ig_rows[st], m_sts[st] = mj, b_row, ig_row, m_st

    av = {st: jnp.dot(v_ext_t[st], a[st], preferred_element_type=F32) for st in a}

    vw, decays = {}, {}
    for bb in range(nb):
        outs = []
        for h in range(MLSTM_HEADS):
            st = bb * MLSTM_HEADS + h
            mj, b_row = mjs[st], b_rows[st]
            r = w_inter[st] * gq[st] + av[st]
            num = r[:hd, :]
            den = r[hd:hd + 1, :]
            hout = num / jnp.maximum(jnp.abs(den), jnp.exp(-mj))
            b_last = b_row[:, L - 1:L]
            m_new = mj[:, L - 1:L]
            wk = jnp.exp(b_last - b_row + ig_rows[st] - m_new)
            decays[st] = jnp.exp(b_last + m_sts[st] - m_new)
            vw[st] = (v_ext_t[st] * wk).astype(BF16)
            m_ref[st] = jnp.broadcast_to(m_new, m_ref.shape[1:])
            ms = jnp.mean(hout * hout, axis=0, keepdims=True)
            hn = hout * lax.rsqrt(ms + EPS) * ng_ref[h * hd:(h + 1) * hd, :]
            outs.append(jax.nn.sigmoid(o_t[bb][h * hd:(h + 1) * hd, :]) * hn)
        y_ref[bb] = jnp.concatenate(outs, axis=0).T.astype(BF16)

    for st in vw:
        c_ref[st] = decays[st] * c_ref[st] + jnp.dot(vw[st], kh[st], preferred_element_type=F32)


def _mlstm(proj, small, gbias, norm_g, layer, batch, seq, nb=MLSTM_SEQS_PER_STEP):
    t = proj.shape[0]
    L = MLSTM_CHUNK
    nc = seq // L
    nb = min(nb, batch)
    assert batch % nb == 0
    proj3 = proj.reshape(batch, seq, PROJ_W)
    small3 = small.reshape(batch, seq, LANES)

    def col(off):
        return pl.BlockSpec((nb, L, 256), lambda b, c: (b, c, off // 256))

    kern = functools.partial(_mlstm_kernel, nb=nb)
    y = pl.pallas_call(
        kern,
        grid=(batch // nb, nc),
        in_specs=[col(COL_CQ), col(COL_CK), col(COL_CV), col(COL_CO),
                  pl.BlockSpec((nb, L, LANES), lambda b, c: (b, c, 0)),
                  pl.BlockSpec((None, 1, LANES), lambda b, c: (layer, 0, 0)),
                  pl.BlockSpec((None, 256, LANES), lambda b, c: (layer, 0, 0))],
        out_specs=pl.BlockSpec((nb, L, 256), lambda b, c: (b, c, 0)),
        out_shape=jax.ShapeDtypeStruct((batch, seq, 256), BF16),
        scratch_shapes=[pltpu.VMEM((nb * MLSTM_HEADS, LANES, MLSTM_HD), F32),
                        pltpu.VMEM((nb * MLSTM_HEADS, SUBLANES, LANES), F32)],
        compiler_params=_vmem_params(("parallel", "arbitrary"), 32),
        name="mlstm",
    )(proj3, proj3, proj3, proj3, small3, gbias, norm_g)
    return y.reshape(t, 256)


def _merge_kernel(x_ref, uv_ref, db_ref, dc_ref, dx_ref, dcp_ref, dxp_ref, g_ref, yb_ref, yc_ref,
                  sn_ref, sw_ref, sb_ref, cw_ref, wb_ref, wo_ref, o_ref, *, tm, tiles_per_seq):
    i = pl.program_id(0)
    L = SGU_CHUNK
    u = jax.nn.gelu(uv_ref[:, :BRANCH_W].astype(F32))
    v = jax.nn.gelu(uv_ref[:, BRANCH_W:].astype(F32))
    ms = jnp.mean(v * v, axis=-1, keepdims=True)
    vn = (v * lax.rsqrt(ms + EPS) * sn_ref[...]).astype(BF16)
    r_i = lax.broadcasted_iota(jnp.int32, (L, L), 0)
    c_i = lax.broadcasted_iota(jnp.int32, (L, L), 1)
    tril = r_i >= c_i
    wm = [jnp.where(tril, sw_ref[g], 0.0).astype(BF16) for g in range(SGU_GROUPS)]
    sb = sb_ref[...]
    chunks = []
    for c in range(tm // L):
        parts = []
        for g in range(SGU_GROUPS):
            vg = vn[c * L:(c + 1) * L, g * SGU_GD:(g + 1) * SGU_GD]
            parts.append(jnp.dot(wm[g], vg, preferred_element_type=F32) + sb[:, g:g + 1])
        chunks.append(jnp.concatenate(parts, axis=1))
    y_a = u * jnp.concatenate(chunks, axis=0)

    z = dc_ref[...].astype(F32) * dx_ref[...].astype(F32)
    zp = dcp_ref[...].astype(F32) * dxp_ref[...].astype(F32)
    zp = jnp.where(i % tiles_per_seq == 0, jnp.zeros_like(zp), zp)
    zz = jnp.concatenate([zp, z], axis=0)
    cw = cw_ref[...]
    conv = cw[0:1, :] * zz[PREV_ROWS - 2:PREV_ROWS - 2 + tm, :] + cw[1:2, :] * zz[PREV_ROWS - 1:PREV_ROWS - 1 + tm, :] \
        + cw[2:3, :] * z
    y_d = db_ref[...].astype(F32) * conv

    ys = (y_a, yb_ref[...], yc_ref[...], y_d)
    merged = jnp.zeros((tm, D_MODEL), F32)
    for n in range(N_BRANCH):
        gate = jax.nn.sigmoid(g_ref[:, n * D_MODEL:(n + 1) * D_MODEL].astype(F32))
        merged = merged + gate * jnp.dot(ys[n].astype(BF16), wb_ref[n], preferred_element_type=F32)
    o_ref[...] = x_ref[...] + jnp.dot(merged.astype(BF16), wo_ref[...], preferred_element_type=F32)


def _merge(x2, proj, y_b, y_c, sgu_norm, sgu_w, sgu_bt, conv_w, w_branch, w_out, layer, seq, tm=512):
    t = x2.shape[0]
    tm = min(tm, seq)
    tiles_per_seq = seq // tm
    rb = tm // PREV_ROWS

    def col(width, off):
        return pl.BlockSpec((tm, width), lambda i: (i, off // width))

    def prev(off):
        return pl.BlockSpec((PREV_ROWS, 256), lambda i: (jnp.maximum(i * rb - 1, 0), off // 256))

    def full(shape):
        return pl.BlockSpec((None,) + shape, lambda i: (layer,) + (0,) * len(shape))

    kern = functools.partial(_merge_kernel, tm=tm, tiles_per_seq=tiles_per_seq)
    return pl.pallas_call(
        kern,
        grid=(t // tm,),
        in_specs=[pl.BlockSpec((tm, D_MODEL), lambda i: (i, 0)),
                  col(512, COL_AU), col(256, COL_DB), col(256, COL_DC), col(256, COL_DX),
                  prev(COL_DC), prev(COL_DX), col(4096, COL_G),
                  pl.BlockSpec((tm, 256), lambda i: (i, 0)), pl.BlockSpec((tm, 256), lambda i: (i, 0)),
                  full((1, BRANCH_W)),
                  full((SGU_GROUPS, SGU_CHUNK, SGU_CHUNK)), full((SGU_CHUNK, LANES)),
                  full((SUBLANES, BRANCH_W)), full((N_BRANCH, BRANCH_W, D_MODEL)), full((D_MODEL, D_MODEL))],
        out_specs=pl.BlockSpec((tm, D_MODEL), lambda i: (i, 0)),
        out_shape=jax.ShapeDtypeStruct((t, D_MODEL), F32),
        compiler_params=_vmem_params(("parallel",), 48),
        name="merge",
    )(x2, proj, proj, proj, proj, proj, proj, proj, y_b, y_c, sgu_norm, sgu_w, sgu_bt, conv_w, w_branch, w_out)


def _mlp_kernel(x_ref, g_ref, wu_ref, wd_ref, o_ref, h_ref):
    f = pl.program_id(1)

    @pl.when(f == 0)
    def _():
        x = x_ref[...]
        ms = jnp.mean(x * x, axis=-1, keepdims=True)
        h_ref[...] = (x * lax.rsqrt(ms + EPS) * g_ref[...]).astype(BF16)
        o_ref[...] = x

    up = jnp.maximum(jnp.dot(h_ref[...], wu_ref[...], preferred_element_type=F32), 0.0)
    o_ref[...] += jnp.dot((up * up).astype(BF16), wd_ref[...], preferred_element_type=F32)


def _mlp(x2, g, w_up, w_down, layer, tm=1024, tf=1024):
    t = x2.shape[0]
    tm = min(tm, t)
    return pl.pallas_call(
        _mlp_kernel,
        grid=(t // tm, D_FF // tf),
        in_specs=[pl.BlockSpec((tm, D_MODEL), lambda i, f: (i, 0)),
                  pl.BlockSpec((None, 1, D_MODEL), lambda i, f: (layer, 0, 0)),
                  pl.BlockSpec((None, D_MODEL, tf), lambda i, f: (layer, 0, f)),
                  pl.BlockSpec((None, tf, D_MODEL), lambda i, f: (layer, f, 0))],
        out_specs=pl.BlockSpec((tm, D_MODEL), lambda i, f: (i, 0)),
        out_shape=jax.ShapeDtypeStruct((t, D_MODEL), F32),
        scratch_shapes=[pltpu.VMEM((tm, D_MODEL), BF16)],
        compiler_params=_vmem_params(("parallel", "arbitrary"), 48),
        name="mlp",
    )(x2, g, w_up, w_down)


W_IN_SIZES = (256, 256, 256, 256, 256, 512, 64, 8, 256, 256, 256, 256, 4, 4, 256, 256, 256, 4096)
O_KI = sum(W_IN_SIZES[:6])
O_CQ = sum(W_IN_SIZES[:8])
O_CI = sum(W_IN_SIZES[:12])
O_DB = sum(W_IN_SIZES[:14])
O_G = sum(W_IN_SIZES[:17])


def _relayout_w_in(w_in):
    assert O_KI == COL_CQ and O_G - O_DB == COL_SMALL - COL_DB
    depth, d, _ = w_in.shape
    pad = jnp.zeros((depth, d, COL_G - COL_SMALL - (O_CQ - O_KI) - (O_DB - O_CI)), w_in.dtype)
    out = jnp.concatenate([w_in[..., :O_KI], w_in[..., O_CQ:O_CI], w_in[..., O_DB:O_G],
                           w_in[..., O_KI:O_CQ], w_in[..., O_CI:O_DB], pad, w_in[..., O_G:]], axis=-1)
    assert out.shape[-1] == PROJ_W
    return out.astype(BF16)


def _rope_tables(seq):
    half = ATT_HD // 2
    inv = jnp.float32(ROPE_THETA) ** (-jnp.arange(half, dtype=jnp.float32) * 2.0 / ATT_HD)
    ang = jnp.arange(seq, dtype=jnp.int32).astype(jnp.float32)[:, None] * inv[None, :]
    cos = jnp.cos(ang)
    sin = jnp.sin(ang)
    cos_t = jnp.concatenate([cos, cos, cos, cos], axis=1)
    sin_t = jnp.concatenate([-sin, sin, -sin, sin], axis=1)
    return cos_t, sin_t


def _forward(x, ln_mix, w_in, sgu_norm, sgu_w, sgu_b, q_norm, k_norm, kidx_norm, mlstm_i_bias, mlstm_f_bias,
             mlstm_norm, conv_w, w_branch, w_out, ln_mlp, w_up, w_down):
    batch, seq, d = x.shape
    depth = w_in.shape[0]
    x2 = x.reshape(batch * seq, d)
    cos_t, sin_t = _rope_tables(seq)
    w_in_r = _relayout_w_in(w_in)
    w_branch_b, w_out_b, w_up_b, w_down_b = (w.astype(BF16) for w in (w_branch, w_out, w_up, w_down))
    qn2, kn2, kin2 = (jnp.tile(g, (1, 2))[:, None, :] for g in (q_norm, k_norm, kidx_norm))
    logit_bound = (ATT_HD ** 0.5 * LOG2E * jnp.max(jnp.abs(q_norm), axis=1, keepdims=True)
                   * jnp.max(jnp.abs(k_norm), axis=1, keepdims=True))
    gbias = jnp.concatenate([jnp.zeros((depth, SM_CI), F32), mlstm_i_bias, mlstm_f_bias,
                             jnp.zeros((depth, LANES - SM_CF - MLSTM_HEADS), F32)], axis=1)[:, None, :]
    mnorm = jnp.broadcast_to(mlstm_norm[:, :, None], (depth, MLSTM_HEADS * MLSTM_HD, LANES))
    sgu_bt = jnp.pad(jnp.swapaxes(sgu_b, 1, 2), ((0, 0), (0, 0), (0, LANES - SGU_GROUPS)))
    conv_p = jnp.pad(conv_w, ((0, 0), (0, SUBLANES - CONV_WIDTH), (0, 0)))
    for l in range(depth):
        proj, small = _inproj(x2, ln_mix[:, None, :], w_in_r, l)
        q, k, v, qi, ki = _attprep(proj, small, cos_t, sin_t, qn2, kn2, kin2, l, seq)
        y_b = _dsa(logit_bound, qi, small, q, ki, k, v, l, batch, seq)
        y_c = _mlstm(proj, small, gbias, mnorm, l, batch, seq)
        x2 = _merge(x2, proj, y_b, y_c, sgu_norm[:, None, :], sgu_w, sgu_bt, conv_p, w_branch_b, w_out_b, l, seq)
        x2 = _mlp(x2, ln_mlp[:, None, :], w_up_b, w_down_b, l)
    return x2.reshape(batch, seq, d)


def kernel(x, ln_mix, w_in, sgu_norm, sgu_w, sgu_b, q_norm, k_norm, kidx_norm, mlstm_i_bias, mlstm_f_bias,
           mlstm_norm, conv_w, w_branch, w_out, ln_mlp, w_up, w_down):
    return _forward(x, ln_mix, w_in, sgu_norm, sgu_w, sgu_b, q_norm, k_norm, kidx_norm, mlstm_i_bias,
                    mlstm_f_bias, mlstm_norm, conv_w, w_branch, w_out, ln_mlp, w_up, w_down)
```

```python
import functools

import jax
import jax.numpy as jnp
from jax import lax
from jax.experimental import pallas as pl
from jax.experimental.pallas import tpu as pltpu

F32 = jnp.float32
BF16 = jnp.bfloat16

D_MODEL = 1024
N_BRANCH = 4
BRANCH_W = 256
SGU_GROUPS = 4
SGU_GD = BRANCH_W // SGU_GROUPS
SGU_CHUNK = 128
ATT_HEADS = 4
ATT_HD = 64
IDX_HEADS = 8
IDX_HD = 64
TOPK_MAX = 256
ROPE_THETA = 10000.0
MLSTM_HEADS = 4
MLSTM_HD = 64
MLSTM_CHUNK = 128
CONV_WIDTH = 3
D_FF = 4 * D_MODEL
EPS = 1e-6

LANES = 128
SUBLANES = 8
PREV_ROWS = 16

COL_AU = 0
COL_AV = 256
COL_Q = 512
COL_K = 768
COL_V = 1024
COL_QI = 1280
COL_CQ = 1792
COL_CK = 2048
COL_CV = 2304
COL_CO = 2560
COL_DB = 2816
COL_DC = 3072
COL_DX = 3328
COL_SMALL = 3584
COL_G = 4096
PROJ_W = 8192
SM_WI = IDX_HD
SM_CI = SM_WI + IDX_HEADS
SM_CF = SM_CI + MLSTM_HEADS

LOG2E = 1.4426950408889634
NEG_BIG = -1e30
M_INIT = -1e29
F32_MIN = -3.4028234663852886e38
F32_ABOVE_MIN = -3.4028232635611926e38
F32_TINY = 1.1754943508222875e-38
BISECT_FIRST_PASSES = 16
BISECT_PASSES_PER_ROUND = 2
MAX_BISECT_ROUNDS = 160
SOFTMAX_BOUND_MAX = 60.0
SCORE_UNROLL = 2
MLSTM_SEQS_PER_STEP = 4


def _vmem_params(sem, mib):
    return pltpu.CompilerParams(dimension_semantics=sem, vmem_limit_bytes=mib * 1024 * 1024)


def _inproj_kernel(x_ref, g_ref, w_ref, o_ref, sm_ref, h_ref, *, small_tile, small_off):
    j = pl.program_id(1)

    @pl.when(j == 0)
    def _():
        x = x_ref[...]
        ms = jnp.mean(x * x, axis=-1, keepdims=True)
        h_ref[...] = (x * lax.rsqrt(ms + EPS) * g_ref[...]).astype(BF16)

    res = jnp.dot(h_ref[...], w_ref[...], preferred_element_type=F32)
    o_ref[...] = res.astype(BF16)

    @pl.when(j == small_tile)
    def _():
        sm_ref[...] = res[:, small_off:small_off + LANES]


def _inproj(x2, g, w, layer, tm=2048, tn=1024):
    t = x2.shape[0]
    tm = min(tm, t)
    kern = functools.partial(_inproj_kernel, small_tile=COL_SMALL // tn, small_off=COL_SMALL % tn)
    return pl.pallas_call(
        kern,
        grid=(t // tm, PROJ_W // tn),
        in_specs=[
            pl.BlockSpec((tm, D_MODEL), lambda i, j: (i, 0)),
            pl.BlockSpec((None, 1, D_MODEL), lambda i, j: (layer, 0, 0)),
            pl.BlockSpec((None, D_MODEL, tn), lambda i, j: (layer, 0, j)),
        ],
        out_specs=[pl.BlockSpec((tm, tn), lambda i, j: (i, j)), pl.BlockSpec((tm, LANES), lambda i, j: (i, 0))],
        out_shape=[jax.ShapeDtypeStruct((t, PROJ_W), BF16), jax.ShapeDtypeStruct((t, LANES), F32)],
        scratch_shapes=[pltpu.VMEM((tm, D_MODEL), BF16)],
        compiler_params=_vmem_params(("parallel", "arbitrary"), 56),
        name="inproj",
    )(x2, g, w)


def _rope_slab(x, cos, sin_signed, first_half):
    x_hi = pltpu.roll(x, LANES - ATT_HD // 2, axis=1)
    x_lo = pltpu.roll(x, ATT_HD // 2, axis=1)
    rot = jnp.where(first_half, x_hi, x_lo)
    return x * cos + rot * sin_signed


def _head_sumsq(x2, lane, n_heads):
    out = []
    for h in range(n_heads):
        m = (lane >= h * ATT_HD) & (lane < (h + 1) * ATT_HD)
        out.append(jnp.sum(jnp.where(m, x2, 0.0), axis=-1, keepdims=True))
    return out


def _attprep_kernel(q_ref, k_ref, v_ref, qi0_ref, qi1_ref, sm_ref, cos_ref, sin_ref, qn_ref, kn_ref, kin_ref,
                    qo_ref, ko_ref, vo_ref, qio_ref, kio_ref):
    cos = cos_ref[...]
    sin = sin_ref[...]
    rows = cos.shape[0]
    lane = lax.broadcasted_iota(jnp.int32, (rows, LANES), 1)
    first_half = (lane % ATT_HD) < (ATT_HD // 2)
    head0 = lane < ATT_HD

    def norm_rope(ref, g_ref, scale):
        slabs = []
        for s in range(ref.shape[1] // LANES):
            x = ref[:, s * LANES:(s + 1) * LANES].astype(F32)
            ss = _head_sumsq(x * x, lane, 2)
            r0 = lax.rsqrt(ss[0] * (1.0 / ATT_HD) + EPS)
            r1 = lax.rsqrt(ss[1] * (1.0 / ATT_HD) + EPS)
            y = x * jnp.where(head0, r0, r1) * g_ref[...]
            y = _rope_slab(y, cos, sin, first_half)
            if scale != 1.0:
                y = y * scale
            slabs.append(y)
        return slabs

    for s, y in enumerate(norm_rope(q_ref, qn_ref, ATT_HD ** -0.5 * LOG2E)):
        qo_ref[s * LANES:(s + 1) * LANES, :] = y.T.astype(BF16)
    for s, y in enumerate(norm_rope(k_ref, kn_ref, 1.0)):
        ko_ref[2 * s] = y[:, :ATT_HD].astype(BF16)
        ko_ref[2 * s + 1] = y[:, ATT_HD:].astype(BF16)
    vo_ref[...] = v_ref[...].astype(F32).T.astype(BF16)

    for s in range(IDX_HEADS * IDX_HD // LANES):
        qi_ref = (qi0_ref, qi1_ref)[s // 2]
        y = _rope_slab(qi_ref[:, (s % 2) * LANES:(s % 2 + 1) * LANES].astype(F32), cos, sin, first_half)
        qio_ref[s * LANES:(s + 1) * LANES, :] = (y * (IDX_HD ** -0.5)).T.astype(BF16)

    sm = sm_ref[...]
    mu = jnp.sum(jnp.where(head0, sm, 0.0), axis=-1, keepdims=True) * (1.0 / IDX_HD)
    d = sm - mu
    var = jnp.sum(jnp.where(head0, d * d, 0.0), axis=-1, keepdims=True) * (1.0 / IDX_HD)
    y = d * lax.rsqrt(var + EPS) * kin_ref[...]
    y = _rope_slab(y, cos, sin, first_half)
    kio_ref[...] = y[:, :IDX_HD].astype(BF16)


def _attprep(proj, small, cos, sin, qn, kn, kin, layer, seq, tm=512):
    t = proj.shape[0]
    tm = min(tm, seq)
    npos = seq // tm

    def col(width, off):
        return pl.BlockSpec((tm, width), lambda i: (i, off // width))

    vec = pl.BlockSpec((None, 1, LANES), lambda i: (layer, 0, 0))
    tab = pl.BlockSpec((tm, LANES), lambda i: (i % npos, 0))
    return pl.pallas_call(
        _attprep_kernel,
        grid=(t // tm,),
        in_specs=[col(256, COL_Q), col(256, COL_K), col(256, COL_V), col(256, COL_QI), col(256, COL_QI + 256),
                  pl.BlockSpec((tm, LANES), lambda i: (i, 0)), tab, tab, vec, vec, vec],
        out_specs=[pl.BlockSpec((256, tm), lambda i: (0, i)),
                   pl.BlockSpec((ATT_HEADS, tm, ATT_HD), lambda i: (0, i, 0)),
                   pl.BlockSpec((256, tm), lambda i: (0, i)), pl.BlockSpec((512, tm), lambda i: (0, i)),
                   pl.BlockSpec((tm, IDX_HD), lambda i: (i, 0))],
        out_shape=[jax.ShapeDtypeStruct((256, t), BF16), jax.ShapeDtypeStruct((ATT_HEADS, t, ATT_HD), BF16),
                   jax.ShapeDtypeStruct((256, t), BF16), jax.ShapeDtypeStruct((512, t), BF16),
                   jax.ShapeDtypeStruct((t, IDX_HD), BF16)],
        compiler_params=_vmem_params(("parallel",), 32),
        name="attprep",
    )(proj, proj, proj, proj, proj, small, cos, sin, qn, kn, kin)


def _dsa_kernel(mb_ref, qi_ref, sm_ref, q_ref, ki_ref, k_ref, vt_ref, o_ref, sc_ref, acc_ref, s0_ref, s1_ref,
                *, tq, n_sel, seq, layer):
    i = pl.program_id(1)
    tk1 = min(128, tq)
    tk2 = tq
    tk3 = tq
    n_keys = (i + 1) * tq
    qpos = i * tq + lax.broadcasted_iota(jnp.int32, (1, tq), 1)

    wt = (sm_ref[...] * (IDX_HEADS ** -0.5)).T
    qi_h = [qi_ref[h * IDX_HD:(h + 1) * IDX_HD, :] for h in range(IDX_HEADS)]
    w_h = [wt[SM_WI + h:SM_WI + h + 1, :] for h in range(IDX_HEADS)]
    kidx1 = lax.broadcasted_iota(jnp.int32, (tk1, tq), 0)

    def score_chunk(c, amax):
        for u in range(SCORE_UNROLL):
            start = pl.multiple_of((c * SCORE_UNROLL + u) * tk1, tk1)
            kc = ki_ref[pl.ds(start, tk1), :]
            acc = jnp.zeros((tk1, tq), F32)
            for h in range(IDX_HEADS):
                s = jnp.dot(kc, qi_h[h], preferred_element_type=F32)
                acc = acc + w_h[h] * jnp.maximum(s, 0.0)
            amax = jnp.maximum(amax, jnp.max(jnp.abs(acc), axis=0, keepdims=True))
            sc_ref[pl.ds(start, tk1), :] = jnp.where(kidx1 + start <= qpos, acc, F32_MIN)
        return amax

    amax = lax.fori_loop(0, n_keys // (tk1 * SCORE_UNROLL), score_chunk, jnp.zeros((1, tq), F32))

    def count(pred_fn):
        def body(c, acc):
            start = pl.multiple_of(c * tk2, tk2)
            x = sc_ref[pl.ds(start, tk2), :]
            return acc + jnp.sum(pred_fn(x).reshape(tk2 // 32, 32, tq), axis=0)
        acc = lax.fori_loop(0, n_keys // tk2, body, jnp.zeros((32, tq), F32))
        return jnp.sum(acc, axis=0, keepdims=True)

    n_pos = count(lambda x: jnp.where(x > 0.0, 1.0, 0.0))
    n_nn = count(lambda x: jnp.where(x >= 0.0, 1.0, 0.0))
    n_causal = (qpos + 1).astype(F32)
    k_sel = jnp.float32(n_sel)
    take_all = n_causal <= k_sel
    at_zero = (n_pos < k_sel) & (n_nn >= k_sel)
    wide = amax + amax * 2.0 ** -20 + F32_TINY
    positive = n_pos >= k_sel
    lo0 = jnp.where(positive, 0.0, -wide)
    hi0 = jnp.where(positive, wide, 0.0)
    nhi0 = jnp.where(positive, 0.0, n_nn)
    thr0 = jnp.where(take_all, F32_ABOVE_MIN, 0.0)
    tie0 = jnp.where(at_zero & (n_nn > k_sel) & jnp.logical_not(take_all), 1.0, 0.0)
    need0 = k_sel - n_pos
    act0 = jnp.where(take_all | at_zero, 0.0, 1.0)

    def bisect_pass(state):
        lo, hi, n_hi, thr, tie, need, act = state
        cand = 0.5 * lo + 0.5 * hi
        cnt = count(lambda x: jnp.where(x >= cand, 1.0, 0.0))
        live = act > 0.0
        stuck = live & ((cand <= lo) | (cand >= hi))
        hit = live & jnp.logical_not(stuck) & (cnt == k_sel)
        up = live & jnp.logical_not(stuck) & (cnt > k_sel)
        down = live & jnp.logical_not(stuck) & (cnt < k_sel)
        thr = jnp.where(hit, cand, jnp.where(stuck, lo, thr))
        tie = jnp.where(stuck, 1.0, tie)
        need = jnp.where(stuck, k_sel - n_hi, need)
        lo = jnp.where(up, cand, lo)
        hi = jnp.where(down, cand, hi)
        n_hi = jnp.where(down, cnt, n_hi)
        act = jnp.where(hit | stuck, 0.0, act)
        return lo, hi, n_hi, thr, tie, need, act

    def search_cond(carry):
        it, state = carry
        return (it < MAX_BISECT_ROUNDS) & (jnp.max(state[6]) > 0.0)

    def search_body(carry):
        it, state = carry
        for _ in range(BISECT_PASSES_PER_ROUND):
            state = bisect_pass(state)
        return it + 1, state

    state = lax.fori_loop(0, BISECT_FIRST_PASSES, lambda _, st: bisect_pass(st),
                          (lo0, hi0, nhi0, thr0, tie0, need0, act0))
    _, state = lax.while_loop(search_cond, search_body, (jnp.int32(0), state))
    thr, tie, need = state[3], state[4], state[5]

    tri = (lax.broadcasted_iota(jnp.int32, (tk2, tk2), 0) >= lax.broadcasted_iota(jnp.int32, (tk2, tk2), 1))
    tri = jnp.where(tri, 1.0, 0.0).astype(BF16)

    def drop_chunk(c, run):
        start = pl.multiple_of(c * tk2, tk2)
        x = sc_ref[pl.ds(start, tk2), :]
        eq = jnp.where(tie > 0.0, jnp.where(x == thr, 1.0, 0.0), 0.0)
        rank = run + jnp.dot(tri, eq.astype(BF16), preferred_element_type=F32)
        sc_ref[pl.ds(start, tk2), :] = jnp.where(eq > 0.0, jnp.where(rank > need, F32_MIN, x), x)
        return rank[tk2 - 1:tk2, :]

    @pl.when(jnp.max(tie) > 0.0)
    def _():
        lax.fori_loop(0, n_keys // tk2, drop_chunk, jnp.zeros((1, tq), F32))

    q_h = [q_ref[h * ATT_HD:(h + 1) * ATT_HD, :] for h in range(ATT_HEADS)]
    acc_ref[...] = jnp.zeros(acc_ref.shape, F32)
    bound = mb_ref[layer, 0]

    def scores_into(buf_ref, c):
        start = pl.multiple_of(c * tk3, tk3)
        for h in range(ATT_HEADS):
            buf_ref[h] = jnp.dot(k_ref[h, pl.ds(start, tk3), :], q_h[h], preferred_element_type=F32)

    def attend_bounded(buf_ref, c, live, l_run):
        start = pl.multiple_of(c * tk3, tk3)
        sel = (sc_ref[pl.ds(start, tk3), :] >= thr) & live
        l_out = []
        for h in range(ATT_HEADS):
            p = jnp.where(sel, jnp.exp2(buf_ref[h] - bound), 0.0)
            l_out.append(l_run[h] + jnp.sum(p, axis=0, keepdims=True))
            vt = vt_ref[h * ATT_HD:(h + 1) * ATT_HD, pl.ds(start, tk3)]
            acc_ref[h * ATT_HD:(h + 1) * ATT_HD, :] += jnp.dot(vt, p.astype(BF16), preferred_element_type=F32)
        return tuple(l_out)

    n_chunks3 = n_keys // tk3
    last = n_chunks3 - 1

    def attn_bounded_pair(j, l_run):
        c_a = 2 * j
        c_b = jnp.minimum(c_a + 1, last)
        scores_into(s1_ref, c_b)
        l_run = attend_bounded(s0_ref, c_a, True, l_run)
        scores_into(s0_ref, jnp.minimum(c_a + 2, last))
        return attend_bounded(s1_ref, c_b, c_a + 1 <= last, l_run)

    def attn_online(c, carry):
        m_run, l_run = carry
        start = pl.multiple_of(c * tk3, tk3)
        sel = sc_ref[pl.ds(start, tk3), :] >= thr
        s_all = [jnp.dot(k_ref[h, pl.ds(start, tk3), :], q_h[h], preferred_element_type=F32)
                 for h in range(ATT_HEADS)]
        m_out, l_out, alphas, ps = [], [], [], []
        for h in range(ATT_HEADS):
            s = jnp.where(sel, s_all[h], NEG_BIG)
            m_new = jnp.maximum(m_run[h], jnp.max(s, axis=0, keepdims=True))
            alpha = jnp.exp2(m_run[h] - m_new)
            p = jnp.exp2(s - m_new)
            l_out.append(alpha * l_run[h] + jnp.sum(p, axis=0, keepdims=True))
            m_out.append(m_new)
            alphas.append(alpha)
            ps.append(p.astype(BF16))
        for h in range(ATT_HEADS):
            vt = vt_ref[h * ATT_HD:(h + 1) * ATT_HD, pl.ds(start, tk3)]
            pv = jnp.dot(vt, ps[h], preferred_element_type=F32)
            acc_ref[h * ATT_HD:(h + 1) * ATT_HD, :] = alphas[h] * acc_ref[h * ATT_HD:(h + 1) * ATT_HD, :] + pv
        return tuple(m_out), tuple(l_out)

    zeros = tuple(jnp.zeros((1, tq), F32) for _ in range(ATT_HEADS))

    def run_bounded():
        scores_into(s0_ref, 0)
        return lax.fori_loop(0, (n_chunks3 + 1) // 2, attn_bounded_pair, zeros)

    def run_online():
        init = (tuple(jnp.full((1, tq), M_INIT, F32) for _ in range(ATT_HEADS)), zeros)
        return lax.fori_loop(0, n_keys // tk3, attn_online, init)[1]

    l_fin = lax.cond(bound < SOFTMAX_BOUND_MAX, run_bounded, run_online)
    outs = [acc_ref[h * ATT_HD:(h + 1) * ATT_HD, :] / l_fin[h] for h in range(ATT_HEADS)]
    o_ref[...] = jnp.concatenate(outs, axis=0).T.astype(BF16)


def _dsa(logit_bound, qi, small, q, ki, k, vt, layer, batch, seq, tq=256):
    t = q.shape[1]
    tq = min(tq, seq)
    nq = seq // tq
    n_sel = min(TOPK_MAX, seq // 4)
    assert tq >= n_sel and seq % tq == 0 and tq % (128 * SCORE_UNROLL) == 0
    kern = functools.partial(_dsa_kernel, tq=tq, n_sel=n_sel, seq=seq, layer=layer)
    return pl.pallas_call(
        kern,
        grid=(batch, nq),
        in_specs=[
            pl.BlockSpec(memory_space=pltpu.SMEM),
            pl.BlockSpec((512, tq), lambda b, i: (0, b * nq + i)),
            pl.BlockSpec((tq, LANES), lambda b, i: (b * nq + i, 0)),
            pl.BlockSpec((256, tq), lambda b, i: (0, b * nq + i)),
            pl.BlockSpec((seq, IDX_HD), lambda b, i: (b, 0)),
            pl.BlockSpec((ATT_HEADS, seq, ATT_HD), lambda b, i: (0, b, 0)),
            pl.BlockSpec((256, seq), lambda b, i: (0, b)),
        ],
        out_specs=pl.BlockSpec((tq, 256), lambda b, i: (b * nq + i, 0)),
        out_shape=jax.ShapeDtypeStruct((t, 256), BF16),
        scratch_shapes=[pltpu.VMEM((seq, tq), F32), pltpu.VMEM((ATT_HEADS * ATT_HD, tq), F32),
                        pltpu.VMEM((ATT_HEADS, tq, tq), F32), pltpu.VMEM((ATT_HEADS, tq, tq), F32)],
        compiler_params=_vmem_params(("parallel", "arbitrary"), 48),
        name="dsa",
    )(logit_bound, qi, small, q, ki, k, vt)


def _mlstm_kernel(q_ref, k_ref, v_ref, o_ref, sm_ref, gb_ref, ng_ref, y_ref, c_ref, m_ref, *, nb):
    L = MLSTM_CHUNK
    hd = MLSTM_HD

    @pl.when(pl.program_id(1) == 0)
    def _():
        c_ref[...] = jnp.zeros_like(c_ref)
        m_ref[...] = jnp.zeros_like(m_ref)

    r_i = lax.broadcasted_iota(jnp.int32, (L, L), 0)
    c_i = lax.broadcasted_iota(jnp.int32, (L, L), 1)
    tril_f = (r_i >= c_i).astype(F32)
    causal = r_i <= c_i
    is_f = (c_i >= SM_CF) & (c_i < SM_CF + MLSTM_HEADS)
    ones = jnp.ones((hd, L), BF16)
    chains = [(bb, h) for bb in range(nb) for h in range(MLSTM_HEADS)]
    xc, xr, q_t, v_t, o_t = [], [], [], [], []
    for bb in range(nb):
        g = sm_ref[bb] + gb_ref[...]
        lf = jax.nn.log_sigmoid(g)
        bcum = jnp.dot(tril_f, lf, precision=lax.Precision.HIGHEST, preferred_element_type=F32)
        xc.append(jnp.where(is_f, bcum, g))
        xr.append(xc[bb].T)
        q_t.append(q_ref[bb].astype(F32).T.astype(BF16))
        v_t.append(v_ref[bb].astype(F32).T.astype(BF16))
        o_t.append(o_ref[bb].astype(F32).T)

    qh_t, kh, v_ext_t, qk, gq = {}, {}, {}, {}, {}
    for st, (bb, h) in enumerate(chains):
        qh_t[st] = q_t[bb][h * hd:(h + 1) * hd, :]
        kh[st] = (k_ref[bb, :, h * hd:(h + 1) * hd].astype(F32) * (hd ** -0.5)).astype(BF16)
        v_ext_t[st] = jnp.concatenate([v_t[bb][h * hd:(h + 1) * hd, :], ones], axis=0)
        qk[st] = jnp.dot(kh[st], qh_t[st], preferred_element_type=F32)
        gq[st] = jnp.dot(c_ref[st].astype(BF16), qh_t[st], preferred_element_type=F32)

    a, w_inter, mjs, b_rows, ig_rows, m_sts = {}, {}, {}, {}, {}, {}
    for st, (bb, h) in enumerate(chains):
        b_row = xr[bb][SM_CF + h:SM_CF + h + 1, :]
        ig_row = xr[bb][SM_CI + h:SM_CI + h + 1, :]
        src_col = xc[bb][:, SM_CI + h:SM_CI + h + 1] - xc[bb][:, SM_CF + h:SM_CF + h + 1]
        m_st = m_ref[st][0:1, 0:1]
        dm = jnp.where(causal, b_row + src_col, -jnp.inf)
        inter = b_row + m_st
        mj = jnp.maximum(inter, jnp.max(dm, axis=0, keepdims=True))
        a[st] = (jnp.exp(dm - mj) * qk[st]).astype(BF16)
        w_inter[st] = jnp.exp(inter - mj)
        mjs[st], b_rows[st], ig_rows[st], m_sts[st] = mj, b_row, ig_row, m_st

    av = {st: jnp.dot(v_ext_t[st], a[st], preferred_element_type=F32) for st in a}

    vw, decays = {}, {}
    for bb in range(nb):
        outs = []
        for h in range(MLSTM_HEADS):
            st = bb * MLSTM_HEADS + h
            mj, b_row = mjs[st], b_rows[st]
            r = w_inter[st] * gq[st] + av[st]
            num = r[:hd, :]
            den = r[hd:hd + 1, :]
            hout = num / jnp.maximum(jnp.abs(den), jnp.exp(-mj))
            b_last = b_row[:, L - 1:L]
            m_new = mj[:, L - 1:L]
            wk = jnp.exp(b_last - b_row + ig_rows[st] - m_new)
            decays[st] = jnp.exp(b_last + m_sts[st] - m_new)
            vw[st] = (v_ext_t[st] * wk).astype(BF16)
            m_ref[st] = jnp.broadcast_to(m_new, m_ref.shape[1:])
            ms = jnp.mean(hout * hout, axis=0, keepdims=True)
            hn = hout * lax.rsqrt(ms + EPS) * ng_ref[h * hd:(h + 1) * hd, :]
            outs.append(jax.nn.sigmoid(o_t[bb][h * hd:(h + 1) * hd, :]) * hn)
        y_ref[bb] = jnp.concatenate(outs, axis=0).T.astype(BF16)

    for st in vw:
        c_ref[st] = decays[st] * c_ref[st] + jnp.dot(vw[st], kh[st], preferred_element_type=F32)


def _mlstm(proj, small, gbias, norm_g, layer, batch, seq, nb=MLSTM_SEQS_PER_STEP):
    t = proj.shape[0]
    L = MLSTM_CHUNK
    nc = seq // L
    nb = min(nb, batch)
    assert batch % nb == 0
    proj3 = proj.reshape(batch, seq, PROJ_W)
    small3 = small.reshape(batch, seq, LANES)

    def col(off):
        return pl.BlockSpec((nb, L, 256), lambda b, c: (b, c, off // 256))

    kern = functools.partial(_mlstm_kernel, nb=nb)
    y = pl.pallas_call(
        kern,
        grid=(batch // nb, nc),
        in_specs=[col(COL_CQ), col(COL_CK), col(COL_CV), col(COL_CO),
                  pl.BlockSpec((nb, L, LANES), lambda b, c: (b, c, 0)),
                  pl.BlockSpec((None, 1, LANES), lambda b, c: (layer, 0, 0)),
                  pl.BlockSpec((None, 256, LANES), lambda b, c: (layer, 0, 0))],
        out_specs=pl.BlockSpec((nb, L, 256), lambda b, c: (b, c, 0)),
        out_shape=jax.ShapeDtypeStruct((batch, seq, 256), BF16),
        scratch_shapes=[pltpu.VMEM((nb * MLSTM_HEADS, LANES, MLSTM_HD), F32),
                        pltpu.VMEM((nb * MLSTM_HEADS, SUBLANES, LANES), F32)],
        compiler_params=_vmem_params(("parallel", "arbitrary"), 32),
        name="mlstm",
    )(proj3, proj3, proj3, proj3, small3, gbias, norm_g)
    return y.reshape(t, 256)


def _merge_kernel(x_ref, uv_ref, db_ref, dc_ref, dx_ref, dcp_ref, dxp_ref, g_ref, yb_ref, yc_ref,
                  sn_ref, sw_ref, sb_ref, cw_ref, wb_ref, wo_ref, o_ref, *, tm, tiles_per_seq):
    i = pl.program_id(0)
    L = SGU_CHUNK
    u = jax.nn.gelu(uv_ref[:, :BRANCH_W].astype(F32))
    v = jax.nn.gelu(uv_ref[:, BRANCH_W:].astype(F32))
    ms = jnp.mean(v * v, axis=-1, keepdims=True)
    vn = (v * lax.rsqrt(ms + EPS) * sn_ref[...]).astype(BF16)
    r_i = lax.broadcasted_iota(jnp.int32, (L, L), 0)
    c_i = lax.broadcasted_iota(jnp.int32, (L, L), 1)
    tril = r_i >= c_i
    wm = [jnp.where(tril, sw_ref[g], 0.0).astype(BF16) for g in range(SGU_GROUPS)]
    sb = sb_ref[...]
    chunks = []
    for c in range(tm // L):
        parts = []
        for g in range(SGU_GROUPS):
            vg = vn[c * L:(c + 1) * L, g * SGU_GD:(g + 1) * SGU_GD]
            parts.append(jnp.dot(wm[g], vg, preferred_element_type=F32) + sb[:, g:g + 1])
        chunks.append(jnp.concatenate(parts, axis=1))
    y_a = u * jnp.concatenate(chunks, axis=0)

    z = dc_ref[...].astype(F32) * dx_ref[...].astype(F32)
    zp = dcp_ref[...].astype(F32) * dxp_ref[...].astype(F32)
    zp = jnp.where(i % tiles_per_seq == 0, jnp.zeros_like(zp), zp)
    zz = jnp.concatenate([zp, z], axis=0)
    cw = cw_ref[...]
    conv = cw[0:1, :] * zz[PREV_ROWS - 2:PREV_ROWS - 2 + tm, :] + cw[1:2, :] * zz[PREV_ROWS - 1:PREV_ROWS - 1 + tm, :] \
        + cw[2:3, :] * z
    y_d = db_ref[...].astype(F32) * conv

    ys = (y_a, yb_ref[...], yc_ref[...], y_d)
    merged = jnp.zeros((tm, D_MODEL), F32)
    for n in range(N_BRANCH):
        gate = 0.5 * jnp.tanh(0.5 * g_ref[:, n * D_MODEL:(n + 1) * D_MODEL].astype(F32)) + 0.5
        merged = merged + gate * jnp.dot(ys[n].astype(BF16), wb_ref[n], preferred_element_type=F32)
    o_ref[...] = x_ref[...] + jnp.dot(merged.astype(BF16), wo_ref[...], preferred_element_type=F32)


def _merge(x2, proj, y_b, y_c, sgu_norm, sgu_w, sgu_bt, conv_w, w_branch, w_out, layer, seq, tm=512):
    t = x2.shape[0]
    tm = min(tm, seq)
    tiles_per_seq = seq // tm
    rb = tm // PREV_ROWS

    def col(width, off):
        return pl.BlockSpec((tm, width), lambda i: (i, off // width))

    def prev(off):
        return pl.BlockSpec((PREV_ROWS, 256), lambda i: (jnp.maximum(i * rb - 1, 0), off // 256))

    def full(shape):
        return pl.BlockSpec((None,) + shape, lambda i: (layer,) + (0,) * len(shape))

    kern = functools.partial(_merge_kernel, tm=tm, tiles_per_seq=tiles_per_seq)
    return pl.pallas_call(
        kern,
        grid=(t // tm,),
        in_specs=[pl.BlockSpec((tm, D_MODEL), lambda i: (i, 0)),
                  col(512, COL_AU), col(256, COL_DB), col(256, COL_DC), col(256, COL_DX),
                  prev(COL_DC), prev(COL_DX), col(4096, COL_G),
                  pl.BlockSpec((tm, 256), lambda i: (i, 0)), pl.BlockSpec((tm, 256), lambda i: (i, 0)),
                  full((1, BRANCH_W)),
                  full((SGU_GROUPS, SGU_CHUNK, SGU_CHUNK)), full((SGU_CHUNK, LANES)),
                  full((SUBLANES, BRANCH_W)), full((N_BRANCH, BRANCH_W, D_MODEL)), full((D_MODEL, D_MODEL))],
        out_specs=pl.BlockSpec((tm, D_MODEL), lambda i: (i, 0)),
        out_shape=jax.ShapeDtypeStruct((t, D_MODEL), F32),
        compiler_params=_vmem_params(("parallel",), 48),
        name="merge",
    )(x2, proj, proj, proj, proj, proj, proj, proj, y_b, y_c, sgu_norm, sgu_w, sgu_bt, conv_w, w_branch, w_out)


def _mlp_kernel(x_ref, g_ref, wu_ref, wd_ref, o_ref, h_ref):
    f = pl.program_id(1)

    @pl.when(f == 0)
    def _():
        x = x_ref[...]
        ms = jnp.mean(x * x, axis=-1, keepdims=True)
        h_ref[...] = (x * lax.rsqrt(ms + EPS) * g_ref[...]).astype(BF16)
        o_ref[...] = x

    up = jnp.maximum(jnp.dot(h_ref[...], wu_ref[...], preferred_element_type=F32), 0.0)
    o_ref[...] += jnp.dot((up * up).astype(BF16), wd_ref[...], preferred_element_type=F32)


def _mlp(x2, g, w_up, w_down, layer, tm=1024, tf=2048):
    t = x2.shape[0]
    tm = min(tm, t)
    return pl.pallas_call(
        _mlp_kernel,
        grid=(t // tm, D_FF // tf),
        in_specs=[pl.BlockSpec((tm, D_MODEL), lambda i, f: (i, 0)),
                  pl.BlockSpec((None, 1, D_MODEL), lambda i, f: (layer, 0, 0)),
                  pl.BlockSpec((None, D_MODEL, tf), lambda i, f: (layer, 0, f)),
                  pl.BlockSpec((None, tf, D_MODEL), lambda i, f: (layer, f, 0))],
        out_specs=pl.BlockSpec((tm, D_MODEL), lambda i, f: (i, 0)),
        out_shape=jax.ShapeDtypeStruct((t, D_MODEL), F32),
        scratch_shapes=[pltpu.VMEM((tm, D_MODEL), BF16)],
        compiler_params=_vmem_params(("parallel", "arbitrary"), 48),
        name="mlp",
    )(x2, g, w_up, w_down)


W_IN_SIZES = (256, 256, 256, 256, 256, 512, 64, 8, 256, 256, 256, 256, 4, 4, 256, 256, 256, 4096)
O_KI = sum(W_IN_SIZES[:6])
O_CQ = sum(W_IN_SIZES[:8])
O_CI = sum(W_IN_SIZES[:12])
O_DB = sum(W_IN_SIZES[:14])
O_G = sum(W_IN_SIZES[:17])


def _relayout_kernel(w_ref, o_ref):
    x = w_ref[...]
    rows = x.shape[0]
    small = jnp.concatenate([x[:, O_KI:O_CQ], x[:, O_CI:O_DB],
                             jnp.zeros((rows, LANES - (O_CQ - O_KI) - (O_DB - O_CI)), x.dtype)], axis=1)
    o_ref[:, :COL_CQ] = x[:, :O_KI].astype(BF16)
    o_ref[:, COL_CQ:COL_DB] = x[:, O_CQ:O_CI].astype(BF16)
    o_ref[:, COL_DB:COL_SMALL] = x[:, O_DB:O_G].astype(BF16)
    o_ref[:, COL_SMALL:COL_SMALL + LANES] = small.astype(BF16)
    o_ref[:, COL_SMALL + LANES:COL_G] = jnp.zeros((rows, COL_G - COL_SMALL - LANES), BF16)
    o_ref[:, COL_G:] = x[:, O_G:].astype(BF16)


def _relayout_w_in(w_in, tr=128):
    assert O_KI == COL_CQ and O_CI - O_CQ == COL_DB - COL_CQ and O_G - O_DB == COL_SMALL - COL_DB
    depth, d, in_w = w_in.shape
    return pl.pallas_call(
        _relayout_kernel,
        grid=(depth, d // tr),
        in_specs=[pl.BlockSpec((None, tr, in_w), lambda l, i: (l, i, 0))],
        out_specs=pl.BlockSpec((None, tr, PROJ_W), lambda l, i: (l, i, 0)),
        out_shape=jax.ShapeDtypeStruct((depth, d, PROJ_W), BF16),
        compiler_params=_vmem_params(("parallel", "parallel"), 32),
        name="relayout",
    )(w_in)


def _rope_tables(seq):
    half = ATT_HD // 2
    inv = jnp.float32(ROPE_THETA) ** (-jnp.arange(half, dtype=jnp.float32) * 2.0 / ATT_HD)
    ang = jnp.arange(seq, dtype=jnp.int32).astype(jnp.float32)[:, None] * inv[None, :]
    cos = jnp.cos(ang)
    sin = jnp.sin(ang)
    cos_t = jnp.concatenate([cos, cos, cos, cos], axis=1)
    sin_t = jnp.concatenate([-sin, sin, -sin, sin], axis=1)
    return cos_t, sin_t


def _forward(x, ln_mix, w_in, sgu_norm, sgu_w, sgu_b, q_norm, k_norm, kidx_norm, mlstm_i_bias, mlstm_f_bias,
             mlstm_norm, conv_w, w_branch, w_out, ln_mlp, w_up, w_down):
    batch, seq, d = x.shape
    depth = w_in.shape[0]
    x2 = x.reshape(batch * seq, d)
    cos_t, sin_t = _rope_tables(seq)
    w_in_r = _relayout_w_in(w_in)
    w_branch_b, w_out_b, w_up_b, w_down_b = (w.astype(BF16) for w in (w_branch, w_out, w_up, w_down))
    qn2, kn2, kin2 = (jnp.tile(g, (1, 2))[:, None, :] for g in (q_norm, k_norm, kidx_norm))
    logit_bound = (ATT_HD ** 0.5 * LOG2E * jnp.max(jnp.abs(q_norm), axis=1, keepdims=True)
                   * jnp.max(jnp.abs(k_norm), axis=1, keepdims=True))
    gbias = jnp.concatenate([jnp.zeros((depth, SM_CI), F32), mlstm_i_bias, mlstm_f_bias,
                             jnp.zeros((depth, LANES - SM_CF - MLSTM_HEADS), F32)], axis=1)[:, None, :]
    mnorm = jnp.broadcast_to(mlstm_norm[:, :, None], (depth, MLSTM_HEADS * MLSTM_HD, LANES))
    sgu_bt = jnp.pad(jnp.swapaxes(sgu_b, 1, 2), ((0, 0), (0, 0), (0, LANES - SGU_GROUPS)))
    conv_p = jnp.pad(conv_w, ((0, 0), (0, SUBLANES - CONV_WIDTH), (0, 0)))
    for l in range(depth):
        proj, small = _inproj(x2, ln_mix[:, None, :], w_in_r, l)
        q, k, v, qi, ki = _attprep(proj, small, cos_t, sin_t, qn2, kn2, kin2, l, seq)
        y_b = _dsa(logit_bound, qi, small, q, ki, k, v, l, batch, seq)
        y_c = _mlstm(proj, small, gbias, mnorm, l, batch, seq)
        x2 = _merge(x2, proj, y_b, y_c, sgu_norm[:, None, :], sgu_w, sgu_bt, conv_p, w_branch_b, w_out_b, l, seq)
        x2 = _mlp(x2, ln_mlp[:, None, :], w_up_b, w_down_b, l)
    return x2.reshape(batch, seq, d)


def kernel(x, ln_mix, w_in, sgu_norm, sgu_w, sgu_b, q_norm, k_norm, kidx_norm, mlstm_i_bias, mlstm_f_bias,
           mlstm_norm, conv_w, w_branch, w_out, ln_mlp, w_up, w_down):
    return _forward(x, ln_mix, w_in, sgu_norm, sgu_w, sgu_b, q_norm, k_norm, kidx_norm, mlstm_i_bias,
                    mlstm_f_bias, mlstm_norm, conv_w, w_branch, w_out, ln_mlp, w_up, w_down)
```

```python
import functools

import jax
import jax.numpy as jnp
from jax import lax
from jax.experimental import pallas as pl
from jax.experimental.pallas import tpu as pltpu

F32 = jnp.float32
BF16 = jnp.bfloat16

D_MODEL = 1024
N_BRANCH = 4
BRANCH_W = 256
SGU_GROUPS = 4
SGU_GD = BRANCH_W // SGU_GROUPS
SGU_CHUNK = 128
ATT_HEADS = 4
ATT_HD = 64
IDX_HEADS = 8
IDX_HD = 64
TOPK_MAX = 256
ROPE_THETA = 10000.0
MLSTM_HEADS = 4
MLSTM_HD = 64
MLSTM_CHUNK = 128
CONV_WIDTH = 3
D_FF = 4 * D_MODEL
EPS = 1e-6

LANES = 128
SUBLANES = 8
PREV_ROWS = 16

COL_AU = 0
COL_AV = 256
COL_Q = 512
COL_K = 768
COL_V = 1024
COL_QI = 1280
COL_CQ = 1792
COL_CK = 2048
COL_CV = 2304
COL_CO = 2560
COL_DB = 2816
COL_DC = 3072
COL_DX = 3328
COL_SMALL = 3584
COL_G = 4096
PROJ_W = 8192
SM_WI = IDX_HD
SM_CI = SM_WI + IDX_HEADS
SM_CF = SM_CI + MLSTM_HEADS

LOG2E = 1.4426950408889634
NEG_BIG = -1e30
M_INIT = -1e29
F32_MIN = -3.4028234663852886e38
F32_ABOVE_MIN = -3.4028232635611926e38
F32_TINY = 1.1754943508222875e-38
BISECT_COARSE_PASSES = 9
BISECT_FIRST_PASSES = 8
BISECT_PASSES_PER_ROUND = 2
MAX_BISECT_ROUNDS = 160
SOFTMAX_BOUND_MAX = 60.0
MLSTM_SEQS_PER_STEP = 4


def _vmem_params(sem, mib):
    return pltpu.CompilerParams(dimension_semantics=sem, vmem_limit_bytes=mib * 1024 * 1024)


def _inproj_kernel(x_ref, g_ref, w_ref, o_ref, sm_ref, h_ref, *, small_tile, small_off):
    j = pl.program_id(1)

    @pl.when(j == 0)
    def _():
        x = x_ref[...]
        ms = jnp.mean(x * x, axis=-1, keepdims=True)
        h_ref[...] = (x * lax.rsqrt(ms + EPS) * g_ref[...]).astype(BF16)

    res = jnp.dot(h_ref[...], w_ref[...], preferred_element_type=F32)
    o_ref[...] = res.astype(BF16)

    @pl.when(j == small_tile)
    def _():
        sm_ref[...] = res[:, small_off:small_off + LANES]


def _inproj(x2, g, w, layer, tm=2048, tn=2048):
    t = x2.shape[0]
    tm = min(tm, t)
    kern = functools.partial(_inproj_kernel, small_tile=COL_SMALL // tn, small_off=COL_SMALL % tn)
    return pl.pallas_call(
        kern,
        grid=(t // tm, PROJ_W // tn),
        in_specs=[
            pl.BlockSpec((tm, D_MODEL), lambda i, j: (i, 0)),
            pl.BlockSpec((None, 1, D_MODEL), lambda i, j: (layer, 0, 0)),
            pl.BlockSpec((None, D_MODEL, tn), lambda i, j: (layer, 0, j)),
        ],
        out_specs=[pl.BlockSpec((tm, tn), lambda i, j: (i, j)), pl.BlockSpec((tm, LANES), lambda i, j: (i, 0))],
        out_shape=[jax.ShapeDtypeStruct((t, PROJ_W), BF16), jax.ShapeDtypeStruct((t, LANES), F32)],
        scratch_shapes=[pltpu.VMEM((tm, D_MODEL), BF16)],
        compiler_params=_vmem_params(("parallel", "arbitrary"), 56),
        name="inproj",
    )(x2, g, w)


def _rope_slab(x, cos, sin_signed, first_half):
    x_hi = pltpu.roll(x, LANES - ATT_HD // 2, axis=1)
    x_lo = pltpu.roll(x, ATT_HD // 2, axis=1)
    rot = jnp.where(first_half, x_hi, x_lo)
    return x * cos + rot * sin_signed


def _head_sumsq(x2, lane, n_heads):
    out = []
    for h in range(n_heads):
        m = (lane >= h * ATT_HD) & (lane < (h + 1) * ATT_HD)
        out.append(jnp.sum(jnp.where(m, x2, 0.0), axis=-1, keepdims=True))
    return out


def _attprep_kernel(q_ref, k_ref, v_ref, qi0_ref, qi1_ref, sm_ref, cos_ref, sin_ref, qn_ref, kn_ref, kin_ref,
                    qo_ref, ko_ref, vo_ref, qio_ref, kio_ref):
    cos = cos_ref[...]
    sin = sin_ref[...]
    rows = cos.shape[0]
    lane = lax.broadcasted_iota(jnp.int32, (rows, LANES), 1)
    first_half = (lane % ATT_HD) < (ATT_HD // 2)
    head0 = lane < ATT_HD

    def norm_rope(ref, g_ref, scale):
        slabs = []
        for s in range(ref.shape[1] // LANES):
            x = ref[:, s * LANES:(s + 1) * LANES].astype(F32)
            ss = _head_sumsq(x * x, lane, 2)
            r0 = lax.rsqrt(ss[0] * (1.0 / ATT_HD) + EPS)
            r1 = lax.rsqrt(ss[1] * (1.0 / ATT_HD) + EPS)
            y = x * jnp.where(head0, r0, r1) * g_ref[...]
            y = _rope_slab(y, cos, sin, first_half)
            if scale != 1.0:
                y = y * scale
            slabs.append(y)
        return slabs

    for s, y in enumerate(norm_rope(q_ref, qn_ref, ATT_HD ** -0.5 * LOG2E)):
        qo_ref[s * LANES:(s + 1) * LANES, :] = y.T.astype(BF16)
    for s, y in enumerate(norm_rope(k_ref, kn_ref, 1.0)):
        ko_ref[2 * s] = y[:, :ATT_HD].astype(BF16)
        ko_ref[2 * s + 1] = y[:, ATT_HD:].astype(BF16)
    vo_ref[...] = v_ref[...].astype(F32).T.astype(BF16)

    for s in range(IDX_HEADS * IDX_HD // LANES):
        qi_ref = (qi0_ref, qi1_ref)[s // 2]
        y = _rope_slab(qi_ref[:, (s % 2) * LANES:(s % 2 + 1) * LANES].astype(F32), cos, sin, first_half)
        qio_ref[s * LANES:(s + 1) * LANES, :] = (y * (IDX_HD ** -0.5)).T.astype(BF16)

    sm = sm_ref[...]
    mu = jnp.sum(jnp.where(head0, sm, 0.0), axis=-1, keepdims=True) * (1.0 / IDX_HD)
    d = sm - mu
    var = jnp.sum(jnp.where(head0, d * d, 0.0), axis=-1, keepdims=True) * (1.0 / IDX_HD)
    y = d * lax.rsqrt(var + EPS) * kin_ref[...]
    y = _rope_slab(y, cos, sin, first_half)
    kio_ref[...] = y[:, :IDX_HD].astype(BF16)


def _attprep(proj, small, cos, sin, qn, kn, kin, layer, seq, tm=512):
    t = proj.shape[0]
    tm = min(tm, seq)
    npos = seq // tm

    def col(width, off):
        return pl.BlockSpec((tm, width), lambda i: (i, off // width))

    vec = pl.BlockSpec((None, 1, LANES), lambda i: (layer, 0, 0))
    tab = pl.BlockSpec((tm, LANES), lambda i: (i % npos, 0))
    return pl.pallas_call(
        _attprep_kernel,
        grid=(t // tm,),
        in_specs=[col(256, COL_Q), col(256, COL_K), col(256, COL_V), col(256, COL_QI), col(256, COL_QI + 256),
                  pl.BlockSpec((tm, LANES), lambda i: (i, 0)), tab, tab, vec, vec, vec],
        out_specs=[pl.BlockSpec((256, tm), lambda i: (0, i)),
                   pl.BlockSpec((ATT_HEADS, tm, ATT_HD), lambda i: (0, i, 0)),
                   pl.BlockSpec((256, tm), lambda i: (0, i)), pl.BlockSpec((512, tm), lambda i: (0, i)),
                   pl.BlockSpec((tm, IDX_HD), lambda i: (i, 0))],
        out_shape=[jax.ShapeDtypeStruct((256, t), BF16), jax.ShapeDtypeStruct((ATT_HEADS, t, ATT_HD), BF16),
                   jax.ShapeDtypeStruct((256, t), BF16), jax.ShapeDtypeStruct((512, t), BF16),
                   jax.ShapeDtypeStruct((t, IDX_HD), BF16)],
        compiler_params=_vmem_params(("parallel",), 32),
        name="attprep",
    )(proj, proj, proj, proj, proj, small, cos, sin, qn, kn, kin)


def _dsa_kernel(mb_ref, qi_ref, sm_ref, q_ref, ki_ref, k_ref, vt_ref, o_ref, sc_ref, acc_ref, s0_ref, s1_ref,
                i0_ref, i1_ref, sc16_ref, *, tq, n_sel, seq, layer):
    i = pl.program_id(1)
    tk1 = min(128, tq)
    tk2 = tq
    tk3 = tq
    n_keys = (i + 1) * tq
    qpos = i * tq + lax.broadcasted_iota(jnp.int32, (1, tq), 1)

    wt = (sm_ref[...] * (IDX_HEADS ** -0.5)).T
    qi_h = [qi_ref[h * IDX_HD:(h + 1) * IDX_HD, :] for h in range(IDX_HEADS)]
    w_h = [wt[SM_WI + h:SM_WI + h + 1, :] for h in range(IDX_HEADS)]
    kidx1 = lax.broadcasted_iota(jnp.int32, (tk1, tq), 0)

    def idx_scores_into(buf_ref, c):
        kc = ki_ref[pl.ds(pl.multiple_of(c * tk1, tk1), tk1), :]
        for h in range(IDX_HEADS):
            buf_ref[h] = jnp.dot(kc, qi_h[h], preferred_element_type=F32)

    def combine(buf_ref, c, amax):
        start = pl.multiple_of(c * tk1, tk1)
        acc = jnp.zeros((tk1, tq), F32)
        for h in range(IDX_HEADS):
            acc = acc + w_h[h] * jnp.maximum(buf_ref[h], 0.0)
        masked = jnp.where(kidx1 + start <= qpos, acc, F32_MIN)
        sc_ref[pl.ds(start, tk1), :] = masked
        sc16_ref[pl.ds(start, tk1), :] = masked.astype(BF16)
        return jnp.maximum(amax, jnp.max(jnp.abs(acc), axis=0, keepdims=True))

    n_chunks1 = n_keys // tk1
    last1 = n_chunks1 - 1

    def score_pair(j, amax):
        c_a = 2 * j
        idx_scores_into(i1_ref, c_a + 1)
        amax = combine(i0_ref, c_a, amax)
        idx_scores_into(i0_ref, jnp.minimum(c_a + 2, last1))
        return combine(i1_ref, c_a + 1, amax)

    idx_scores_into(i0_ref, 0)
    amax = lax.fori_loop(0, n_chunks1 // 2, score_pair, jnp.zeros((1, tq), F32))

    def count(pred_fn):
        def body(c, acc):
            start = pl.multiple_of(c * tk2, tk2)
            x = sc_ref[pl.ds(start, tk2), :]
            return acc + jnp.sum(pred_fn(x).reshape(tk2 // 32, 32, tq), axis=0)
        acc = lax.fori_loop(0, n_keys // tk2, body, jnp.zeros((32, tq), F32))
        return jnp.sum(acc, axis=0, keepdims=True)

    def count16(cand16):
        one, zero = jnp.ones((), BF16), jnp.zeros((), BF16)

        def body(c, acc):
            start = pl.multiple_of(c * tk2, tk2)
            m = jnp.where(sc16_ref[pl.ds(start, tk2), :] >= cand16, one, zero)
            parts = [m[g * 32:(g + 1) * 32, :] for g in range(tk2 // 32)]
            while len(parts) > 1:
                parts = [parts[g] + parts[g + 1] for g in range(0, len(parts), 2)]
            return acc + parts[0].astype(F32)
        acc = lax.fori_loop(0, n_keys // tk2, body, jnp.zeros((32, tq), F32))
        return jnp.sum(acc, axis=0, keepdims=True)

    n_pos = count(lambda x: jnp.where(x > 0.0, 1.0, 0.0))
    n_nn = count(lambda x: jnp.where(x >= 0.0, 1.0, 0.0))
    n_causal = (qpos + 1).astype(F32)
    k_sel = jnp.float32(n_sel)
    take_all = n_causal <= k_sel
    at_zero = (n_pos < k_sel) & (n_nn >= k_sel)
    wide = amax + amax * 2.0 ** -20 + F32_TINY
    positive = n_pos >= k_sel
    lo0 = jnp.where(positive, 0.0, -wide)
    hi0 = jnp.where(positive, wide, 0.0)
    nhi0 = jnp.where(positive, 0.0, n_nn)
    thr0 = jnp.where(take_all, F32_ABOVE_MIN, 0.0)
    tie0 = jnp.where(at_zero & (n_nn > k_sel) & jnp.logical_not(take_all), 1.0, 0.0)
    need0 = k_sel - n_pos
    act0 = jnp.where(take_all | at_zero, 0.0, 1.0)

    def bisect_pass(state):
        lo, hi, n_hi, thr, tie, need, act = state
        cand = 0.5 * lo + 0.5 * hi
        cnt = count(lambda x: jnp.where(x >= cand, 1.0, 0.0))
        live = act > 0.0
        stuck = live & ((cand <= lo) | (cand >= hi))
        hit = live & jnp.logical_not(stuck) & (cnt == k_sel)
        up = live & jnp.logical_not(stuck) & (cnt > k_sel)
        down = live & jnp.logical_not(stuck) & (cnt < k_sel)
        thr = jnp.where(hit, cand, jnp.where(stuck, lo, thr))
        tie = jnp.where(stuck, 1.0, tie)
        need = jnp.where(stuck, k_sel - n_hi, need)
        lo = jnp.where(up, cand, lo)
        hi = jnp.where(down, cand, hi)
        n_hi = jnp.where(down, cnt, n_hi)
        act = jnp.where(hit | stuck, 0.0, act)
        return lo, hi, n_hi, thr, tie, need, act

    def search_cond(carry):
        it, state = carry
        return (it < MAX_BISECT_ROUNDS) & (jnp.max(state[6]) > 0.0)

    def search_body(carry):
        it, state = carry
        for _ in range(BISECT_PASSES_PER_ROUND):
            state = bisect_pass(state)
        return it + 1, state

    def coarse_pass(_, bracket):
        lo, hi = bracket
        c16 = (0.5 * lo + 0.5 * hi).astype(BF16)
        cf = c16.astype(F32)
        cnt = count16(c16)
        below = cf - jnp.abs(cf) * 2.0 ** -7 - F32_TINY
        live = (act0 > 0.0) & (cf < hi)
        hi = jnp.where(live & (cnt < k_sel) & (cf > lo), cf, hi)
        lo = jnp.where(live & (cnt >= k_sel) & (below > lo), below, lo)
        return lo, hi

    lo1, hi1 = lax.fori_loop(0, BISECT_COARSE_PASSES, coarse_pass, (lo0, hi0))
    nhi1 = count(lambda x: jnp.where(x >= hi1, 1.0, 0.0))
    state = lax.fori_loop(0, BISECT_FIRST_PASSES, lambda _, st: bisect_pass(st),
                          (lo1, hi1, nhi1, thr0, tie0, need0, act0))
    _, state = lax.while_loop(search_cond, search_body, (jnp.int32(0), state))
    thr, tie, need = state[3], state[4], state[5]

    tri = (lax.broadcasted_iota(jnp.int32, (tk2, tk2), 0) >= lax.broadcasted_iota(jnp.int32, (tk2, tk2), 1))
    tri = jnp.where(tri, 1.0, 0.0).astype(BF16)

    def drop_chunk(c, run):
        start = pl.multiple_of(c * tk2, tk2)
        x = sc_ref[pl.ds(start, tk2), :]
        eq = jnp.where(tie > 0.0, jnp.where(x == thr, 1.0, 0.0), 0.0)
        rank = run + jnp.dot(tri, eq.astype(BF16), preferred_element_type=F32)
        sc_ref[pl.ds(start, tk2), :] = jnp.where(eq > 0.0, jnp.where(rank > need, F32_MIN, x), x)
        return rank[tk2 - 1:tk2, :]

    @pl.when(jnp.max(tie) > 0.0)
    def _():
        lax.fori_loop(0, n_keys // tk2, drop_chunk, jnp.zeros((1, tq), F32))

    q_h = [q_ref[h * ATT_HD:(h + 1) * ATT_HD, :] for h in range(ATT_HEADS)]
    acc_ref[...] = jnp.zeros(acc_ref.shape, F32)
    bound = mb_ref[layer, 0]

    def scores_into(buf_ref, c):
        start = pl.multiple_of(c * tk3, tk3)
        for h in range(ATT_HEADS):
            buf_ref[h] = jnp.dot(k_ref[h, pl.ds(start, tk3), :], q_h[h], preferred_element_type=F32)

    def attend_bounded(buf_ref, c, live, l_run):
        start = pl.multiple_of(c * tk3, tk3)
        sel = (sc_ref[pl.ds(start, tk3), :] >= thr) & live
        l_out = []
        for h in range(ATT_HEADS):
            p = jnp.where(sel, jnp.exp2(buf_ref[h] - bound), 0.0)
            l_out.append(l_run[h] + jnp.sum(p, axis=0, keepdims=True))
            vt = vt_ref[h * ATT_HD:(h + 1) * ATT_HD, pl.ds(start, tk3)]
            acc_ref[h * ATT_HD:(h + 1) * ATT_HD, :] += jnp.dot(vt, p.astype(BF16), preferred_element_type=F32)
        return tuple(l_out)

    n_chunks3 = n_keys // tk3
    last = n_chunks3 - 1

    def attn_bounded_pair(j, l_run):
        c_a = 2 * j
        c_b = jnp.minimum(c_a + 1, last)
        scores_into(s1_ref, c_b)
        l_run = attend_bounded(s0_ref, c_a, True, l_run)
        scores_into(s0_ref, jnp.minimum(c_a + 2, last))
        return attend_bounded(s1_ref, c_b, c_a + 1 <= last, l_run)

    def attn_online(c, carry):
        m_run, l_run = carry
        start = pl.multiple_of(c * tk3, tk3)
        sel = sc_ref[pl.ds(start, tk3), :] >= thr
        s_all = [jnp.dot(k_ref[h, pl.ds(start, tk3), :], q_h[h], preferred_element_type=F32)
                 for h in range(ATT_HEADS)]
        m_out, l_out, alphas, ps = [], [], [], []
        for h in range(ATT_HEADS):
            s = jnp.where(sel, s_all[h], NEG_BIG)
            m_new = jnp.maximum(m_run[h], jnp.max(s, axis=0, keepdims=True))
            alpha = jnp.exp2(m_run[h] - m_new)
            p = jnp.exp2(s - m_new)
            l_out.append(alpha * l_run[h] + jnp.sum(p, axis=0, keepdims=True))
            m_out.append(m_new)
            alphas.append(alpha)
            ps.append(p.astype(BF16))
        for h in range(ATT_HEADS):
            vt = vt_ref[h * ATT_HD:(h + 1) * ATT_HD, pl.ds(start, tk3)]
            pv = jnp.dot(vt, ps[h], preferred_element_type=F32)
            acc_ref[h * ATT_HD:(h + 1) * ATT_HD, :] = alphas[h] * acc_ref[h * ATT_HD:(h + 1) * ATT_HD, :] + pv
        return tuple(m_out), tuple(l_out)

    zeros = tuple(jnp.zeros((1, tq), F32) for _ in range(ATT_HEADS))

    def run_bounded():
        scores_into(s0_ref, 0)
        return lax.fori_loop(0, (n_chunks3 + 1) // 2, attn_bounded_pair, zeros)

    def run_online():
        init = (tuple(jnp.full((1, tq), M_INIT, F32) for _ in range(ATT_HEADS)), zeros)
        return lax.fori_loop(0, n_keys // tk3, attn_online, init)[1]

    l_fin = lax.cond(bound < SOFTMAX_BOUND_MAX, run_bounded, run_online)
    outs = [acc_ref[h * ATT_HD:(h + 1) * ATT_HD, :] / l_fin[h] for h in range(ATT_HEADS)]
    o_ref[...] = jnp.concatenate(outs, axis=0).T.astype(BF16)


def _dsa(logit_bound, qi, small, q, ki, k, vt, layer, batch, seq, tq=256):
    t = q.shape[1]
    tq = min(tq, seq)
    nq = seq // tq
    n_sel = min(TOPK_MAX, seq // 4)
    assert tq >= n_sel and seq % tq == 0 and tq % 256 == 0
    kern = functools.partial(_dsa_kernel, tq=tq, n_sel=n_sel, seq=seq, layer=layer)
    return pl.pallas_call(
        kern,
        grid=(batch, nq),
        in_specs=[
            pl.BlockSpec(memory_space=pltpu.SMEM),
            pl.BlockSpec((512, tq), lambda b, i: (0, b * nq + i)),
            pl.BlockSpec((tq, LANES), lambda b, i: (b * nq + i, 0)),
            pl.BlockSpec((256, tq), lambda b, i: (0, b * nq + i)),
            pl.BlockSpec((seq, IDX_HD), lambda b, i: (b, 0)),
            pl.BlockSpec((ATT_HEADS, seq, ATT_HD), lambda b, i: (0, b, 0)),
            pl.BlockSpec((256, seq), lambda b, i: (0, b)),
        ],
        out_specs=pl.BlockSpec((tq, 256), lambda b, i: (b * nq + i, 0)),
        out_shape=jax.ShapeDtypeStruct((t, 256), BF16),
        scratch_shapes=[pltpu.VMEM((seq, tq), F32), pltpu.VMEM((ATT_HEADS * ATT_HD, tq), F32),
                        pltpu.VMEM((ATT_HEADS, tq, tq), F32), pltpu.VMEM((ATT_HEADS, tq, tq), F32),
                        pltpu.VMEM((IDX_HEADS, min(128, tq), tq), F32), pltpu.VMEM((IDX_HEADS, min(128, tq), tq), F32),
                        pltpu.VMEM((seq, tq), BF16)],
        compiler_params=_vmem_params(("parallel", "arbitrary"), 48),
        name="dsa",
    )(logit_bound, qi, small, q, ki, k, vt)


def _mlstm_kernel(q_ref, k_ref, v_ref, o_ref, sm_ref, gb_ref, ng_ref, y_ref, c_ref, m_ref, *, nb):
    L = MLSTM_CHUNK
    hd = MLSTM_HD

    @pl.when(pl.program_id(1) == 0)
    def _():
        c_ref[...] = jnp.zeros_like(c_ref)
        m_ref[...] = jnp.zeros_like(m_ref)

    r_i = lax.broadcasted_iota(jnp.int32, (L, L), 0)
    c_i = lax.broadcasted_iota(jnp.int32, (L, L), 1)
    tril_f = (r_i >= c_i).astype(F32)
    causal = r_i <= c_i
    is_f = (c_i >= SM_CF) & (c_i < SM_CF + MLSTM_HEADS)
    ones = jnp.ones((hd, L), BF16)
    chains = [(bb, h) for bb in range(nb) for h in range(MLSTM_HEADS)]
    xc, xr, q_t, v_t, o_t = [], [], [], [], []
    for bb in range(nb):
        g = sm_ref[bb] + gb_ref[...]
        lf = jax.nn.log_sigmoid(g)
        bcum = jnp.dot(tril_f, lf, precision=lax.Precision.HIGHEST, preferred_element_type=F32)
        xc.append(jnp.where(is_f, bcum, g))
        xr.append(xc[bb].T)
        q_t.append(q_ref[bb].astype(F32).T.astype(BF16))
        v_t.append(v_ref[bb].astype(F32).T.astype(BF16))
        o_t.append(o_ref[bb].astype(F32).T)

    qh_t, kh, v_ext_t, qk, gq = {}, {}, {}, {}, {}
    for st, (bb, h) in enumerate(chains):
        qh_t[st] = q_t[bb][h * hd:(h + 1) * hd, :]
        kh[st] = (k_ref[bb, :, h * hd:(h + 1) * hd].astype(F32) * (hd ** -0.5)).astype(BF16)
        v_ext_t[st] = jnp.concatenate([v_t[bb][h * hd:(h + 1) * hd, :], ones], axis=0)
        qk[st] = jnp.dot(kh[st], qh_t[st], preferred_element_type=F32)
        gq[st] = jnp.dot(c_ref[st].astype(BF16), qh_t[st], preferred_element_type=F32)

    a, w_inter, mjs, b_rows, ig_rows, m_sts = {}, {}, {}, {}, {}, {}
    for st, (bb, h) in enumerate(chains):
        b_row = xr[bb][SM_CF + h:SM_CF + h + 1, :]
        ig_row = xr[bb][SM_CI + h:SM_CI + h + 1, :]
        src_col = xc[bb][:, SM_CI + h:SM_CI + h + 1] - xc[bb][:, SM_CF + h:SM_CF + h + 1]
        m_st = m_ref[st][0:1, 0:1]
        dm = jnp.where(causal, b_row + src_col, -jnp.inf)
        inter = b_row + m_st
        mj = jnp.maximum(inter, jnp.max(dm, axis=0, keepdims=True))
        a[st] = (jnp.exp(dm - mj) * qk[st]).astype(BF16)
        w_inter[st] = jnp.exp(inter - mj)
        mjs[st], b_rows[st], ig_rows[st], m_sts[st] = mj, b_row, ig_row, m_st

    av = {st: jnp.dot(v_ext_t[st], a[st], preferred_element_type=F32) for st in a}

    vw, decays = {}, {}
    for bb in range(nb):
        outs = []
        for h in range(MLSTM_HEADS):
            st = bb * MLSTM_HEADS + h
            mj, b_row = mjs[st], b_rows[st]
            r = w_inter[st] * gq[st] + av[st]
            num = r[:hd, :]
            den = r[hd:hd + 1, :]
            hout = num / jnp.maximum(jnp.abs(den), jnp.exp(-mj))
            b_last = b_row[:, L - 1:L]
            m_new = mj[:, L - 1:L]
            wk = jnp.exp(b_last - b_row + ig_rows[st] - m_new)
            decays[st] = jnp.exp(b_last + m_sts[st] - m_new)
            vw[st] = (v_ext_t[st] * wk).astype(BF16)
            m_ref[st] = jnp.broadcast_to(m_new, m_ref.shape[1:])
            ms = jnp.mean(hout * hout, axis=0, keepdims=True)
            hn = hout * lax.rsqrt(ms + EPS) * ng_ref[h * hd:(h + 1) * hd, :]
            outs.append(jax.nn.sigmoid(o_t[bb][h * hd:(h + 1) * hd, :]) * hn)
        y_ref[bb] = jnp.concatenate(outs, axis=0).T.astype(BF16)

    for st in vw:
        c_ref[st] = decays[st] * c_ref[st] + jnp.dot(vw[st], kh[st], preferred_element_type=F32)


def _mlstm(proj, small, gbias, norm_g, layer, batch, seq, nb=MLSTM_SEQS_PER_STEP):
    t = proj.shape[0]
    L = MLSTM_CHUNK
    nc = seq // L
    nb = min(nb, batch)
    assert batch % nb == 0
    proj3 = proj.reshape(batch, seq, PROJ_W)
    small3 = small.reshape(batch, seq, LANES)

    def col(off):
        return pl.BlockSpec((nb, L, 256), lambda b, c: (b, c, off // 256))

    kern = functools.partial(_mlstm_kernel, nb=nb)
    y = pl.pallas_call(
        kern,
        grid=(batch // nb, nc),
        in_specs=[col(COL_CQ), col(COL_CK), col(COL_CV), col(COL_CO),
                  pl.BlockSpec((nb, L, LANES), lambda b, c: (b, c, 0)),
                  pl.BlockSpec((None, 1, LANES), lambda b, c: (layer, 0, 0)),
                  pl.BlockSpec((None, 256, LANES), lambda b, c: (layer, 0, 0))],
        out_specs=pl.BlockSpec((nb, L, 256), lambda b, c: (b, c, 0)),
        out_shape=jax.ShapeDtypeStruct((batch, seq, 256), BF16),
        scratch_shapes=[pltpu.VMEM((nb * MLSTM_HEADS, LANES, MLSTM_HD), F32),
                        pltpu.VMEM((nb * MLSTM_HEADS, SUBLANES, LANES), F32)],
        compiler_params=_vmem_params(("parallel", "arbitrary"), 32),
        name="mlstm",
    )(proj3, proj3, proj3, proj3, small3, gbias, norm_g)
    return y.reshape(t, 256)


def _merge_kernel(x_ref, uv_ref, db_ref, dc_ref, dx_ref, dcp_ref, dxp_ref, g_ref, yb_ref, yc_ref,
                  sn_ref, sw_ref, sb_ref, cw_ref, wb_ref, wo_ref, o_ref, *, tm, tiles_per_seq):
    i = pl.program_id(0)
    L = SGU_CHUNK
    u = jax.nn.gelu(uv_ref[:, :BRANCH_W].astype(F32))
    v = jax.nn.gelu(uv_ref[:, BRANCH_W:].astype(F32))
    ms = jnp.mean(v * v, axis=-1, keepdims=True)
    vn = (v * lax.rsqrt(ms + EPS) * sn_ref[...]).astype(BF16)
    r_i = lax.broadcasted_iota(jnp.int32, (L, L), 0)
    c_i = lax.broadcasted_iota(jnp.int32, (L, L), 1)
    tril = r_i >= c_i
    wm = [jnp.where(tril, sw_ref[g], 0.0).astype(BF16) for g in range(SGU_GROUPS)]
    sb = sb_ref[...]
    chunks = []
    for c in range(tm // L):
        parts = []
        for g in range(SGU_GROUPS):
            vg = vn[c * L:(c + 1) * L, g * SGU_GD:(g + 1) * SGU_GD]
            parts.append(jnp.dot(wm[g], vg, preferred_element_type=F32) + sb[:, g:g + 1])
        chunks.append(jnp.concatenate(parts, axis=1))
    y_a = u * jnp.concatenate(chunks, axis=0)

    z = dc_ref[...].astype(F32) * dx_ref[...].astype(F32)
    zp = dcp_ref[...].astype(F32) * dxp_ref[...].astype(F32)
    zp = jnp.where(i % tiles_per_seq == 0, jnp.zeros_like(zp), zp)
    zz = jnp.concatenate([zp, z], axis=0)
    cw = cw_ref[...]
    conv = cw[0:1, :] * zz[PREV_ROWS - 2:PREV_ROWS - 2 + tm, :] + cw[1:2, :] * zz[PREV_ROWS - 1:PREV_ROWS - 1 + tm, :] \
        + cw[2:3, :] * z
    y_d = db_ref[...].astype(F32) * conv

    ys = (y_a, yb_ref[...], yc_ref[...], y_d)
    merged = jnp.zeros((tm, D_MODEL), F32)
    for n in range(N_BRANCH):
        gate = 0.5 * jnp.tanh(0.5 * g_ref[:, n * D_MODEL:(n + 1) * D_MODEL].astype(F32)) + 0.5
        merged = merged + gate * jnp.dot(ys[n].astype(BF16), wb_ref[n], preferred_element_type=F32)
    o_ref[...] = x_ref[...] + jnp.dot(merged.astype(BF16), wo_ref[...], preferred_element_type=F32)


def _merge(x2, proj, y_b, y_c, sgu_norm, sgu_w, sgu_bt, conv_w, w_branch, w_out, layer, seq, tm=512):
    t = x2.shape[0]
    tm = min(tm, seq)
    tiles_per_seq = seq // tm
    rb = tm // PREV_ROWS

    def col(width, off):
        return pl.BlockSpec((tm, width), lambda i: (i, off // width))

    def prev(off):
        return pl.BlockSpec((PREV_ROWS, 256), lambda i: (jnp.maximum(i * rb - 1, 0), off // 256))

    def full(shape):
        return pl.BlockSpec((None,) + shape, lambda i: (layer,) + (0,) * len(shape))

    kern = functools.partial(_merge_kernel, tm=tm, tiles_per_seq=tiles_per_seq)
    return pl.pallas_call(
        kern,
        grid=(t // tm,),
        in_specs=[pl.BlockSpec((tm, D_MODEL), lambda i: (i, 0)),
                  col(512, COL_AU), col(256, COL_DB), col(256, COL_DC), col(256, COL_DX),
                  prev(COL_DC), prev(COL_DX), col(4096, COL_G),
                  pl.BlockSpec((tm, 256), lambda i: (i, 0)), pl.BlockSpec((tm, 256), lambda i: (i, 0)),
                  full((1, BRANCH_W)),
                  full((SGU_GROUPS, SGU_CHUNK, SGU_CHUNK)), full((SGU_CHUNK, LANES)),
                  full((SUBLANES, BRANCH_W)), full((N_BRANCH, BRANCH_W, D_MODEL)), full((D_MODEL, D_MODEL))],
        out_specs=pl.BlockSpec((tm, D_MODEL), lambda i: (i, 0)),
        out_shape=jax.ShapeDtypeStruct((t, D_MODEL), F32),
        compiler_params=_vmem_params(("parallel",), 48),
        name="merge",
    )(x2, proj, proj, proj, proj, proj, proj, proj, y_b, y_c, sgu_norm, sgu_w, sgu_bt, conv_w, w_branch, w_out)


def _mlp_kernel(x_ref, g_ref, wu_ref, wd_ref, o_ref, h_ref):
    f = pl.program_id(1)

    @pl.when(f == 0)
    def _():
        x = x_ref[...]
        ms = jnp.mean(x * x, axis=-1, keepdims=True)
        h_ref[...] = (x * lax.rsqrt(ms + EPS) * g_ref[...]).astype(BF16)
        o_ref[...] = x

    up = jnp.maximum(jnp.dot(h_ref[...], wu_ref[...], preferred_element_type=F32), 0.0)
    o_ref[...] += jnp.dot((up * up).astype(BF16), wd_ref[...], preferred_element_type=F32)


def _mlp(x2, g, w_up, w_down, layer, tm=1024, tf=2048):
    t = x2.shape[0]
    tm = min(tm, t)
    return pl.pallas_call(
        _mlp_kernel,
        grid=(t // tm, D_FF // tf),
        in_specs=[pl.BlockSpec((tm, D_MODEL), lambda i, f: (i, 0)),
                  pl.BlockSpec((None, 1, D_MODEL), lambda i, f: (layer, 0, 0)),
                  pl.BlockSpec((None, D_MODEL, tf), lambda i, f: (layer, 0, f)),
                  pl.BlockSpec((None, tf, D_MODEL), lambda i, f: (layer, f, 0))],
        out_specs=pl.BlockSpec((tm, D_MODEL), lambda i, f: (i, 0)),
        out_shape=jax.ShapeDtypeStruct((t, D_MODEL), F32),
        scratch_shapes=[pltpu.VMEM((tm, D_MODEL), BF16)],
        compiler_params=_vmem_params(("parallel", "arbitrary"), 48),
        name="mlp",
    )(x2, g, w_up, w_down)


W_IN_SIZES = (256, 256, 256, 256, 256, 512, 64, 8, 256, 256, 256, 256, 4, 4, 256, 256, 256, 4096)
O_KI = sum(W_IN_SIZES[:6])
O_CQ = sum(W_IN_SIZES[:8])
O_CI = sum(W_IN_SIZES[:12])
O_DB = sum(W_IN_SIZES[:14])
O_G = sum(W_IN_SIZES[:17])


def _relayout_kernel(w_ref, o_ref):
    x = w_ref[...]
    rows = x.shape[0]
    small = jnp.concatenate([x[:, O_KI:O_CQ], x[:, O_CI:O_DB],
                             jnp.zeros((rows, LANES - (O_CQ - O_KI) - (O_DB - O_CI)), x.dtype)], axis=1)
    o_ref[:, :COL_CQ] = x[:, :O_KI].astype(BF16)
    o_ref[:, COL_CQ:COL_DB] = x[:, O_CQ:O_CI].astype(BF16)
    o_ref[:, COL_DB:COL_SMALL] = x[:, O_DB:O_G].astype(BF16)
    o_ref[:, COL_SMALL:COL_SMALL + LANES] = small.astype(BF16)
    o_ref[:, COL_SMALL + LANES:COL_G] = jnp.zeros((rows, COL_G - COL_SMALL - LANES), BF16)
    o_ref[:, COL_G:] = x[:, O_G:].astype(BF16)


def _relayout_w_in(w_in, tr=128):
    assert O_KI == COL_CQ and O_CI - O_CQ == COL_DB - COL_CQ and O_G - O_DB == COL_SMALL - COL_DB
    depth, d, in_w = w_in.shape
    return pl.pallas_call(
        _relayout_kernel,
        grid=(depth, d // tr),
        in_specs=[pl.BlockSpec((None, tr, in_w), lambda l, i: (l, i, 0))],
        out_specs=pl.BlockSpec((None, tr, PROJ_W), lambda l, i: (l, i, 0)),
        out_shape=jax.ShapeDtypeStruct((depth, d, PROJ_W), BF16),
        compiler_params=_vmem_params(("parallel", "parallel"), 32),
        name="relayout",
    )(w_in)


def _rope_tables(seq):
    half = ATT_HD // 2
    inv = jnp.float32(ROPE_THETA) ** (-jnp.arange(half, dtype=jnp.float32) * 2.0 / ATT_HD)
    ang = jnp.arange(seq, dtype=jnp.int32).astype(jnp.float32)[:, None] * inv[None, :]
    cos = jnp.cos(ang)
    sin = jnp.sin(ang)
    cos_t = jnp.concatenate([cos, cos, cos, cos], axis=1)
    sin_t = jnp.concatenate([-sin, sin, -sin, sin], axis=1)
    return cos_t, sin_t


def _forward(x, ln_mix, w_in, sgu_norm, sgu_w, sgu_b, q_norm, k_norm, kidx_norm, mlstm_i_bias, mlstm_f_bias,
             mlstm_norm, conv_w, w_branch, w_out, ln_mlp, w_up, w_down):
    batch, seq, d = x.shape
    depth = w_in.shape[0]
    x2 = x.reshape(batch * seq, d)
    cos_t, sin_t = _rope_tables(seq)
    w_in_r = _relayout_w_in(w_in)
    w_branch_b, w_out_b, w_up_b, w_down_b = (w.astype(BF16) for w in (w_branch, w_out, w_up, w_down))
    qn2, kn2, kin2 = (jnp.tile(g, (1, 2))[:, None, :] for g in (q_norm, k_norm, kidx_norm))
    logit_bound = (ATT_HD ** 0.5 * LOG2E * jnp.max(jnp.abs(q_norm), axis=1, keepdims=True)
                   * jnp.max(jnp.abs(k_norm), axis=1, keepdims=True))
    gbias = jnp.concatenate([jnp.zeros((depth, SM_CI), F32), mlstm_i_bias, mlstm_f_bias,
                             jnp.zeros((depth, LANES - SM_CF - MLSTM_HEADS), F32)], axis=1)[:, None, :]
    mnorm = jnp.broadcast_to(mlstm_norm[:, :, None], (depth, MLSTM_HEADS * MLSTM_HD, LANES))
    sgu_bt = jnp.pad(jnp.swapaxes(sgu_b, 1, 2), ((0, 0), (0, 0), (0, LANES - SGU_GROUPS)))
    conv_p = jnp.pad(conv_w, ((0, 0), (0, SUBLANES - CONV_WIDTH), (0, 0)))
    for l in range(depth):
        proj, small = _inproj(x2, ln_mix[:, None, :], w_in_r, l)
        q, k, v, qi, ki = _attprep(proj, small, cos_t, sin_t, qn2, kn2, kin2, l, seq)
        y_b = _dsa(logit_bound, qi, small, q, ki, k, v, l, batch, seq)
        y_c = _mlstm(proj, small, gbias, mnorm, l, batch, seq)
        x2 = _merge(x2, proj, y_b, y_c, sgu_norm[:, None, :], sgu_w, sgu_bt, conv_p, w_branch_b, w_out_b, l, seq)
        x2 = _mlp(x2, ln_mlp[:, None, :], w_up_b, w_down_b, l)
    return x2.reshape(batch, seq, d)


def kernel(x, ln_mix, w_in, sgu_norm, sgu_w, sgu_b, q_norm, k_norm, kidx_norm, mlstm_i_bias, mlstm_f_bias,
           mlstm_norm, conv_w, w_branch, w_out, ln_mlp, w_up, w_down):
    return _forward(x, ln_mix, w_in, sgu_norm, sgu_w, sgu_b, q_norm, k_norm, kidx_norm, mlstm_i_bias,
                    mlstm_f_bias, mlstm_norm, conv_w, w_branch, w_out, ln_mlp, w_up, w_down)
```

```python
import functools

import jax
import jax.numpy as jnp
from jax import lax
from jax.experimental import pallas as pl
from jax.experimental.pallas import tpu as pltpu

F32 = jnp.float32
BF16 = jnp.bfloat16

D_MODEL = 1024
N_BRANCH = 4
BRANCH_W = 256
SGU_GROUPS = 4
SGU_GD = BRANCH_W // SGU_GROUPS
SGU_CHUNK = 128
ATT_HEADS = 4
ATT_HD = 64
IDX_HEADS = 8
IDX_HD = 64
TOPK_MAX = 256
ROPE_THETA = 10000.0
MLSTM_HEADS = 4
MLSTM_HD = 64
MLSTM_CHUNK = 128
CONV_WIDTH = 3
D_FF = 4 * D_MODEL
EPS = 1e-6

LANES = 128
SUBLANES = 8
PREV_ROWS = 16

COL_AU = 0
COL_AV = 256
COL_Q = 512
COL_K = 768
COL_V = 1024
COL_QI = 1280
COL_CQ = 1792
COL_CK = 2048
COL_CV = 2304
COL_CO = 2560
COL_DB = 2816
COL_DC = 3072
COL_DX = 3328
COL_SMALL = 3584
PROJ_W = 4096
GATE_W = N_BRANCH * D_MODEL
SM_WI = IDX_HD
SM_CI = SM_WI + IDX_HEADS
SM_CF = SM_CI + MLSTM_HEADS

LOG2E = 1.4426950408889634
NEG_BIG = -1e30
M_INIT = -1e29
F32_MIN = -3.4028234663852886e38
F32_ABOVE_MIN = -3.4028232635611926e38
F32_TINY = 1.1754943508222875e-38
BISECT_COARSE_PASSES = 9
BISECT_FIRST_PASSES = 8
BISECT_PASSES_PER_ROUND = 2
MAX_BISECT_ROUNDS = 160
SOFTMAX_BOUND_MAX = 60.0
MLSTM_SEQS_PER_STEP = 4


def _vmem_params(sem, mib):
    return pltpu.CompilerParams(dimension_semantics=sem, vmem_limit_bytes=mib * 1024 * 1024)


def _inproj_kernel(x_ref, g_ref, w_ref, o_ref, sm_ref, h_ref, *, small_tile, small_off):
    j = pl.program_id(1)

    @pl.when(j == 0)
    def _():
        x = x_ref[...]
        ms = jnp.mean(x * x, axis=-1, keepdims=True)
        h_ref[...] = (x * lax.rsqrt(ms + EPS) * g_ref[...]).astype(BF16)

    res = jnp.dot(h_ref[...], w_ref[...], preferred_element_type=F32)
    o_ref[...] = res.astype(BF16)

    @pl.when(j == small_tile)
    def _():
        sm_ref[...] = res[:, small_off:small_off + LANES]


def _inproj(x2, g, w, layer, tm=2048, tn=2048):
    t = x2.shape[0]
    tm = min(tm, t)
    kern = functools.partial(_inproj_kernel, small_tile=COL_SMALL // tn, small_off=COL_SMALL % tn)
    return pl.pallas_call(
        kern,
        grid=(t // tm, PROJ_W // tn),
        in_specs=[
            pl.BlockSpec((tm, D_MODEL), lambda i, j: (i, 0)),
            pl.BlockSpec((None, 1, D_MODEL), lambda i, j: (layer, 0, 0)),
            pl.BlockSpec((None, D_MODEL, tn), lambda i, j: (layer, 0, j)),
        ],
        out_specs=[pl.BlockSpec((tm, tn), lambda i, j: (i, j)), pl.BlockSpec((tm, LANES), lambda i, j: (i, 0))],
        out_shape=[jax.ShapeDtypeStruct((t, PROJ_W), BF16), jax.ShapeDtypeStruct((t, LANES), F32)],
        scratch_shapes=[pltpu.VMEM((tm, D_MODEL), BF16)],
        compiler_params=_vmem_params(("parallel", "arbitrary"), 56),
        name="inproj",
    )(x2, g, w)


def _rope_slab(x, cos, sin_signed, first_half):
    x_hi = pltpu.roll(x, LANES - ATT_HD // 2, axis=1)
    x_lo = pltpu.roll(x, ATT_HD // 2, axis=1)
    rot = jnp.where(first_half, x_hi, x_lo)
    return x * cos + rot * sin_signed


def _head_sumsq(x2, lane, n_heads):
    out = []
    for h in range(n_heads):
        m = (lane >= h * ATT_HD) & (lane < (h + 1) * ATT_HD)
        out.append(jnp.sum(jnp.where(m, x2, 0.0), axis=-1, keepdims=True))
    return out


def _attprep_kernel(q_ref, k_ref, v_ref, qi0_ref, qi1_ref, sm_ref, cos_ref, sin_ref, qn_ref, kn_ref, kin_ref,
                    qo_ref, ko_ref, vo_ref, qio_ref, kio_ref):
    cos = cos_ref[...]
    sin = sin_ref[...]
    rows = cos.shape[0]
    lane = lax.broadcasted_iota(jnp.int32, (rows, LANES), 1)
    first_half = (lane % ATT_HD) < (ATT_HD // 2)
    head0 = lane < ATT_HD

    def norm_rope(ref, g_ref, scale):
        slabs = []
        for s in range(ref.shape[1] // LANES):
            x = ref[:, s * LANES:(s + 1) * LANES].astype(F32)
            ss = _head_sumsq(x * x, lane, 2)
            r0 = lax.rsqrt(ss[0] * (1.0 / ATT_HD) + EPS)
            r1 = lax.rsqrt(ss[1] * (1.0 / ATT_HD) + EPS)
            y = x * jnp.where(head0, r0, r1) * g_ref[...]
            y = _rope_slab(y, cos, sin, first_half)
            if scale != 1.0:
                y = y * scale
            slabs.append(y)
        return slabs

    for s, y in enumerate(norm_rope(q_ref, qn_ref, ATT_HD ** -0.5 * LOG2E)):
        qo_ref[s * LANES:(s + 1) * LANES, :] = y.T.astype(BF16)
    for s, y in enumerate(norm_rope(k_ref, kn_ref, 1.0)):
        ko_ref[2 * s] = y[:, :ATT_HD].astype(BF16)
        ko_ref[2 * s + 1] = y[:, ATT_HD:].astype(BF16)
    vo_ref[...] = v_ref[...].astype(F32).T.astype(BF16)

    for s in range(IDX_HEADS * IDX_HD // LANES):
        qi_ref = (qi0_ref, qi1_ref)[s // 2]
        y = _rope_slab(qi_ref[:, (s % 2) * LANES:(s % 2 + 1) * LANES].astype(F32), cos, sin, first_half)
        qio_ref[s * LANES:(s + 1) * LANES, :] = (y * (IDX_HD ** -0.5)).T.astype(BF16)

    sm = sm_ref[...]
    mu = jnp.sum(jnp.where(head0, sm, 0.0), axis=-1, keepdims=True) * (1.0 / IDX_HD)
    d = sm - mu
    var = jnp.sum(jnp.where(head0, d * d, 0.0), axis=-1, keepdims=True) * (1.0 / IDX_HD)
    y = d * lax.rsqrt(var + EPS) * kin_ref[...]
    y = _rope_slab(y, cos, sin, first_half)
    kio_ref[...] = y[:, :IDX_HD].astype(BF16)


def _attprep(proj, small, cos, sin, qn, kn, kin, layer, seq, tm=512):
    t = proj.shape[0]
    tm = min(tm, seq)
    npos = seq // tm

    def col(width, off):
        return pl.BlockSpec((tm, width), lambda i: (i, off // width))

    vec = pl.BlockSpec((None, 1, LANES), lambda i: (layer, 0, 0))
    tab = pl.BlockSpec((tm, LANES), lambda i: (i % npos, 0))
    return pl.pallas_call(
        _attprep_kernel,
        grid=(t // tm,),
        in_specs=[col(256, COL_Q), col(256, COL_K), col(256, COL_V), col(256, COL_QI), col(256, COL_QI + 256),
                  pl.BlockSpec((tm, LANES), lambda i: (i, 0)), tab, tab, vec, vec, vec],
        out_specs=[pl.BlockSpec((256, tm), lambda i: (0, i)),
                   pl.BlockSpec((ATT_HEADS, tm, ATT_HD), lambda i: (0, i, 0)),
                   pl.BlockSpec((256, tm), lambda i: (0, i)), pl.BlockSpec((512, tm), lambda i: (0, i)),
                   pl.BlockSpec((tm, IDX_HD), lambda i: (i, 0))],
        out_shape=[jax.ShapeDtypeStruct((256, t), BF16), jax.ShapeDtypeStruct((ATT_HEADS, t, ATT_HD), BF16),
                   jax.ShapeDtypeStruct((256, t), BF16), jax.ShapeDtypeStruct((512, t), BF16),
                   jax.ShapeDtypeStruct((t, IDX_HD), BF16)],
        compiler_params=_vmem_params(("parallel",), 32),
        name="attprep",
    )(proj, proj, proj, proj, proj, small, cos, sin, qn, kn, kin)


def _dsa_kernel(mb_ref, qi_ref, sm_ref, q_ref, ki_ref, k_ref, vt_ref, o_ref, sc_ref, acc_ref, s0_ref, s1_ref,
                i0_ref, i1_ref, sc16_ref, *, tq, n_sel, seq, layer):
    i = pl.program_id(1)
    tk1 = min(128, tq)
    tk2 = tq
    tk3 = tq
    n_keys = (i + 1) * tq
    qpos = i * tq + lax.broadcasted_iota(jnp.int32, (1, tq), 1)

    wt = (sm_ref[...] * (IDX_HEADS ** -0.5)).T
    qi_h = [qi_ref[h * IDX_HD:(h + 1) * IDX_HD, :] for h in range(IDX_HEADS)]
    w_h = [wt[SM_WI + h:SM_WI + h + 1, :] for h in range(IDX_HEADS)]
    kidx1 = lax.broadcasted_iota(jnp.int32, (tk1, tq), 0)

    def idx_scores_into(buf_ref, c):
        kc = ki_ref[pl.ds(pl.multiple_of(c * tk1, tk1), tk1), :]
        for h in range(IDX_HEADS):
            buf_ref[h] = jnp.dot(kc, qi_h[h], preferred_element_type=F32)

    def combine(buf_ref, c, amax):
        start = pl.multiple_of(c * tk1, tk1)
        acc = jnp.zeros((tk1, tq), F32)
        for h in range(IDX_HEADS):
            acc = acc + w_h[h] * jnp.maximum(buf_ref[h], 0.0)
        masked = jnp.where(kidx1 + start <= qpos, acc, F32_MIN)
        sc_ref[pl.ds(start, tk1), :] = masked
        sc16_ref[pl.ds(start, tk1), :] = masked.astype(BF16)
        return jnp.maximum(amax, jnp.max(jnp.abs(acc), axis=0, keepdims=True))

    n_chunks1 = n_keys // tk1
    last1 = n_chunks1 - 1

    def score_pair(j, amax):
        c_a = 2 * j
        idx_scores_into(i1_ref, c_a + 1)
        amax = combine(i0_ref, c_a, amax)
        idx_scores_into(i0_ref, jnp.minimum(c_a + 2, last1))
        return combine(i1_ref, c_a + 1, amax)

    idx_scores_into(i0_ref, 0)
    amax = lax.fori_loop(0, n_chunks1 // 2, score_pair, jnp.zeros((1, tq), F32))

    def count(pred_fn):
        def body(c, acc):
            start = pl.multiple_of(c * tk2, tk2)
            x = sc_ref[pl.ds(start, tk2), :]
            return acc + jnp.sum(pred_fn(x).reshape(tk2 // 32, 32, tq), axis=0)
        acc = lax.fori_loop(0, n_keys // tk2, body, jnp.zeros((32, tq), F32))
        return jnp.sum(acc, axis=0, keepdims=True)

    def count16(cand16):
        one, zero = jnp.ones((), BF16), jnp.zeros((), BF16)

        def body(c, acc):
            start = pl.multiple_of(c * tk2, tk2)
            m = jnp.where(sc16_ref[pl.ds(start, tk2), :] >= cand16, one, zero)
            parts = [m[g * 32:(g + 1) * 32, :] for g in range(tk2 // 32)]
            while len(parts) > 1:
                parts = [parts[g] + parts[g + 1] for g in range(0, len(parts), 2)]
            return acc + parts[0].astype(F32)
        acc = lax.fori_loop(0, n_keys // tk2, body, jnp.zeros((32, tq), F32))
        return jnp.sum(acc, axis=0, keepdims=True)

    n_pos = count(lambda x: jnp.where(x > 0.0, 1.0, 0.0))
    n_nn = count(lambda x: jnp.where(x >= 0.0, 1.0, 0.0))
    n_causal = (qpos + 1).astype(F32)
    k_sel = jnp.float32(n_sel)
    take_all = n_causal <= k_sel
    at_zero = (n_pos < k_sel) & (n_nn >= k_sel)
    wide = amax + amax * 2.0 ** -20 + F32_TINY
    positive = n_pos >= k_sel
    lo0 = jnp.where(positive, 0.0, -wide)
    hi0 = jnp.where(positive, wide, 0.0)
    nhi0 = jnp.where(positive, 0.0, n_nn)
    thr0 = jnp.where(take_all, F32_ABOVE_MIN, 0.0)
    tie0 = jnp.where(at_zero & (n_nn > k_sel) & jnp.logical_not(take_all), 1.0, 0.0)
    need0 = k_sel - n_pos
    act0 = jnp.where(take_all | at_zero, 0.0, 1.0)

    def bisect_pass(state):
        lo, hi, n_hi, thr, tie, need, act = state
        cand = 0.5 * lo + 0.5 * hi
        cnt = count(lambda x: jnp.where(x >= cand, 1.0, 0.0))
        live = act > 0.0
        stuck = live & ((cand <= lo) | (cand >= hi))
        hit = live & jnp.logical_not(stuck) & (cnt == k_sel)
        up = live & jnp.logical_not(stuck) & (cnt > k_sel)
        down = live & jnp.logical_not(stuck) & (cnt < k_sel)
        thr = jnp.where(hit, cand, jnp.where(stuck, lo, thr))
        tie = jnp.where(stuck, 1.0, tie)
        need = jnp.where(stuck, k_sel - n_hi, need)
        lo = jnp.where(up, cand, lo)
        hi = jnp.where(down, cand, hi)
        n_hi = jnp.where(down, cnt, n_hi)
        act = jnp.where(hit | stuck, 0.0, act)
        return lo, hi, n_hi, thr, tie, need, act

    def search_cond(carry):
        it, state = carry
        return (it < MAX_BISECT_ROUNDS) & (jnp.max(state[6]) > 0.0)

    def search_body(carry):
        it, state = carry
        for _ in range(BISECT_PASSES_PER_ROUND):
            state = bisect_pass(state)
        return it + 1, state

    def coarse_pass(_, bracket):
        lo, hi = bracket
        c16 = (0.5 * lo + 0.5 * hi).astype(BF16)
        cf = c16.astype(F32)
        cnt = count16(c16)
        below = cf - jnp.abs(cf) * 2.0 ** -7 - F32_TINY
        live = (act0 > 0.0) & (cf < hi)
        hi = jnp.where(live & (cnt < k_sel) & (cf > lo), cf, hi)
        lo = jnp.where(live & (cnt >= k_sel) & (below > lo), below, lo)
        return lo, hi

    lo1, hi1 = lax.fori_loop(0, BISECT_COARSE_PASSES, coarse_pass, (lo0, hi0))
    nhi1 = count(lambda x: jnp.where(x >= hi1, 1.0, 0.0))
    state = lax.fori_loop(0, BISECT_FIRST_PASSES, lambda _, st: bisect_pass(st),
                          (lo1, hi1, nhi1, thr0, tie0, need0, act0))
    _, state = lax.while_loop(search_cond, search_body, (jnp.int32(0), state))
    thr, tie, need = state[3], state[4], state[5]

    tri = (lax.broadcasted_iota(jnp.int32, (tk2, tk2), 0) >= lax.broadcasted_iota(jnp.int32, (tk2, tk2), 1))
    tri = jnp.where(tri, 1.0, 0.0).astype(BF16)

    def drop_chunk(c, run):
        start = pl.multiple_of(c * tk2, tk2)
        x = sc_ref[pl.ds(start, tk2), :]
        eq = jnp.where(tie > 0.0, jnp.where(x == thr, 1.0, 0.0), 0.0)
        rank = run + jnp.dot(tri, eq.astype(BF16), preferred_element_type=F32)
        sc_ref[pl.ds(start, tk2), :] = jnp.where(eq > 0.0, jnp.where(rank > need, F32_MIN, x), x)
        return rank[tk2 - 1:tk2, :]

    @pl.when(jnp.max(tie) > 0.0)
    def _():
        lax.fori_loop(0, n_keys // tk2, drop_chunk, jnp.zeros((1, tq), F32))

    q_h = [q_ref[h * ATT_HD:(h + 1) * ATT_HD, :] for h in range(ATT_HEADS)]
    acc_ref[...] = jnp.zeros(acc_ref.shape, F32)
    bound = mb_ref[layer, 0]

    def scores_into(buf_ref, c):
        start = pl.multiple_of(c * tk3, tk3)
        for h in range(ATT_HEADS):
            buf_ref[h] = jnp.dot(k_ref[h, pl.ds(start, tk3), :], q_h[h], preferred_element_type=F32)

    def attend_bounded(buf_ref, c, live, l_run):
        start = pl.multiple_of(c * tk3, tk3)
        sel = (sc_ref[pl.ds(start, tk3), :] >= thr) & live
        l_out = []
        for h in range(ATT_HEADS):
            p = jnp.where(sel, jnp.exp2(buf_ref[h] - bound), 0.0)
            l_out.append(l_run[h] + jnp.sum(p, axis=0, keepdims=True))
            vt = vt_ref[h * ATT_HD:(h + 1) * ATT_HD, pl.ds(start, tk3)]
            acc_ref[h * ATT_HD:(h + 1) * ATT_HD, :] += jnp.dot(vt, p.astype(BF16), preferred_element_type=F32)
        return tuple(l_out)

    n_chunks3 = n_keys // tk3
    last = n_chunks3 - 1

    def attn_bounded_pair(j, l_run):
        c_a = 2 * j
        c_b = jnp.minimum(c_a + 1, last)
        scores_into(s1_ref, c_b)
        l_run = attend_bounded(s0_ref, c_a, True, l_run)
        scores_into(s0_ref, jnp.minimum(c_a + 2, last))
        return attend_bounded(s1_ref, c_b, c_a + 1 <= last, l_run)

    def attn_online(c, carry):
        m_run, l_run = carry
        start = pl.multiple_of(c * tk3, tk3)
        sel = sc_ref[pl.ds(start, tk3), :] >= thr
        s_all = [jnp.dot(k_ref[h, pl.ds(start, tk3), :], q_h[h], preferred_element_type=F32)
                 for h in range(ATT_HEADS)]
        m_out, l_out, alphas, ps = [], [], [], []
        for h in range(ATT_HEADS):
            s = jnp.where(sel, s_all[h], NEG_BIG)
            m_new = jnp.maximum(m_run[h], jnp.max(s, axis=0, keepdims=True))
            alpha = jnp.exp2(m_run[h] - m_new)
            p = jnp.exp2(s - m_new)
            l_out.append(alpha * l_run[h] + jnp.sum(p, axis=0, keepdims=True))
            m_out.append(m_new)
            alphas.append(alpha)
            ps.append(p.astype(BF16))
        for h in range(ATT_HEADS):
            vt = vt_ref[h * ATT_HD:(h + 1) * ATT_HD, pl.ds(start, tk3)]
            pv = jnp.dot(vt, ps[h], preferred_element_type=F32)
            acc_ref[h * ATT_HD:(h + 1) * ATT_HD, :] = alphas[h] * acc_ref[h * ATT_HD:(h + 1) * ATT_HD, :] + pv
        return tuple(m_out), tuple(l_out)

    zeros = tuple(jnp.zeros((1, tq), F32) for _ in range(ATT_HEADS))

    def run_bounded():
        scores_into(s0_ref, 0)
        return lax.fori_loop(0, (n_chunks3 + 1) // 2, attn_bounded_pair, zeros)

    def run_online():
        init = (tuple(jnp.full((1, tq), M_INIT, F32) for _ in range(ATT_HEADS)), zeros)
        return lax.fori_loop(0, n_keys // tk3, attn_online, init)[1]

    l_fin = lax.cond(bound < SOFTMAX_BOUND_MAX, run_bounded, run_online)
    outs = [acc_ref[h * ATT_HD:(h + 1) * ATT_HD, :] / l_fin[h] for h in range(ATT_HEADS)]
    o_ref[...] = jnp.concatenate(outs, axis=0).T.astype(BF16)


def _dsa(logit_bound, qi, small, q, ki, k, vt, layer, batch, seq, tq=256):
    t = q.shape[1]
    tq = min(tq, seq)
    nq = seq // tq
    n_sel = min(TOPK_MAX, seq // 4)
    assert tq >= n_sel and seq % tq == 0 and tq % 256 == 0
    kern = functools.partial(_dsa_kernel, tq=tq, n_sel=n_sel, seq=seq, layer=layer)
    return pl.pallas_call(
        kern,
        grid=(batch, nq),
        in_specs=[
            pl.BlockSpec(memory_space=pltpu.SMEM),
            pl.BlockSpec((512, tq), lambda b, i: (0, b * nq + i)),
            pl.BlockSpec((tq, LANES), lambda b, i: (b * nq + i, 0)),
            pl.BlockSpec((256, tq), lambda b, i: (0, b * nq + i)),
            pl.BlockSpec((seq, IDX_HD), lambda b, i: (b, 0)),
            pl.BlockSpec((ATT_HEADS, seq, ATT_HD), lambda b, i: (0, b, 0)),
            pl.BlockSpec((256, seq), lambda b, i: (0, b)),
        ],
        out_specs=pl.BlockSpec((tq, 256), lambda b, i: (b * nq + i, 0)),
        out_shape=jax.ShapeDtypeStruct((t, 256), BF16),
        scratch_shapes=[pltpu.VMEM((seq, tq), F32), pltpu.VMEM((ATT_HEADS * ATT_HD, tq), F32),
                        pltpu.VMEM((ATT_HEADS, tq, tq), F32), pltpu.VMEM((ATT_HEADS, tq, tq), F32),
                        pltpu.VMEM((IDX_HEADS, min(128, tq), tq), F32), pltpu.VMEM((IDX_HEADS, min(128, tq), tq), F32),
                        pltpu.VMEM((seq, tq), BF16)],
        compiler_params=_vmem_params(("parallel", "arbitrary"), 48),
        name="dsa",
    )(logit_bound, qi, small, q, ki, k, vt)


def _mlstm_kernel(q_ref, k_ref, v_ref, o_ref, sm_ref, gb_ref, ng_ref, y_ref, c_ref, m_ref, *, nb):
    L = MLSTM_CHUNK
    hd = MLSTM_HD

    @pl.when(pl.program_id(1) == 0)
    def _():
        c_ref[...] = jnp.zeros_like(c_ref)
        m_ref[...] = jnp.zeros_like(m_ref)

    r_i = lax.broadcasted_iota(jnp.int32, (L, L), 0)
    c_i = lax.broadcasted_iota(jnp.int32, (L, L), 1)
    tril_f = (r_i >= c_i).astype(F32)
    causal = r_i <= c_i
    is_f = (c_i >= SM_CF) & (c_i < SM_CF + MLSTM_HEADS)
    ones = jnp.ones((hd, L), BF16)
    chains = [(bb, h) for bb in range(nb) for h in range(MLSTM_HEADS)]
    xc, xr, q_t, v_t, o_t = [], [], [], [], []
    for bb in range(nb):
        g = sm_ref[bb] + gb_ref[...]
        lf = jax.nn.log_sigmoid(g)
        bcum = jnp.dot(tril_f, lf, precision=lax.Precision.HIGHEST, preferred_element_type=F32)
        xc.append(jnp.where(is_f, bcum, g))
        xr.append(xc[bb].T)
        q_t.append(q_ref[bb].astype(F32).T.astype(BF16))
        v_t.append(v_ref[bb].astype(F32).T.astype(BF16))
        o_t.append(o_ref[bb].astype(F32).T)

    qh_t, kh, v_ext_t, qk, gq = {}, {}, {}, {}, {}
    for st, (bb, h) in enumerate(chains):
        qh_t[st] = q_t[bb][h * hd:(h + 1) * hd, :]
        kh[st] = (k_ref[bb, :, h * hd:(h + 1) * hd].astype(F32) * (hd ** -0.5)).astype(BF16)
        v_ext_t[st] = jnp.concatenate([v_t[bb][h * hd:(h + 1) * hd, :], ones], axis=0)
        qk[st] = jnp.dot(kh[st], qh_t[st], preferred_element_type=F32)
        gq[st] = jnp.dot(c_ref[st].astype(BF16), qh_t[st], preferred_element_type=F32)

    a, w_inter, mjs, b_rows, ig_rows, m_sts = {}, {}, {}, {}, {}, {}
    for st, (bb, h) in enumerate(chains):
        b_row = xr[bb][SM_CF + h:SM_CF + h + 1, :]
        ig_row = xr[bb][SM_CI + h:SM_CI + h + 1, :]
        src_col = xc[bb][:, SM_CI + h:SM_CI + h + 1] - xc[bb][:, SM_CF + h:SM_CF + h + 1]
        m_st = m_ref[st][0:1, 0:1]
        dm = jnp.where(causal, b_row + src_col, -jnp.inf)
        inter = b_row + m_st
        mj = jnp.maximum(inter, jnp.max(dm, axis=0, keepdims=True))
        a[st] = (jnp.exp(dm - mj) * qk[st]).astype(BF16)
        w_inter[st] = jnp.exp(inter - mj)
        mjs[st], b_rows[st], ig_rows[st], m_sts[st] = mj, b_row, ig_row, m_st

    av = {st: jnp.dot(v_ext_t[st], a[st], preferred_element_type=F32) for st in a}

    vw, decays = {}, {}
    for bb in range(nb):
        outs = []
        for h in range(MLSTM_HEADS):
            st = bb * MLSTM_HEADS + h
            mj, b_row = mjs[st], b_rows[st]
            r = w_inter[st] * gq[st] + av[st]
            num = r[:hd, :]
            den = r[hd:hd + 1, :]
            hout = num / jnp.maximum(jnp.abs(den), jnp.exp(-mj))
            b_last = b_row[:, L - 1:L]
            m_new = mj[:, L - 1:L]
            wk = jnp.exp(b_last - b_row + ig_rows[st] - m_new)
            decays[st] = jnp.exp(b_last + m_sts[st] - m_new)
            vw[st] = (v_ext_t[st] * wk).astype(BF16)
            m_ref[st] = jnp.broadcast_to(m_new, m_ref.shape[1:])
            ms = jnp.mean(hout * hout, axis=0, keepdims=True)
            hn = hout * lax.rsqrt(ms + EPS) * ng_ref[h * hd:(h + 1) * hd, :]
            outs.append(jax.nn.sigmoid(o_t[bb][h * hd:(h + 1) * hd, :]) * hn)
        y_ref[bb] = jnp.concatenate(outs, axis=0).T.astype(BF16)

    for st in vw:
        c_ref[st] = decays[st] * c_ref[st] + jnp.dot(vw[st], kh[st], preferred_element_type=F32)


def _mlstm(proj, small, gbias, norm_g, layer, batch, seq, nb=MLSTM_SEQS_PER_STEP):
    t = proj.shape[0]
    L = MLSTM_CHUNK
    nc = seq // L
    nb = min(nb, batch)
    assert batch % nb == 0
    proj3 = proj.reshape(batch, seq, PROJ_W)
    small3 = small.reshape(batch, seq, LANES)

    def col(off):
        return pl.BlockSpec((nb, L, 256), lambda b, c: (b, c, off // 256))

    kern = functools.partial(_mlstm_kernel, nb=nb)
    y = pl.pallas_call(
        kern,
        grid=(batch // nb, nc),
        in_specs=[col(COL_CQ), col(COL_CK), col(COL_CV), col(COL_CO),
                  pl.BlockSpec((nb, L, LANES), lambda b, c: (b, c, 0)),
                  pl.BlockSpec((None, 1, LANES), lambda b, c: (layer, 0, 0)),
                  pl.BlockSpec((None, 256, LANES), lambda b, c: (layer, 0, 0))],
        out_specs=pl.BlockSpec((nb, L, 256), lambda b, c: (b, c, 0)),
        out_shape=jax.ShapeDtypeStruct((batch, seq, 256), BF16),
        scratch_shapes=[pltpu.VMEM((nb * MLSTM_HEADS, LANES, MLSTM_HD), F32),
                        pltpu.VMEM((nb * MLSTM_HEADS, SUBLANES, LANES), F32)],
        compiler_params=_vmem_params(("parallel", "arbitrary"), 32),
        name="mlstm",
    )(proj3, proj3, proj3, proj3, small3, gbias, norm_g)
    return y.reshape(t, 256)


def _merge_kernel(x_ref, uv_ref, db_ref, dc_ref, dx_ref, dcp_ref, dxp_ref, yb_ref, yc_ref,
                  ln_ref, wg_ref, sn_ref, sw_ref, sb_ref, cw_ref, wb_ref, wo_ref, o_ref, *, tm, tiles_per_seq):
    i = pl.program_id(0)
    L = SGU_CHUNK
    u = jax.nn.gelu(uv_ref[:, :BRANCH_W].astype(F32))
    v = jax.nn.gelu(uv_ref[:, BRANCH_W:].astype(F32))
    ms = jnp.mean(v * v, axis=-1, keepdims=True)
    vn = (v * lax.rsqrt(ms + EPS) * sn_ref[...]).astype(BF16)
    r_i = lax.broadcasted_iota(jnp.int32, (L, L), 0)
    c_i = lax.broadcasted_iota(jnp.int32, (L, L), 1)
    tril = r_i >= c_i
    wm = [jnp.where(tril, sw_ref[g], 0.0).astype(BF16) for g in range(SGU_GROUPS)]
    sb = sb_ref[...]
    chunks = []
    for c in range(tm // L):
        parts = []
        for g in range(SGU_GROUPS):
            vg = vn[c * L:(c + 1) * L, g * SGU_GD:(g + 1) * SGU_GD]
            parts.append(jnp.dot(wm[g], vg, preferred_element_type=F32) + sb[:, g:g + 1])
        chunks.append(jnp.concatenate(parts, axis=1))
    y_a = u * jnp.concatenate(chunks, axis=0)

    z = dc_ref[...].astype(F32) * dx_ref[...].astype(F32)
    zp = dcp_ref[...].astype(F32) * dxp_ref[...].astype(F32)
    zp = jnp.where(i % tiles_per_seq == 0, jnp.zeros_like(zp), zp)
    zz = jnp.concatenate([zp, z], axis=0)
    cw = cw_ref[...]
    conv = cw[0:1, :] * zz[PREV_ROWS - 2:PREV_ROWS - 2 + tm, :] + cw[1:2, :] * zz[PREV_ROWS - 1:PREV_ROWS - 1 + tm, :] \
        + cw[2:3, :] * z
    y_d = db_ref[...].astype(F32) * conv

    xf = x_ref[...]
    h = (xf * lax.rsqrt(jnp.mean(xf * xf, axis=-1, keepdims=True) + EPS) * ln_ref[...]).astype(BF16)
    ys = (y_a, yb_ref[...], yc_ref[...], y_d)
    merged = jnp.zeros((tm, D_MODEL), F32)
    for n in range(N_BRANCH):
        g = jnp.dot(h, wg_ref[:, n * D_MODEL:(n + 1) * D_MODEL], preferred_element_type=F32)
        gate = 0.5 * jnp.tanh(0.5 * g) + 0.5
        merged = merged + gate * jnp.dot(ys[n].astype(BF16), wb_ref[n], preferred_element_type=F32)
    o_ref[...] = xf + jnp.dot(merged.astype(BF16), wo_ref[...], preferred_element_type=F32)


def _merge(x2, proj, y_b, y_c, ln_mix, w_gate, sgu_norm, sgu_w, sgu_bt, conv_w, w_branch, w_out, layer, seq, tm=512):
    t = x2.shape[0]
    tm = min(tm, seq)
    tiles_per_seq = seq // tm
    rb = tm // PREV_ROWS

    def col(width, off):
        return pl.BlockSpec((tm, width), lambda i: (i, off // width))

    def prev(off):
        return pl.BlockSpec((PREV_ROWS, 256), lambda i: (jnp.maximum(i * rb - 1, 0), off // 256))

    def full(shape):
        return pl.BlockSpec((None,) + shape, lambda i: (layer,) + (0,) * len(shape))

    kern = functools.partial(_merge_kernel, tm=tm, tiles_per_seq=tiles_per_seq)
    return pl.pallas_call(
        kern,
        grid=(t // tm,),
        in_specs=[pl.BlockSpec((tm, D_MODEL), lambda i: (i, 0)),
                  col(512, COL_AU), col(256, COL_DB), col(256, COL_DC), col(256, COL_DX),
                  prev(COL_DC), prev(COL_DX),
                  pl.BlockSpec((tm, 256), lambda i: (i, 0)), pl.BlockSpec((tm, 256), lambda i: (i, 0)),
                  full((1, D_MODEL)), full((D_MODEL, GATE_W)),
                  full((1, BRANCH_W)),
                  full((SGU_GROUPS, SGU_CHUNK, SGU_CHUNK)), full((SGU_CHUNK, LANES)),
                  full((SUBLANES, BRANCH_W)), full((N_BRANCH, BRANCH_W, D_MODEL)), full((D_MODEL, D_MODEL))],
        out_specs=pl.BlockSpec((tm, D_MODEL), lambda i: (i, 0)),
        out_shape=jax.ShapeDtypeStruct((t, D_MODEL), F32),
        compiler_params=_vmem_params(("parallel",), 56),
        name="merge",
    )(x2, proj, proj, proj, proj, proj, proj, y_b, y_c, ln_mix, w_gate, sgu_norm, sgu_w, sgu_bt, conv_w, w_branch, w_out)


def _mlp_kernel(x_ref, g_ref, wu_ref, wd_ref, o_ref, h_ref):
    f = pl.program_id(1)

    @pl.when(f == 0)
    def _():
        x = x_ref[...]
        ms = jnp.mean(x * x, axis=-1, keepdims=True)
        h_ref[...] = (x * lax.rsqrt(ms + EPS) * g_ref[...]).astype(BF16)
        o_ref[...] = x

    up = jnp.maximum(jnp.dot(h_ref[...], wu_ref[...], preferred_element_type=F32), 0.0)
    o_ref[...] += jnp.dot((up * up).astype(BF16), wd_ref[...], preferred_element_type=F32)


def _mlp(x2, g, w_up, w_down, layer, tm=1024, tf=2048):
    t = x2.shape[0]
    tm = min(tm, t)
    return pl.pallas_call(
        _mlp_kernel,
        grid=(t // tm, D_FF // tf),
        in_specs=[pl.BlockSpec((tm, D_MODEL), lambda i, f: (i, 0)),
                  pl.BlockSpec((None, 1, D_MODEL), lambda i, f: (layer, 0, 0)),
                  pl.BlockSpec((None, D_MODEL, tf), lambda i, f: (layer, 0, f)),
                  pl.BlockSpec((None, tf, D_MODEL), lambda i, f: (layer, f, 0))],
        out_specs=pl.BlockSpec((tm, D_MODEL), lambda i, f: (i, 0)),
        out_shape=jax.ShapeDtypeStruct((t, D_MODEL), F32),
        scratch_shapes=[pltpu.VMEM((tm, D_MODEL), BF16)],
        compiler_params=_vmem_params(("parallel", "arbitrary"), 48),
        name="mlp",
    )(x2, g, w_up, w_down)


W_IN_SIZES = (256, 256, 256, 256, 256, 512, 64, 8, 256, 256, 256, 256, 4, 4, 256, 256, 256, 4096)
O_KI = sum(W_IN_SIZES[:6])
O_CQ = sum(W_IN_SIZES[:8])
O_CI = sum(W_IN_SIZES[:12])
O_DB = sum(W_IN_SIZES[:14])
O_G = sum(W_IN_SIZES[:17])


def _relayout_kernel(w_ref, o_ref, og_ref):
    x = w_ref[...]
    rows = x.shape[0]
    small = jnp.concatenate([x[:, O_KI:O_CQ], x[:, O_CI:O_DB],
                             jnp.zeros((rows, LANES - (O_CQ - O_KI) - (O_DB - O_CI)), x.dtype)], axis=1)
    o_ref[:, :COL_CQ] = x[:, :O_KI].astype(BF16)
    o_ref[:, COL_CQ:COL_DB] = x[:, O_CQ:O_CI].astype(BF16)
    o_ref[:, COL_DB:COL_SMALL] = x[:, O_DB:O_G].astype(BF16)
    o_ref[:, COL_SMALL:COL_SMALL + LANES] = small.astype(BF16)
    o_ref[:, COL_SMALL + LANES:] = jnp.zeros((rows, PROJ_W - COL_SMALL - LANES), BF16)
    og_ref[...] = x[:, O_G:].astype(BF16)


def _relayout_w_in(w_in, tr=128):
    assert O_KI == COL_CQ and O_CI - O_CQ == COL_DB - COL_CQ and O_G - O_DB == COL_SMALL - COL_DB
    depth, d, in_w = w_in.shape
    return pl.pallas_call(
        _relayout_kernel,
        grid=(depth, d // tr),
        in_specs=[pl.BlockSpec((None, tr, in_w), lambda l, i: (l, i, 0))],
        out_specs=[pl.BlockSpec((None, tr, PROJ_W), lambda l, i: (l, i, 0)),
                   pl.BlockSpec((None, tr, GATE_W), lambda l, i: (l, i, 0))],
        out_shape=[jax.ShapeDtypeStruct((depth, d, PROJ_W), BF16), jax.ShapeDtypeStruct((depth, d, GATE_W), BF16)],
        compiler_params=_vmem_params(("parallel", "parallel"), 32),
        name="relayout",
    )(w_in)


def _rope_tables(seq):
    half = ATT_HD // 2
    inv = jnp.float32(ROPE_THETA) ** (-jnp.arange(half, dtype=jnp.float32) * 2.0 / ATT_HD)
    ang = jnp.arange(seq, dtype=jnp.int32).astype(jnp.float32)[:, None] * inv[None, :]
    cos = jnp.cos(ang)
    sin = jnp.sin(ang)
    cos_t = jnp.concatenate([cos, cos, cos, cos], axis=1)
    sin_t = jnp.concatenate([-sin, sin, -sin, sin], axis=1)
    return cos_t, sin_t


def _forward(x, ln_mix, w_in, sgu_norm, sgu_w, sgu_b, q_norm, k_norm, kidx_norm, mlstm_i_bias, mlstm_f_bias,
             mlstm_norm, conv_w, w_branch, w_out, ln_mlp, w_up, w_down):
    batch, seq, d = x.shape
    depth = w_in.shape[0]
    x2 = x.reshape(batch * seq, d)
    cos_t, sin_t = _rope_tables(seq)
    w_in_r, w_gate = _relayout_w_in(w_in)
    w_branch_b, w_out_b, w_up_b, w_down_b = (w.astype(BF16) for w in (w_branch, w_out, w_up, w_down))
    qn2, kn2, kin2 = (jnp.tile(g, (1, 2))[:, None, :] for g in (q_norm, k_norm, kidx_norm))
    logit_bound = (ATT_HD ** 0.5 * LOG2E * jnp.max(jnp.abs(q_norm), axis=1, keepdims=True)
                   * jnp.max(jnp.abs(k_norm), axis=1, keepdims=True))
    gbias = jnp.concatenate([jnp.zeros((depth, SM_CI), F32), mlstm_i_bias, mlstm_f_bias,
                             jnp.zeros((depth, LANES - SM_CF - MLSTM_HEADS), F32)], axis=1)[:, None, :]
    mnorm = jnp.broadcast_to(mlstm_norm[:, :, None], (depth, MLSTM_HEADS * MLSTM_HD, LANES))
    sgu_bt = jnp.pad(jnp.swapaxes(sgu_b, 1, 2), ((0, 0), (0, 0), (0, LANES - SGU_GROUPS)))
    conv_p = jnp.pad(conv_w, ((0, 0), (0, SUBLANES - CONV_WIDTH), (0, 0)))
    for l in range(depth):
        proj, small = _inproj(x2, ln_mix[:, None, :], w_in_r, l)
        q, k, v, qi, ki = _attprep(proj, small, cos_t, sin_t, qn2, kn2, kin2, l, seq)
        y_b = _dsa(logit_bound, qi, small, q, ki, k, v, l, batch, seq)
        y_c = _mlstm(proj, small, gbias, mnorm, l, batch, seq)
        x2 = _merge(x2, proj, y_b, y_c, ln_mix[:, None, :], w_gate, sgu_norm[:, None, :], sgu_w, sgu_bt, conv_p,
                    w_branch_b, w_out_b, l, seq)
        x2 = _mlp(x2, ln_mlp[:, None, :], w_up_b, w_down_b, l)
    return x2.reshape(batch, seq, d)


def kernel(x, ln_mix, w_in, sgu_norm, sgu_w, sgu_b, q_norm, k_norm, kidx_norm, mlstm_i_bias, mlstm_f_bias,
           mlstm_norm, conv_w, w_branch, w_out, ln_mlp, w_up, w_down):
    return _forward(x, ln_mix, w_in, sgu_norm, sgu_w, sgu_b, q_norm, k_norm, kidx_norm, mlstm_i_bias,
                    mlstm_f_bias, mlstm_norm, conv_w, w_branch, w_out, ln_mlp, w_up, w_down)
```

```python
import functools

import jax
import jax.numpy as jnp
from jax import lax
from jax.experimental import pallas as pl
from jax.experimental.pallas import tpu as pltpu

F32 = jnp.float32
BF16 = jnp.bfloat16

D_MODEL = 1024
N_BRANCH = 4
BRANCH_W = 256
SGU_GROUPS = 4
SGU_GD = BRANCH_W // SGU_GROUPS
SGU_CHUNK = 128
ATT_HEADS = 4
ATT_HD = 64
IDX_HEADS = 8
IDX_HD = 64
TOPK_MAX = 256
ROPE_THETA = 10000.0
MLSTM_HEADS = 4
MLSTM_HD = 64
MLSTM_CHUNK = 128
CONV_WIDTH = 3
D_FF = 4 * D_MODEL
EPS = 1e-6

LANES = 128
SUBLANES = 8
PREV_ROWS = 16

COL_AU = 0
COL_AV = 256
COL_Q = 512
COL_K = 768
COL_V = 1024
COL_QI = 1280
COL_CQ = 1792
COL_CK = 2048
COL_CV = 2304
COL_CO = 2560
COL_DB = 2816
COL_DC = 3072
COL_DX = 3328
COL_SMALL = 3584
PROJ_W = 4096
GATE_W = N_BRANCH * D_MODEL
SM_WI = IDX_HD
SM_CI = SM_WI + IDX_HEADS
SM_CF = SM_CI + MLSTM_HEADS

LOG2E = 1.4426950408889634
NEG_BIG = -1e30
M_INIT = -1e29
F32_MIN = -3.4028234663852886e38
F32_ABOVE_MIN = -3.4028232635611926e38
F32_TINY = 1.1754943508222875e-38
BISECT_COARSE_PASSES = 9
BISECT_FIRST_PASSES = 10
BISECT_PASSES_PER_ROUND = 2
MAX_BISECT_ROUNDS = 160
SOFTMAX_BOUND_MAX = 60.0
MLSTM_SEQS_PER_STEP = 4


def _vmem_params(sem, mib):
    return pltpu.CompilerParams(dimension_semantics=sem, vmem_limit_bytes=mib * 1024 * 1024)


def _inproj_kernel(x_ref, g_ref, w_ref, o_ref, sm_ref, h_ref, *, small_tile, small_off):
    j = pl.program_id(1)

    @pl.when(j == 0)
    def _():
        x = x_ref[...]
        ms = jnp.mean(x * x, axis=-1, keepdims=True)
        h_ref[...] = (x * lax.rsqrt(ms + EPS) * g_ref[...]).astype(BF16)

    res = jnp.dot(h_ref[...], w_ref[...], preferred_element_type=F32)
    o_ref[...] = res.astype(BF16)

    @pl.when(j == small_tile)
    def _():
        sm_ref[...] = res[:, small_off:small_off + LANES]


def _inproj(x2, g, w, layer, tm=2048, tn=2048):
    t = x2.shape[0]
    tm = min(tm, t)
    kern = functools.partial(_inproj_kernel, small_tile=COL_SMALL // tn, small_off=COL_SMALL % tn)
    return pl.pallas_call(
        kern,
        grid=(t // tm, PROJ_W // tn),
        in_specs=[
            pl.BlockSpec((tm, D_MODEL), lambda i, j: (i, 0)),
            pl.BlockSpec((None, 1, D_MODEL), lambda i, j: (layer, 0, 0)),
            pl.BlockSpec((None, D_MODEL, tn), lambda i, j: (layer, 0, j)),
        ],
        out_specs=[pl.BlockSpec((tm, tn), lambda i, j: (i, j)), pl.BlockSpec((tm, LANES), lambda i, j: (i, 0))],
        out_shape=[jax.ShapeDtypeStruct((t, PROJ_W), BF16), jax.ShapeDtypeStruct((t, LANES), F32)],
        scratch_shapes=[pltpu.VMEM((tm, D_MODEL), BF16)],
        compiler_params=_vmem_params(("parallel", "arbitrary"), 56),
        name="inproj",
    )(x2, g, w)


def _rope_slab(x, cos, sin_signed, first_half):
    x_hi = pltpu.roll(x, LANES - ATT_HD // 2, axis=1)
    x_lo = pltpu.roll(x, ATT_HD // 2, axis=1)
    rot = jnp.where(first_half, x_hi, x_lo)
    return x * cos + rot * sin_signed


def _head_sumsq(x2, lane, n_heads):
    out = []
    for h in range(n_heads):
        m = (lane >= h * ATT_HD) & (lane < (h + 1) * ATT_HD)
        out.append(jnp.sum(jnp.where(m, x2, 0.0), axis=-1, keepdims=True))
    return out


def _attprep_kernel(q_ref, k_ref, v_ref, qi0_ref, qi1_ref, sm_ref, cos_ref, sin_ref, qn_ref, kn_ref, kin_ref,
                    qo_ref, ko_ref, vo_ref, qio_ref, kio_ref):
    cos = cos_ref[...]
    sin = sin_ref[...]
    rows = cos.shape[0]
    lane = lax.broadcasted_iota(jnp.int32, (rows, LANES), 1)
    first_half = (lane % ATT_HD) < (ATT_HD // 2)
    head0 = lane < ATT_HD

    def norm_rope(ref, g_ref, scale):
        slabs = []
        for s in range(ref.shape[1] // LANES):
            x = ref[:, s * LANES:(s + 1) * LANES].astype(F32)
            ss = _head_sumsq(x * x, lane, 2)
            r0 = lax.rsqrt(ss[0] * (1.0 / ATT_HD) + EPS)
            r1 = lax.rsqrt(ss[1] * (1.0 / ATT_HD) + EPS)
            y = x * jnp.where(head0, r0, r1) * g_ref[...]
            y = _rope_slab(y, cos, sin, first_half)
            if scale != 1.0:
                y = y * scale
            slabs.append(y)
        return slabs

    for s, y in enumerate(norm_rope(q_ref, qn_ref, ATT_HD ** -0.5 * LOG2E)):
        qo_ref[s * LANES:(s + 1) * LANES, :] = y.T.astype(BF16)
    for s, y in enumerate(norm_rope(k_ref, kn_ref, 1.0)):
        ko_ref[2 * s] = y[:, :ATT_HD].astype(BF16)
        ko_ref[2 * s + 1] = y[:, ATT_HD:].astype(BF16)
    vo_ref[...] = v_ref[...].astype(F32).T.astype(BF16)

    for s in range(IDX_HEADS * IDX_HD // LANES):
        qi_ref = (qi0_ref, qi1_ref)[s // 2]
        y = _rope_slab(qi_ref[:, (s % 2) * LANES:(s % 2 + 1) * LANES].astype(F32), cos, sin, first_half)
        qio_ref[s * LANES:(s + 1) * LANES, :] = (y * (IDX_HD ** -0.5)).T.astype(BF16)

    sm = sm_ref[...]
    mu = jnp.sum(jnp.where(head0, sm, 0.0), axis=-1, keepdims=True) * (1.0 / IDX_HD)
    d = sm - mu
    var = jnp.sum(jnp.where(head0, d * d, 0.0), axis=-1, keepdims=True) * (1.0 / IDX_HD)
    y = d * lax.rsqrt(var + EPS) * kin_ref[...]
    y = _rope_slab(y, cos, sin, first_half)
    kio_ref[...] = y[:, :IDX_HD].astype(BF16)


def _attprep(proj, small, cos, sin, qn, kn, kin, layer, seq, tm=512):
    t = proj.shape[0]
    tm = min(tm, seq)
    npos = seq // tm

    def col(width, off):
        return pl.BlockSpec((tm, width), lambda i: (i, off // width))

    vec = pl.BlockSpec((None, 1, LANES), lambda i: (layer, 0, 0))
    tab = pl.BlockSpec((tm, LANES), lambda i: (i % npos, 0))
    return pl.pallas_call(
        _attprep_kernel,
        grid=(t // tm,),
        in_specs=[col(256, COL_Q), col(256, COL_K), col(256, COL_V), col(256, COL_QI), col(256, COL_QI + 256),
                  pl.BlockSpec((tm, LANES), lambda i: (i, 0)), tab, tab, vec, vec, vec],
        out_specs=[pl.BlockSpec((256, tm), lambda i: (0, i)),
                   pl.BlockSpec((ATT_HEADS, tm, ATT_HD), lambda i: (0, i, 0)),
                   pl.BlockSpec((256, tm), lambda i: (0, i)), pl.BlockSpec((512, tm), lambda i: (0, i)),
                   pl.BlockSpec((tm, IDX_HD), lambda i: (i, 0))],
        out_shape=[jax.ShapeDtypeStruct((256, t), BF16), jax.ShapeDtypeStruct((ATT_HEADS, t, ATT_HD), BF16),
                   jax.ShapeDtypeStruct((256, t), BF16), jax.ShapeDtypeStruct((512, t), BF16),
                   jax.ShapeDtypeStruct((t, IDX_HD), BF16)],
        compiler_params=_vmem_params(("parallel",), 32),
        name="attprep",
    )(proj, proj, proj, proj, proj, small, cos, sin, qn, kn, kin)


def _dsa_kernel(mb_ref, qi_ref, sm_ref, q_ref, ki_ref, k_ref, vt_ref, o_ref, sc_ref, acc_ref, s0_ref, s1_ref,
                i0_ref, i1_ref, sc16_ref, *, tq, n_sel, seq, layer):
    i = pl.program_id(1)
    tk1 = min(128, tq)
    tk2 = tq
    tk3 = tq
    n_keys = (i + 1) * tq
    qpos = i * tq + lax.broadcasted_iota(jnp.int32, (1, tq), 1)

    wt = (sm_ref[...] * (IDX_HEADS ** -0.5)).T
    qi_h = [qi_ref[h * IDX_HD:(h + 1) * IDX_HD, :] for h in range(IDX_HEADS)]
    w_h = [wt[SM_WI + h:SM_WI + h + 1, :] for h in range(IDX_HEADS)]
    kidx1 = lax.broadcasted_iota(jnp.int32, (tk1, tq), 0)

    def idx_scores_into(buf_ref, c):
        kc = ki_ref[pl.ds(pl.multiple_of(c * tk1, tk1), tk1), :]
        for h in range(IDX_HEADS):
            buf_ref[h] = jnp.dot(kc, qi_h[h], preferred_element_type=F32)

    def combine(buf_ref, c, amax):
        start = pl.multiple_of(c * tk1, tk1)
        acc = jnp.zeros((tk1, tq), F32)
        for h in range(IDX_HEADS):
            acc = acc + w_h[h] * jnp.maximum(buf_ref[h], 0.0)
        masked = jnp.where(kidx1 + start <= qpos, acc, F32_MIN)
        sc_ref[pl.ds(start, tk1), :] = masked
        sc16_ref[pl.ds(start, tk1), :] = masked.astype(BF16)
        return jnp.maximum(amax, jnp.max(jnp.abs(acc), axis=0, keepdims=True))

    n_chunks1 = n_keys // tk1
    last1 = n_chunks1 - 1

    def score_pair(j, amax):
        c_a = 2 * j
        idx_scores_into(i1_ref, c_a + 1)
        amax = combine(i0_ref, c_a, amax)
        idx_scores_into(i0_ref, jnp.minimum(c_a + 2, last1))
        return combine(i1_ref, c_a + 1, amax)

    idx_scores_into(i0_ref, 0)
    amax = lax.fori_loop(0, n_chunks1 // 2, score_pair, jnp.zeros((1, tq), F32))

    def count(pred_fn):
        def body(c, acc):
            start = pl.multiple_of(c * tk2, tk2)
            x = sc_ref[pl.ds(start, tk2), :]
            return acc + jnp.sum(pred_fn(x).reshape(tk2 // 32, 32, tq), axis=0)
        acc = lax.fori_loop(0, n_keys // tk2, body, jnp.zeros((32, tq), F32))
        return jnp.sum(acc, axis=0, keepdims=True)

    def count16(cand16):
        one, zero = jnp.ones((), BF16), jnp.zeros((), BF16)

        def body(c, acc):
            start = pl.multiple_of(c * tk2, tk2)
            m = jnp.where(sc16_ref[pl.ds(start, tk2), :] >= cand16, one, zero)
            parts = [m[g * 32:(g + 1) * 32, :] for g in range(tk2 // 32)]
            while len(parts) > 1:
                parts = [parts[g] + parts[g + 1] for g in range(0, len(parts), 2)]
            return acc + parts[0].astype(F32)
        acc = lax.fori_loop(0, n_keys // tk2, body, jnp.zeros((32, tq), F32))
        return jnp.sum(acc, axis=0, keepdims=True)

    n_pos = count(lambda x: jnp.where(x > 0.0, 1.0, 0.0))
    n_nn = count(lambda x: jnp.where(x >= 0.0, 1.0, 0.0))
    n_causal = (qpos + 1).astype(F32)
    k_sel = jnp.float32(n_sel)
    take_all = n_causal <= k_sel
    at_zero = (n_pos < k_sel) & (n_nn >= k_sel)
    wide = amax + amax * 2.0 ** -20 + F32_TINY
    positive = n_pos >= k_sel
    lo0 = jnp.where(positive, 0.0, -wide)
    hi0 = jnp.where(positive, wide, 0.0)
    nhi0 = jnp.where(positive, 0.0, n_nn)
    thr0 = jnp.where(take_all, F32_ABOVE_MIN, 0.0)
    tie0 = jnp.where(at_zero & (n_nn > k_sel) & jnp.logical_not(take_all), 1.0, 0.0)
    need0 = k_sel - n_pos
    act0 = jnp.where(take_all | at_zero, 0.0, 1.0)

    def bisect_pass(state):
        lo, hi, n_hi, thr, tie, need, act = state
        cand = 0.5 * lo + 0.5 * hi
        cnt = count(lambda x: jnp.where(x >= cand, 1.0, 0.0))
        live = act > 0.0
        stuck = live & ((cand <= lo) | (cand >= hi))
        hit = live & jnp.logical_not(stuck) & (cnt == k_sel)
        up = live & jnp.logical_not(stuck) & (cnt > k_sel)
        down = live & jnp.logical_not(stuck) & (cnt < k_sel)
        thr = jnp.where(hit, cand, jnp.where(stuck, lo, thr))
        tie = jnp.where(stuck, 1.0, tie)
        need = jnp.where(stuck, k_sel - n_hi, need)
        lo = jnp.where(up, cand, lo)
        hi = jnp.where(down, cand, hi)
        n_hi = jnp.where(down, cnt, n_hi)
        act = jnp.where(hit | stuck, 0.0, act)
        return lo, hi, n_hi, thr, tie, need, act

    def search_cond(carry):
        it, state = carry
        return (it < MAX_BISECT_ROUNDS) & (jnp.max(state[6]) > 0.0)

    def search_body(carry):
        it, state = carry
        for _ in range(BISECT_PASSES_PER_ROUND):
            state = bisect_pass(state)
        return it + 1, state

    def coarse_pass(_, bracket):
        lo, hi = bracket
        c16 = (0.5 * lo + 0.5 * hi).astype(BF16)
        cf = c16.astype(F32)
        cnt = count16(c16)
        below = cf - jnp.abs(cf) * 2.0 ** -7 - F32_TINY
        live = (act0 > 0.0) & (cf < hi)
        hi = jnp.where(live & (cnt < k_sel) & (cf > lo), cf, hi)
        lo = jnp.where(live & (cnt >= k_sel) & (below > lo), below, lo)
        return lo, hi

    lo1, hi1 = lax.fori_loop(0, BISECT_COARSE_PASSES, coarse_pass, (lo0, hi0))
    nhi1 = count(lambda x: jnp.where(x >= hi1, 1.0, 0.0))
    state = lax.fori_loop(0, BISECT_FIRST_PASSES, lambda _, st: bisect_pass(st),
                          (lo1, hi1, nhi1, thr0, tie0, need0, act0))
    _, state = lax.while_loop(search_cond, search_body, (jnp.int32(0), state))
    thr, tie, need = state[3], state[4], state[5]

    tri = (lax.broadcasted_iota(jnp.int32, (tk2, tk2), 0) >= lax.broadcasted_iota(jnp.int32, (tk2, tk2), 1))
    tri = jnp.where(tri, 1.0, 0.0).astype(BF16)

    def drop_chunk(c, run):
        start = pl.multiple_of(c * tk2, tk2)
        x = sc_ref[pl.ds(start, tk2), :]
        eq = jnp.where(tie > 0.0, jnp.where(x == thr, 1.0, 0.0), 0.0)
        rank = run + jnp.dot(tri, eq.astype(BF16), preferred_element_type=F32)
        sc_ref[pl.ds(start, tk2), :] = jnp.where(eq > 0.0, jnp.where(rank > need, F32_MIN, x), x)
        return rank[tk2 - 1:tk2, :]

    @pl.when(jnp.max(tie) > 0.0)
    def _():
        lax.fori_loop(0, n_keys // tk2, drop_chunk, jnp.zeros((1, tq), F32))

    q_h = [q_ref[h * ATT_HD:(h + 1) * ATT_HD, :] for h in range(ATT_HEADS)]
    acc_ref[...] = jnp.zeros(acc_ref.shape, F32)
    bound = mb_ref[layer, 0]

    def scores_into(buf_ref, c):
        start = pl.multiple_of(c * tk3, tk3)
        for h in range(ATT_HEADS):
            buf_ref[h] = jnp.dot(k_ref[h, pl.ds(start, tk3), :], q_h[h], preferred_element_type=F32)

    def attend_bounded(buf_ref, c, live, l_run):
        start = pl.multiple_of(c * tk3, tk3)
        sel = (sc_ref[pl.ds(start, tk3), :] >= thr) & live
        l_out = []
        for h in range(ATT_HEADS):
            p = jnp.where(sel, jnp.exp2(buf_ref[h] - bound), 0.0)
            l_out.append(l_run[h] + jnp.sum(p, axis=0, keepdims=True))
            vt = vt_ref[h * ATT_HD:(h + 1) * ATT_HD, pl.ds(start, tk3)]
            acc_ref[h * ATT_HD:(h + 1) * ATT_HD, :] += jnp.dot(vt, p.astype(BF16), preferred_element_type=F32)
        return tuple(l_out)

    n_chunks3 = n_keys // tk3
    last = n_chunks3 - 1

    def attn_bounded_pair(j, l_run):
        c_a = 2 * j
        c_b = jnp.minimum(c_a + 1, last)
        scores_into(s1_ref, c_b)
        l_run = attend_bounded(s0_ref, c_a, True, l_run)
        scores_into(s0_ref, jnp.minimum(c_a + 2, last))
        return attend_bounded(s1_ref, c_b, c_a + 1 <= last, l_run)

    def attn_online(c, carry):
        m_run, l_run = carry
        start = pl.multiple_of(c * tk3, tk3)
        sel = sc_ref[pl.ds(start, tk3), :] >= thr
        s_all = [jnp.dot(k_ref[h, pl.ds(start, tk3), :], q_h[h], preferred_element_type=F32)
                 for h in range(ATT_HEADS)]
        m_out, l_out, alphas, ps = [], [], [], []
        for h in range(ATT_HEADS):
            s = jnp.where(sel, s_all[h], NEG_BIG)
            m_new = jnp.maximum(m_run[h], jnp.max(s, axis=0, keepdims=True))
            alpha = jnp.exp2(m_run[h] - m_new)
            p = jnp.exp2(s - m_new)
            l_out.append(alpha * l_run[h] + jnp.sum(p, axis=0, keepdims=True))
            m_out.append(m_new)
            alphas.append(alpha)
            ps.append(p.astype(BF16))
        for h in range(ATT_HEADS):
            vt = vt_ref[h * ATT_HD:(h + 1) * ATT_HD, pl.ds(start, tk3)]
            pv = jnp.dot(vt, ps[h], preferred_element_type=F32)
            acc_ref[h * ATT_HD:(h + 1) * ATT_HD, :] = alphas[h] * acc_ref[h * ATT_HD:(h + 1) * ATT_HD, :] + pv
        return tuple(m_out), tuple(l_out)

    zeros = tuple(jnp.zeros((1, tq), F32) for _ in range(ATT_HEADS))

    def run_bounded():
        scores_into(s0_ref, 0)
        return lax.fori_loop(0, (n_chunks3 + 1) // 2, attn_bounded_pair, zeros)

    def run_online():
        init = (tuple(jnp.full((1, tq), M_INIT, F32) for _ in range(ATT_HEADS)), zeros)
        return lax.fori_loop(0, n_keys // tk3, attn_online, init)[1]

    l_fin = lax.cond(bound < SOFTMAX_BOUND_MAX, run_bounded, run_online)
    outs = [acc_ref[h * ATT_HD:(h + 1) * ATT_HD, :] / l_fin[h] for h in range(ATT_HEADS)]
    o_ref[...] = jnp.concatenate(outs, axis=0).T.astype(BF16)


def _dsa(logit_bound, qi, small, q, ki, k, vt, layer, batch, seq, tq=256):
    t = q.shape[1]
    tq = min(tq, seq)
    nq = seq // tq
    n_sel = min(TOPK_MAX, seq // 4)
    assert tq >= n_sel and seq % tq == 0 and tq % 256 == 0
    kern = functools.partial(_dsa_kernel, tq=tq, n_sel=n_sel, seq=seq, layer=layer)
    return pl.pallas_call(
        kern,
        grid=(batch, nq),
        in_specs=[
            pl.BlockSpec(memory_space=pltpu.SMEM),
            pl.BlockSpec((512, tq), lambda b, i: (0, b * nq + i)),
            pl.BlockSpec((tq, LANES), lambda b, i: (b * nq + i, 0)),
            pl.BlockSpec((256, tq), lambda b, i: (0, b * nq + i)),
            pl.BlockSpec((seq, IDX_HD), lambda b, i: (b, 0)),
            pl.BlockSpec((ATT_HEADS, seq, ATT_HD), lambda b, i: (0, b, 0)),
            pl.BlockSpec((256, seq), lambda b, i: (0, b)),
        ],
        out_specs=pl.BlockSpec((tq, 256), lambda b, i: (b * nq + i, 0)),
        out_shape=jax.ShapeDtypeStruct((t, 256), BF16),
        scratch_shapes=[pltpu.VMEM((seq, tq), F32), pltpu.VMEM((ATT_HEADS * ATT_HD, tq), F32),
                        pltpu.VMEM((ATT_HEADS, tq, tq), F32), pltpu.VMEM((ATT_HEADS, tq, tq), F32),
                        pltpu.VMEM((IDX_HEADS, min(128, tq), tq), F32), pltpu.VMEM((IDX_HEADS, min(128, tq), tq), F32),
                        pltpu.VMEM((seq, tq), BF16)],
        compiler_params=_vmem_params(("parallel", "arbitrary"), 48),
        name="dsa",
    )(logit_bound, qi, small, q, ki, k, vt)


def _mlstm_kernel(q_ref, k_ref, v_ref, o_ref, sm_ref, gb_ref, ng_ref, y_ref, c_ref, m_ref, *, nb):
    L = MLSTM_CHUNK
    hd = MLSTM_HD

    @pl.when(pl.program_id(1) == 0)
    def _():
        c_ref[...] = jnp.zeros_like(c_ref)
        m_ref[...] = jnp.zeros_like(m_ref)

    r_i = lax.broadcasted_iota(jnp.int32, (L, L), 0)
    c_i = lax.broadcasted_iota(jnp.int32, (L, L), 1)
    tril_f = (r_i >= c_i).astype(F32)
    causal = r_i <= c_i
    is_f = (c_i >= SM_CF) & (c_i < SM_CF + MLSTM_HEADS)
    ones = jnp.ones((hd, L), BF16)
    chains = [(bb, h) for bb in range(nb) for h in range(MLSTM_HEADS)]
    xc, xr, q_t, v_t, o_t = [], [], [], [], []
    for bb in range(nb):
        g = sm_ref[bb] + gb_ref[...]
        lf = jax.nn.log_sigmoid(g)
        bcum = jnp.dot(tril_f, lf, precision=lax.Precision.HIGHEST, preferred_element_type=F32)
        xc.append(jnp.where(is_f, bcum, g))
        xr.append(xc[bb].T)
        q_t.append(q_ref[bb].astype(F32).T.astype(BF16))
        v_t.append(v_ref[bb].astype(F32).T.astype(BF16))
        o_t.append(o_ref[bb].astype(F32).T)

    qh_t, kh, v_ext_t, qk, gq = {}, {}, {}, {}, {}
    for st, (bb, h) in enumerate(chains):
        qh_t[st] = q_t[bb][h * hd:(h + 1) * hd, :]
        kh[st] = (k_ref[bb, :, h * hd:(h + 1) * hd].astype(F32) * (hd ** -0.5)).astype(BF16)
        v_ext_t[st] = jnp.concatenate([v_t[bb][h * hd:(h + 1) * hd, :], ones], axis=0)
        qk[st] = jnp.dot(kh[st], qh_t[st], preferred_element_type=F32)
        gq[st] = jnp.dot(c_ref[st].astype(BF16), qh_t[st], preferred_element_type=F32)

    a, w_inter, mjs, b_rows, ig_rows, m_sts = {}, {}, {}, {}, {}, {}
    for st, (bb, h) in enumerate(chains):
        b_row = xr[bb][SM_CF + h:SM_CF + h + 1, :]
        ig_row = xr[bb][SM_CI + h:SM_CI + h + 1, :]
        src_col = xc[bb][:, SM_CI + h:SM_CI + h + 1] - xc[bb][:, SM_CF + h:SM_CF + h + 1]
        m_st = m_ref[st][0:1, 0:1]
        dm = jnp.where(causal, b_row + src_col, -jnp.inf)
        inter = b_row + m_st
        mj = jnp.maximum(inter, jnp.max(dm, axis=0, keepdims=True))
        a[st] = (jnp.exp(dm - mj) * qk[st]).astype(BF16)
        w_inter[st] = jnp.exp(inter - mj)
        mjs[st], b_rows[st], ig_rows[st], m_sts[st] = mj, b_row, ig_row, m_st

    av = {st: jnp.dot(v_ext_t[st], a[st], preferred_element_type=F32) for st in a}

    vw, decays = {}, {}
    for bb in range(nb):
        outs = []
        for h in range(MLSTM_HEADS):
            st = bb * MLSTM_HEADS + h
            mj, b_row = mjs[st], b_rows[st]
            r = w_inter[st] * gq[st] + av[st]
            num = r[:hd, :]
            den = r[hd:hd + 1, :]
            hout = num / jnp.maximum(jnp.abs(den), jnp.exp(-mj))
            b_last = b_row[:, L - 1:L]
            m_new = mj[:, L - 1:L]
            wk = jnp.exp(b_last - b_row + ig_rows[st] - m_new)
            decays[st] = jnp.exp(b_last + m_sts[st] - m_new)
            vw[st] = (v_ext_t[st] * wk).astype(BF16)
            m_ref[st] = jnp.broadcast_to(m_new, m_ref.shape[1:])
            ms = jnp.mean(hout * hout, axis=0, keepdims=True)
            hn = hout * lax.rsqrt(ms + EPS) * ng_ref[h * hd:(h + 1) * hd, :]
            outs.append(jax.nn.sigmoid(o_t[bb][h * hd:(h + 1) * hd, :]) * hn)
        y_ref[bb] = jnp.concatenate(outs, axis=0).T.astype(BF16)

    for st in vw:
        c_ref[st] = decays[st] * c_ref[st] + jnp.dot(vw[st], kh[st], preferred_element_type=F32)


def _mlstm(proj, small, gbias, norm_g, layer, batch, seq, nb=MLSTM_SEQS_PER_STEP):
    t = proj.shape[0]
    L = MLSTM_CHUNK
    nc = seq // L
    nb = min(nb, batch)
    assert batch % nb == 0
    proj3 = proj.reshape(batch, seq, PROJ_W)
    small3 = small.reshape(batch, seq, LANES)

    def col(off):
        return pl.BlockSpec((nb, L, 256), lambda b, c: (b, c, off // 256))

    kern = functools.partial(_mlstm_kernel, nb=nb)
    y = pl.pallas_call(
        kern,
        grid=(batch // nb, nc),
        in_specs=[col(COL_CQ), col(COL_CK), col(COL_CV), col(COL_CO),
                  pl.BlockSpec((nb, L, LANES), lambda b, c: (b, c, 0)),
                  pl.BlockSpec((None, 1, LANES), lambda b, c: (layer, 0, 0)),
                  pl.BlockSpec((None, 256, LANES), lambda b, c: (layer, 0, 0))],
        out_specs=pl.BlockSpec((nb, L, 256), lambda b, c: (b, c, 0)),
        out_shape=jax.ShapeDtypeStruct((batch, seq, 256), BF16),
        scratch_shapes=[pltpu.VMEM((nb * MLSTM_HEADS, LANES, MLSTM_HD), F32),
                        pltpu.VMEM((nb * MLSTM_HEADS, SUBLANES, LANES), F32)],
        compiler_params=_vmem_params(("parallel", "arbitrary"), 32),
        name="mlstm",
    )(proj3, proj3, proj3, proj3, small3, gbias, norm_g)
    return y.reshape(t, 256)


def _merge_kernel(x_ref, uv_ref, db_ref, dc_ref, dx_ref, dcp_ref, dxp_ref, yb_ref, yc_ref,
                  ln_ref, wg_ref, sn_ref, sw_ref, sb_ref, cw_ref, wb_ref, wo_ref, o_ref, *, tm, tiles_per_seq):
    i = pl.program_id(0)
    L = SGU_CHUNK
    u = jax.nn.gelu(uv_ref[:, :BRANCH_W].astype(F32))
    v = jax.nn.gelu(uv_ref[:, BRANCH_W:].astype(F32))
    ms = jnp.mean(v * v, axis=-1, keepdims=True)
    vn = (v * lax.rsqrt(ms + EPS) * sn_ref[...]).astype(BF16)
    r_i = lax.broadcasted_iota(jnp.int32, (L, L), 0)
    c_i = lax.broadcasted_iota(jnp.int32, (L, L), 1)
    tril = r_i >= c_i
    wm = [jnp.where(tril, sw_ref[g], 0.0).astype(BF16) for g in range(SGU_GROUPS)]
    sb = sb_ref[...]
    chunks = []
    for c in range(tm // L):
        parts = []
        for g in range(SGU_GROUPS):
            vg = vn[c * L:(c + 1) * L, g * SGU_GD:(g + 1) * SGU_GD]
            parts.append(jnp.dot(wm[g], vg, preferred_element_type=F32) + sb[:, g:g + 1])
        chunks.append(jnp.concatenate(parts, axis=1))
    y_a = u * jnp.concatenate(chunks, axis=0)

    z = dc_ref[...].astype(F32) * dx_ref[...].astype(F32)
    zp = dcp_ref[...].astype(F32) * dxp_ref[...].astype(F32)
    zp = jnp.where(i % tiles_per_seq == 0, jnp.zeros_like(zp), zp)
    zz = jnp.concatenate([zp, z], axis=0)
    cw = cw_ref[...]
    conv = cw[0:1, :] * zz[PREV_ROWS - 2:PREV_ROWS - 2 + tm, :] + cw[1:2, :] * zz[PREV_ROWS - 1:PREV_ROWS - 1 + tm, :] \
        + cw[2:3, :] * z
    y_d = db_ref[...].astype(F32) * conv

    xf = x_ref[...]
    h = (xf * lax.rsqrt(jnp.mean(xf * xf, axis=-1, keepdims=True) + EPS) * ln_ref[...]).astype(BF16)
    ys = (y_a, yb_ref[...], yc_ref[...], y_d)
    merged = jnp.zeros((tm, D_MODEL), F32)
    for n in range(N_BRANCH):
        g = jnp.dot(h, wg_ref[:, n * D_MODEL:(n + 1) * D_MODEL], preferred_element_type=F32)
        gate = 0.5 * jnp.tanh(0.5 * g) + 0.5
        merged = merged + gate * jnp.dot(ys[n].astype(BF16), wb_ref[n], preferred_element_type=F32)
    o_ref[...] = xf + jnp.dot(merged.astype(BF16), wo_ref[...], preferred_element_type=F32)


def _merge(x2, proj, y_b, y_c, ln_mix, w_gate, sgu_norm, sgu_w, sgu_bt, conv_w, w_branch, w_out, layer, seq, tm=512):
    t = x2.shape[0]
    tm = min(tm, seq)
    tiles_per_seq = seq // tm
    rb = tm // PREV_ROWS

    def col(width, off):
        return pl.BlockSpec((tm, width), lambda i: (i, off // width))

    def prev(off):
        return pl.BlockSpec((PREV_ROWS, 256), lambda i: (jnp.maximum(i * rb - 1, 0), off // 256))

    def full(shape):
        return pl.BlockSpec((None,) + shape, lambda i: (layer,) + (0,) * len(shape))

    kern = functools.partial(_merge_kernel, tm=tm, tiles_per_seq=tiles_per_seq)
    return pl.pallas_call(
        kern,
        grid=(t // tm,),
        in_specs=[pl.BlockSpec((tm, D_MODEL), lambda i: (i, 0)),
                  col(512, COL_AU), col(256, COL_DB), col(256, COL_DC), col(256, COL_DX),
                  prev(COL_DC), prev(COL_DX),
                  pl.BlockSpec((tm, 256), lambda i: (i, 0)), pl.BlockSpec((tm, 256), lambda i: (i, 0)),
                  full((1, D_MODEL)), full((D_MODEL, GATE_W)),
                  full((1, BRANCH_W)),
                  full((SGU_GROUPS, SGU_CHUNK, SGU_CHUNK)), full((SGU_CHUNK, LANES)),
                  full((SUBLANES, BRANCH_W)), full((N_BRANCH, BRANCH_W, D_MODEL)), full((D_MODEL, D_MODEL))],
        out_specs=pl.BlockSpec((tm, D_MODEL), lambda i: (i, 0)),
        out_shape=jax.ShapeDtypeStruct((t, D_MODEL), F32),
        compiler_params=_vmem_params(("parallel",), 56),
        name="merge",
    )(x2, proj, proj, proj, proj, proj, proj, y_b, y_c, ln_mix, w_gate, sgu_norm, sgu_w, sgu_bt, conv_w, w_branch, w_out)


def _mlp_kernel(x_ref, g_ref, wu_ref, wd_ref, o_ref):
    x = x_ref[...]
    ms = jnp.mean(x * x, axis=-1, keepdims=True)
    h = (x * lax.rsqrt(ms + EPS) * g_ref[...]).astype(BF16)
    up = jnp.maximum(jnp.dot(h, wu_ref[...], preferred_element_type=F32), 0.0)
    o_ref[...] = x + jnp.dot((up * up).astype(BF16), wd_ref[...], preferred_element_type=F32)


def _mlp(x2, g, w_up, w_down, layer, tm=512):
    t = x2.shape[0]
    tm = min(tm, t)
    resident = pl.Buffered(1)
    return pl.pallas_call(
        _mlp_kernel,
        grid=(t // tm,),
        in_specs=[pl.BlockSpec((tm, D_MODEL), lambda i: (i, 0)),
                  pl.BlockSpec((None, 1, D_MODEL), lambda i: (layer, 0, 0)),
                  pl.BlockSpec((None, D_MODEL, D_FF), lambda i: (layer, 0, 0), pipeline_mode=resident),
                  pl.BlockSpec((None, D_FF, D_MODEL), lambda i: (layer, 0, 0), pipeline_mode=resident)],
        out_specs=pl.BlockSpec((tm, D_MODEL), lambda i: (i, 0)),
        out_shape=jax.ShapeDtypeStruct((t, D_MODEL), F32),
        compiler_params=_vmem_params(("parallel",), 56),
        name="mlp",
    )(x2, g, w_up, w_down)


W_IN_SIZES = (256, 256, 256, 256, 256, 512, 64, 8, 256, 256, 256, 256, 4, 4, 256, 256, 256, 4096)
O_KI = sum(W_IN_SIZES[:6])
O_CQ = sum(W_IN_SIZES[:8])
O_CI = sum(W_IN_SIZES[:12])
O_DB = sum(W_IN_SIZES[:14])
O_G = sum(W_IN_SIZES[:17])


def _relayout_kernel(w_ref, o_ref, og_ref):
    x = w_ref[...]
    rows = x.shape[0]
    small = jnp.concatenate([x[:, O_KI:O_CQ], x[:, O_CI:O_DB],
                             jnp.zeros((rows, LANES - (O_CQ - O_KI) - (O_DB - O_CI)), x.dtype)], axis=1)
    o_ref[:, :COL_CQ] = x[:, :O_KI].astype(BF16)
    o_ref[:, COL_CQ:COL_DB] = x[:, O_CQ:O_CI].astype(BF16)
    o_ref[:, COL_DB:COL_SMALL] = x[:, O_DB:O_G].astype(BF16)
    o_ref[:, COL_SMALL:COL_SMALL + LANES] = small.astype(BF16)
    o_ref[:, COL_SMALL + LANES:] = jnp.zeros((rows, PROJ_W - COL_SMALL - LANES), BF16)
    og_ref[...] = x[:, O_G:].astype(BF16)


def _relayout_w_in(w_in, tr=128):
    assert O_KI == COL_CQ and O_CI - O_CQ == COL_DB - COL_CQ and O_G - O_DB == COL_SMALL - COL_DB
    depth, d, in_w = w_in.shape
    return pl.pallas_call(
        _relayout_kernel,
        grid=(depth, d // tr),
        in_specs=[pl.BlockSpec((None, tr, in_w), lambda l, i: (l, i, 0))],
        out_specs=[pl.BlockSpec((None, tr, PROJ_W), lambda l, i: (l, i, 0)),
                   pl.BlockSpec((None, tr, GATE_W), lambda l, i: (l, i, 0))],
        out_shape=[jax.ShapeDtypeStruct((depth, d, PROJ_W), BF16), jax.ShapeDtypeStruct((depth, d, GATE_W), BF16)],
        compiler_params=_vmem_params(("parallel", "parallel"), 32),
        name="relayout",
    )(w_in)


def _rope_tables(seq):
    half = ATT_HD // 2
    inv = jnp.float32(ROPE_THETA) ** (-jnp.arange(half, dtype=jnp.float32) * 2.0 / ATT_HD)
    ang = jnp.arange(seq, dtype=jnp.int32).astype(jnp.float32)[:, None] * inv[None, :]
    cos = jnp.cos(ang)
    sin = jnp.sin(ang)
    cos_t = jnp.concatenate([cos, cos, cos, cos], axis=1)
    sin_t = jnp.concatenate([-sin, sin, -sin, sin], axis=1)
    return cos_t, sin_t


def _forward(x, ln_mix, w_in, sgu_norm, sgu_w, sgu_b, q_norm, k_norm, kidx_norm, mlstm_i_bias, mlstm_f_bias,
             mlstm_norm, conv_w, w_branch, w_out, ln_mlp, w_up, w_down):
    batch, seq, d = x.shape
    depth = w_in.shape[0]
    x2 = x.reshape(batch * seq, d)
    cos_t, sin_t = _rope_tables(seq)
    w_in_r, w_gate = _relayout_w_in(w_in)
    w_branch_b, w_out_b, w_up_b, w_down_b = (w.astype(BF16) for w in (w_branch, w_out, w_up, w_down))
    qn2, kn2, kin2 = (jnp.tile(g, (1, 2))[:, None, :] for g in (q_norm, k_norm, kidx_norm))
    logit_bound = (ATT_HD ** 0.5 * LOG2E * jnp.max(jnp.abs(q_norm), axis=1, keepdims=True)
                   * jnp.max(jnp.abs(k_norm), axis=1, keepdims=True))
    gbias = jnp.concatenate([jnp.zeros((depth, SM_CI), F32), mlstm_i_bias, mlstm_f_bias,
                             jnp.zeros((depth, LANES - SM_CF - MLSTM_HEADS), F32)], axis=1)[:, None, :]
    mnorm = jnp.broadcast_to(mlstm_norm[:, :, None], (depth, MLSTM_HEADS * MLSTM_HD, LANES))
    sgu_bt = jnp.pad(jnp.swapaxes(sgu_b, 1, 2), ((0, 0), (0, 0), (0, LANES - SGU_GROUPS)))
    conv_p = jnp.pad(conv_w, ((0, 0), (0, SUBLANES - CONV_WIDTH), (0, 0)))
    for l in range(depth):
        proj, small = _inproj(x2, ln_mix[:, None, :], w_in_r, l)
        q, k, v, qi, ki = _attprep(proj, small, cos_t, sin_t, qn2, kn2, kin2, l, seq)
        y_b = _dsa(logit_bound, qi, small, q, ki, k, v, l, batch, seq)
        y_c = _mlstm(proj, small, gbias, mnorm, l, batch, seq)
        x2 = _merge(x2, proj, y_b, y_c, ln_mix[:, None, :], w_gate, sgu_norm[:, None, :], sgu_w, sgu_bt, conv_p,
                    w_branch_b, w_out_b, l, seq)
        x2 = _mlp(x2, ln_mlp[:, None, :], w_up_b, w_down_b, l)
    return x2.reshape(batch, seq, d)


def kernel(x, ln_mix, w_in, sgu_norm, sgu_w, sgu_b, q_norm, k_norm, kidx_norm, mlstm_i_bias, mlstm_f_bias,
           mlstm_norm, conv_w, w_branch, w_out, ln_mlp, w_up, w_down):
    return _forward(x, ln_mix, w_in, sgu_norm, sgu_w, sgu_b, q_norm, k_norm, kidx_norm, mlstm_i_bias,
                    mlstm_f_bias, mlstm_norm, conv_w, w_branch, w_out, ln_mlp, w_up, w_down)
```

```python
import functools

import jax
import jax.numpy as jnp
from jax import lax
from jax.experimental import pallas as pl
from jax.experimental.pallas import tpu as pltpu

F32 = jnp.float32
BF16 = jnp.bfloat16

D_MODEL = 1024
N_BRANCH = 4
BRANCH_W = 256
SGU_GROUPS = 4
SGU_GD = BRANCH_W // SGU_GROUPS
SGU_CHUNK = 128
ATT_HEADS = 4
ATT_HD = 64
IDX_HEADS = 8
IDX_HD = 64
IDX_W = IDX_HEADS * IDX_HD
TOPK_MAX = 256
ROPE_THETA = 10000.0
MLSTM_HEADS = 4
MLSTM_HD = 64
MLSTM_CHUNK = 128
CONV_WIDTH = 3
D_FF = 4 * D_MODEL
EPS = 1e-6

LANES = 128
SUBLANES = 8
PREV_ROWS = 16

COL_AU = 0
COL_AV = 256
COL_Q = 512
COL_K = 768
COL_V = 1024
COL_QI = 1280
COL_CQ = 1792
COL_CK = 2048
COL_CV = 2304
COL_CO = 2560
COL_DB = 2816
COL_DC = 3072
COL_DX = 3328
COL_SMALL = 3584
PROJ_W = 4096
GATE_W = N_BRANCH * D_MODEL
SM_WI = IDX_HD
SM_CI = SM_WI + IDX_HEADS
SM_CF = SM_CI + MLSTM_HEADS

LOG2E = 1.4426950408889634
NEG_BIG = -1e30
M_INIT = -1e29
F32_MIN = -3.4028234663852886e38
F32_ABOVE_MIN = -3.4028232635611926e38
F32_TINY = 1.1754943508222875e-38
BISECT_COARSE_PASSES = 9
BISECT_FIRST_PASSES = 10
BISECT_PASSES_PER_ROUND = 2
MAX_BISECT_ROUNDS = 160
SOFTMAX_BOUND_MAX = 60.0
MLSTM_SEQS_PER_STEP = 4


def _vmem_params(sem, mib):
    return pltpu.CompilerParams(dimension_semantics=sem, vmem_limit_bytes=mib * 1024 * 1024)


def _inproj_kernel(x_ref, g_ref, w_ref, o_ref, sm_ref, h_ref, *, small_tile, small_off):
    j = pl.program_id(1)

    @pl.when(j == 0)
    def _():
        x = x_ref[...]
        ms = jnp.mean(x * x, axis=-1, keepdims=True)
        h_ref[...] = (x * lax.rsqrt(ms + EPS) * g_ref[...]).astype(BF16)

    res = jnp.dot(h_ref[...], w_ref[...], preferred_element_type=F32)
    o_ref[...] = res.astype(BF16)

    @pl.when(j == small_tile)
    def _():
        sm_ref[...] = res[:, small_off:small_off + LANES]


def _inproj(x2, g, w, layer, tm=2048, tn=2048):
    t = x2.shape[0]
    tm = min(tm, t)
    kern = functools.partial(_inproj_kernel, small_tile=COL_SMALL // tn, small_off=COL_SMALL % tn)
    return pl.pallas_call(
        kern,
        grid=(t // tm, PROJ_W // tn),
        in_specs=[
            pl.BlockSpec((tm, D_MODEL), lambda i, j: (i, 0)),
            pl.BlockSpec((None, 1, D_MODEL), lambda i, j: (layer, 0, 0)),
            pl.BlockSpec((None, D_MODEL, tn), lambda i, j: (layer, 0, j)),
        ],
        out_specs=[pl.BlockSpec((tm, tn), lambda i, j: (i, j)), pl.BlockSpec((tm, LANES), lambda i, j: (i, 0))],
        out_shape=[jax.ShapeDtypeStruct((t, PROJ_W), BF16), jax.ShapeDtypeStruct((t, LANES), F32)],
        scratch_shapes=[pltpu.VMEM((tm, D_MODEL), BF16)],
        compiler_params=_vmem_params(("parallel", "arbitrary"), 56),
        name="inproj",
    )(x2, g, w)


def _rope_slab(x, cos, sin_signed, first_half):
    x_hi = pltpu.roll(x, LANES - ATT_HD // 2, axis=1)
    x_lo = pltpu.roll(x, ATT_HD // 2, axis=1)
    rot = jnp.where(first_half, x_hi, x_lo)
    return x * cos + rot * sin_signed


def _head_sumsq(x2, lane, n_heads):
    out = []
    for h in range(n_heads):
        m = (lane >= h * ATT_HD) & (lane < (h + 1) * ATT_HD)
        out.append(jnp.sum(jnp.where(m, x2, 0.0), axis=-1, keepdims=True))
    return out


def _attprep_kernel(q_ref, k_ref, v_ref, qi0_ref, qi1_ref, sm_ref, cos_ref, sin_ref, qn_ref, kn_ref, kin_ref,
                    qo_ref, ko_ref, vo_ref, qio_ref, kio_ref):
    cos = cos_ref[...]
    sin = sin_ref[...]
    rows = cos.shape[0]
    lane = lax.broadcasted_iota(jnp.int32, (rows, LANES), 1)
    first_half = (lane % ATT_HD) < (ATT_HD // 2)
    head0 = lane < ATT_HD

    def norm_rope(ref, g_ref, scale):
        slabs = []
        for s in range(ref.shape[1] // LANES):
            x = ref[:, s * LANES:(s + 1) * LANES].astype(F32)
            ss = _head_sumsq(x * x, lane, 2)
            r0 = lax.rsqrt(ss[0] * (1.0 / ATT_HD) + EPS)
            r1 = lax.rsqrt(ss[1] * (1.0 / ATT_HD) + EPS)
            y = x * jnp.where(head0, r0, r1) * g_ref[...]
            y = _rope_slab(y, cos, sin, first_half)
            if scale != 1.0:
                y = y * scale
            slabs.append(y)
        return slabs

    for s, y in enumerate(norm_rope(q_ref, qn_ref, ATT_HD ** -0.5 * LOG2E)):
        qo_ref[s * LANES:(s + 1) * LANES, :] = y.T.astype(BF16)
    for s, y in enumerate(norm_rope(k_ref, kn_ref, 1.0)):
        ko_ref[2 * s] = y[:, :ATT_HD].astype(BF16)
        ko_ref[2 * s + 1] = y[:, ATT_HD:].astype(BF16)
    vo_ref[...] = v_ref[...].astype(F32).T.astype(BF16)

    for s in range(IDX_HEADS * IDX_HD // LANES):
        qi_ref = (qi0_ref, qi1_ref)[s // 2]
        y = _rope_slab(qi_ref[:, (s % 2) * LANES:(s % 2 + 1) * LANES].astype(F32), cos, sin, first_half)
        qio_ref[s * LANES:(s + 1) * LANES, :] = (y * (IDX_HD ** -0.5)).T.astype(BF16)

    sm = sm_ref[...]
    mu = jnp.sum(jnp.where(head0, sm, 0.0), axis=-1, keepdims=True) * (1.0 / IDX_HD)
    d = sm - mu
    var = jnp.sum(jnp.where(head0, d * d, 0.0), axis=-1, keepdims=True) * (1.0 / IDX_HD)
    y = d * lax.rsqrt(var + EPS) * kin_ref[...]
    y = _rope_slab(y, cos, sin, first_half)
    kio_ref[...] = y[:, :IDX_HD].astype(BF16)


def _attprep(proj, small, cos, sin, qn, kn, kin, layer, seq, tm=512):
    t = proj.shape[0]
    tm = min(tm, seq)
    npos = seq // tm

    def col(width, off):
        return pl.BlockSpec((tm, width), lambda i: (i, off // width))

    vec = pl.BlockSpec((None, 1, LANES), lambda i: (layer, 0, 0))
    tab = pl.BlockSpec((tm, LANES), lambda i: (i % npos, 0))
    return pl.pallas_call(
        _attprep_kernel,
        grid=(t // tm,),
        in_specs=[col(BRANCH_W, COL_Q), col(BRANCH_W, COL_K), col(BRANCH_W, COL_V), col(BRANCH_W, COL_QI), col(BRANCH_W, COL_QI + BRANCH_W),
                  pl.BlockSpec((tm, LANES), lambda i: (i, 0)), tab, tab, vec, vec, vec],
        out_specs=[pl.BlockSpec((BRANCH_W, tm), lambda i: (0, i)),
                   pl.BlockSpec((ATT_HEADS, tm, ATT_HD), lambda i: (0, i, 0)),
                   pl.BlockSpec((BRANCH_W, tm), lambda i: (0, i)), pl.BlockSpec((IDX_W, tm), lambda i: (0, i)),
                   pl.BlockSpec((tm, IDX_HD), lambda i: (i, 0))],
        out_shape=[jax.ShapeDtypeStruct((BRANCH_W, t), BF16), jax.ShapeDtypeStruct((ATT_HEADS, t, ATT_HD), BF16),
                   jax.ShapeDtypeStruct((BRANCH_W, t), BF16), jax.ShapeDtypeStruct((IDX_W, t), BF16),
                   jax.ShapeDtypeStruct((t, IDX_HD), BF16)],
        compiler_params=_vmem_params(("parallel",), 32),
        name="attprep",
    )(proj, proj, proj, proj, proj, small, cos, sin, qn, kn, kin)


def _dsa_kernel(mb_ref, qi_ref, sm_ref, q_ref, ki_ref, k_ref, vt_ref, o_ref, sc_ref, acc_ref, s0_ref, s1_ref,
                i0_ref, i1_ref, sc16_ref, *, tq, n_sel, seq, layer):
    i = pl.program_id(1)
    tk1 = min(128, tq)
    tk2 = tq
    tk3 = tq
    n_keys = (i + 1) * tq
    qpos = i * tq + lax.broadcasted_iota(jnp.int32, (1, tq), 1)

    wt = (sm_ref[...] * (IDX_HEADS ** -0.5)).T
    qi_h = [qi_ref[h * IDX_HD:(h + 1) * IDX_HD, :] for h in range(IDX_HEADS)]
    w_h = [wt[SM_WI + h:SM_WI + h + 1, :] for h in range(IDX_HEADS)]
    kidx1 = lax.broadcasted_iota(jnp.int32, (tk1, tq), 0)

    def idx_scores_into(buf_ref, c):
        kc = ki_ref[pl.ds(pl.multiple_of(c * tk1, tk1), tk1), :]
        for h in range(IDX_HEADS):
            buf_ref[h] = jnp.dot(kc, qi_h[h], preferred_element_type=F32)

    def combine(buf_ref, c, amax):
        start = pl.multiple_of(c * tk1, tk1)
        acc = jnp.zeros((tk1, tq), F32)
        for h in range(IDX_HEADS):
            acc = acc + w_h[h] * jnp.maximum(buf_ref[h], 0.0)
        masked = jnp.where(kidx1 + start <= qpos, acc, F32_MIN)
        sc_ref[pl.ds(start, tk1), :] = masked
        sc16_ref[pl.ds(start, tk1), :] = masked.astype(BF16)
        return jnp.maximum(amax, jnp.max(jnp.abs(acc), axis=0, keepdims=True))

    n_chunks1 = n_keys // tk1
    last1 = n_chunks1 - 1

    def score_pair(j, amax):
        c_a = 2 * j
        idx_scores_into(i1_ref, c_a + 1)
        amax = combine(i0_ref, c_a, amax)
        idx_scores_into(i0_ref, jnp.minimum(c_a + 2, last1))
        return combine(i1_ref, c_a + 1, amax)

    q_h = [q_ref[h * ATT_HD:(h + 1) * ATT_HD, :] for h in range(ATT_HEADS)]

    def scores_into(buf_ref, c):
        start = pl.multiple_of(c * tk3, tk3)
        for h in range(ATT_HEADS):
            buf_ref[h] = jnp.dot(k_ref[h, pl.ds(start, tk3), :], q_h[h], preferred_element_type=F32)

    idx_scores_into(i0_ref, 0)
    scores_into(s0_ref, 0)
    amax = lax.fori_loop(0, n_chunks1 // 2, score_pair, jnp.zeros((1, tq), F32))

    def count(pred_fn):
        def body(c, acc):
            start = pl.multiple_of(c * tk2, tk2)
            x = sc_ref[pl.ds(start, tk2), :]
            return acc + jnp.sum(pred_fn(x).reshape(tk2 // 32, 32, tq), axis=0)
        acc = lax.fori_loop(0, n_keys // tk2, body, jnp.zeros((32, tq), F32))
        return jnp.sum(acc, axis=0, keepdims=True)

    def count16(cand16):
        one, zero = jnp.ones((), BF16), jnp.zeros((), BF16)

        def body(c, acc):
            start = pl.multiple_of(c * tk2, tk2)
            m = jnp.where(sc16_ref[pl.ds(start, tk2), :] >= cand16, one, zero)
            parts = [m[g * 32:(g + 1) * 32, :] for g in range(tk2 // 32)]
            while len(parts) > 1:
                parts = [parts[g] + parts[g + 1] for g in range(0, len(parts), 2)]
            return acc + parts[0].astype(F32)
        acc = lax.fori_loop(0, n_keys // tk2, body, jnp.zeros((32, tq), F32))
        return jnp.sum(acc, axis=0, keepdims=True)

    n_pos = count(lambda x: jnp.where(x > 0.0, 1.0, 0.0))
    n_nn = count(lambda x: jnp.where(x >= 0.0, 1.0, 0.0))
    n_causal = (qpos + 1).astype(F32)
    k_sel = jnp.float32(n_sel)
    take_all = n_causal <= k_sel
    at_zero = (n_pos < k_sel) & (n_nn >= k_sel)
    wide = amax + amax * 2.0 ** -20 + F32_TINY
    positive = n_pos >= k_sel
    lo0 = jnp.where(positive, 0.0, -wide)
    hi0 = jnp.where(positive, wide, 0.0)
    nhi0 = jnp.where(positive, 0.0, n_nn)
    thr0 = jnp.where(take_all, F32_ABOVE_MIN, 0.0)
    tie0 = jnp.where(at_zero & (n_nn > k_sel) & jnp.logical_not(take_all), 1.0, 0.0)
    need0 = k_sel - n_pos
    act0 = jnp.where(take_all | at_zero, 0.0, 1.0)

    def bisect_pass(state):
        lo, hi, n_hi, thr, tie, need, act = state
        cand = 0.5 * lo + 0.5 * hi
        cnt = count(lambda x: jnp.where(x >= cand, 1.0, 0.0))
        live = act > 0.0
        stuck = live & ((cand <= lo) | (cand >= hi))
        hit = live & jnp.logical_not(stuck) & (cnt == k_sel)
        up = live & jnp.logical_not(stuck) & (cnt > k_sel)
        down = live & jnp.logical_not(stuck) & (cnt < k_sel)
        thr = jnp.where(hit, cand, jnp.where(stuck, lo, thr))
        tie = jnp.where(stuck, 1.0, tie)
        need = jnp.where(stuck, k_sel - n_hi, need)
        lo = jnp.where(up, cand, lo)
        hi = jnp.where(down, cand, hi)
        n_hi = jnp.where(down, cnt, n_hi)
        act = jnp.where(hit | stuck, 0.0, act)
        return lo, hi, n_hi, thr, tie, need, act

    def search_cond(carry):
        it, state = carry
        return (it < MAX_BISECT_ROUNDS) & (jnp.max(state[6]) > 0.0)

    def search_body(carry):
        it, state = carry
        for _ in range(BISECT_PASSES_PER_ROUND):
            state = bisect_pass(state)
        return it + 1, state

    def coarse_pass(_, bracket):
        lo, hi = bracket
        c16 = (0.5 * lo + 0.5 * hi).astype(BF16)
        cf = c16.astype(F32)
        cnt = count16(c16)
        below = cf - jnp.abs(cf) * 2.0 ** -7 - F32_TINY
        live = (act0 > 0.0) & (cf < hi)
        hi = jnp.where(live & (cnt < k_sel) & (cf > lo), cf, hi)
        lo = jnp.where(live & (cnt >= k_sel) & (below > lo), below, lo)
        return lo, hi

    lo1, hi1 = lax.fori_loop(0, BISECT_COARSE_PASSES, coarse_pass, (lo0, hi0))
    nhi1 = count(lambda x: jnp.where(x >= hi1, 1.0, 0.0))
    state = lax.fori_loop(0, BISECT_FIRST_PASSES, lambda _, st: bisect_pass(st),
                          (lo1, hi1, nhi1, thr0, tie0, need0, act0))
    _, state = lax.while_loop(search_cond, search_body, (jnp.int32(0), state))
    thr, tie, need = state[3], state[4], state[5]

    tri = (lax.broadcasted_iota(jnp.int32, (tk2, tk2), 0) >= lax.broadcasted_iota(jnp.int32, (tk2, tk2), 1))
    tri = jnp.where(tri, 1.0, 0.0).astype(BF16)

    def drop_chunk(c, run):
        start = pl.multiple_of(c * tk2, tk2)
        x = sc_ref[pl.ds(start, tk2), :]
        eq = jnp.where(tie > 0.0, jnp.where(x == thr, 1.0, 0.0), 0.0)
        rank = run + jnp.dot(tri, eq.astype(BF16), preferred_element_type=F32)
        sc_ref[pl.ds(start, tk2), :] = jnp.where(eq > 0.0, jnp.where(rank > need, F32_MIN, x), x)
        return rank[tk2 - 1:tk2, :]

    @pl.when(jnp.max(tie) > 0.0)
    def _():
        lax.fori_loop(0, n_keys // tk2, drop_chunk, jnp.zeros((1, tq), F32))

    acc_ref[...] = jnp.zeros(acc_ref.shape, F32)
    bound = mb_ref[layer, 0]

    def attend_bounded(buf_ref, c, live, l_run):
        start = pl.multiple_of(c * tk3, tk3)
        sel = (sc_ref[pl.ds(start, tk3), :] >= thr) & live
        l_out = []
        for h in range(ATT_HEADS):
            p = jnp.where(sel, jnp.exp2(buf_ref[h] - bound), 0.0)
            l_out.append(l_run[h] + jnp.sum(p, axis=0, keepdims=True))
            vt = vt_ref[h * ATT_HD:(h + 1) * ATT_HD, pl.ds(start, tk3)]
            acc_ref[h * ATT_HD:(h + 1) * ATT_HD, :] += jnp.dot(vt, p.astype(BF16), preferred_element_type=F32)
        return tuple(l_out)

    n_chunks3 = n_keys // tk3
    last = n_chunks3 - 1

    def attn_bounded_pair(j, l_run):
        c_a = 2 * j
        c_b = jnp.minimum(c_a + 1, last)
        scores_into(s1_ref, c_b)
        l_run = attend_bounded(s0_ref, c_a, True, l_run)
        scores_into(s0_ref, jnp.minimum(c_a + 2, last))
        return attend_bounded(s1_ref, c_b, c_a + 1 <= last, l_run)

    def attn_online(c, carry):
        m_run, l_run = carry
        start = pl.multiple_of(c * tk3, tk3)
        sel = sc_ref[pl.ds(start, tk3), :] >= thr
        s_all = [jnp.dot(k_ref[h, pl.ds(start, tk3), :], q_h[h], preferred_element_type=F32)
                 for h in range(ATT_HEADS)]
        m_out, l_out, alphas, ps = [], [], [], []
        for h in range(ATT_HEADS):
            s = jnp.where(sel, s_all[h], NEG_BIG)
            m_new = jnp.maximum(m_run[h], jnp.max(s, axis=0, keepdims=True))
            alpha = jnp.exp2(m_run[h] - m_new)
            p = jnp.exp2(s - m_new)
            l_out.append(alpha * l_run[h] + jnp.sum(p, axis=0, keepdims=True))
            m_out.append(m_new)
            alphas.append(alpha)
            ps.append(p.astype(BF16))
        for h in range(ATT_HEADS):
            vt = vt_ref[h * ATT_HD:(h + 1) * ATT_HD, pl.ds(start, tk3)]
            pv = jnp.dot(vt, ps[h], preferred_element_type=F32)
            acc_ref[h * ATT_HD:(h + 1) * ATT_HD, :] = alphas[h] * acc_ref[h * ATT_HD:(h + 1) * ATT_HD, :] + pv
        return tuple(m_out), tuple(l_out)

    zeros = tuple(jnp.zeros((1, tq), F32) for _ in range(ATT_HEADS))

    def run_bounded():
        return lax.fori_loop(0, (n_chunks3 + 1) // 2, attn_bounded_pair, zeros)

    def run_online():
        init = (tuple(jnp.full((1, tq), M_INIT, F32) for _ in range(ATT_HEADS)), zeros)
        return lax.fori_loop(0, n_keys // tk3, attn_online, init)[1]

    l_fin = lax.cond(bound < SOFTMAX_BOUND_MAX, run_bounded, run_online)
    outs = [acc_ref[h * ATT_HD:(h + 1) * ATT_HD, :] / l_fin[h] for h in range(ATT_HEADS)]
    o_ref[...] = jnp.concatenate(outs, axis=0).T.astype(BF16)


def _dsa(logit_bound, qi, small, q, ki, k, vt, layer, batch, seq, tq=256):
    t = q.shape[1]
    tq = min(tq, seq)
    nq = seq // tq
    n_sel = min(TOPK_MAX, seq // 4)
    assert tq >= n_sel and seq % tq == 0 and tq % (2 * LANES) == 0
    kern = functools.partial(_dsa_kernel, tq=tq, n_sel=n_sel, seq=seq, layer=layer)
    return pl.pallas_call(
        kern,
        grid=(batch, nq),
        in_specs=[
            pl.BlockSpec(memory_space=pltpu.SMEM),
            pl.BlockSpec((IDX_W, tq), lambda b, i: (0, b * nq + i)),
            pl.BlockSpec((tq, LANES), lambda b, i: (b * nq + i, 0)),
            pl.BlockSpec((BRANCH_W, tq), lambda b, i: (0, b * nq + i)),
            pl.BlockSpec((seq, IDX_HD), lambda b, i: (b, 0)),
            pl.BlockSpec((ATT_HEADS, seq, ATT_HD), lambda b, i: (0, b, 0)),
            pl.BlockSpec((BRANCH_W, seq), lambda b, i: (0, b)),
        ],
        out_specs=pl.BlockSpec((tq, BRANCH_W), lambda b, i: (b * nq + i, 0)),
        out_shape=jax.ShapeDtypeStruct((t, BRANCH_W), BF16),
        scratch_shapes=[pltpu.VMEM((seq, tq), F32), pltpu.VMEM((ATT_HEADS * ATT_HD, tq), F32),
                        pltpu.VMEM((ATT_HEADS, tq, tq), F32), pltpu.VMEM((ATT_HEADS, tq, tq), F32),
                        pltpu.VMEM((IDX_HEADS, min(128, tq), tq), F32), pltpu.VMEM((IDX_HEADS, min(128, tq), tq), F32),
                        pltpu.VMEM((seq, tq), BF16)],
        compiler_params=_vmem_params(("parallel", "arbitrary"), 48),
        name="dsa",
    )(logit_bound, qi, small, q, ki, k, vt)


def _mlstm_kernel(q_ref, k_ref, v_ref, o_ref, sm_ref, gb_ref, ng_ref, y_ref, c_ref, m_ref, *, nb):
    L = MLSTM_CHUNK
    hd = MLSTM_HD

    @pl.when(pl.program_id(1) == 0)
    def _():
        c_ref[...] = jnp.zeros_like(c_ref)
        m_ref[...] = jnp.zeros_like(m_ref)

    r_i = lax.broadcasted_iota(jnp.int32, (L, L), 0)
    c_i = lax.broadcasted_iota(jnp.int32, (L, L), 1)
    tril_f = (r_i >= c_i).astype(F32)
    causal = r_i <= c_i
    is_f = (c_i >= SM_CF) & (c_i < SM_CF + MLSTM_HEADS)
    ones = jnp.ones((hd, L), BF16)
    chains = [(bb, h) for bb in range(nb) for h in range(MLSTM_HEADS)]
    xc, xr, q_t, v_t, o_t = [], [], [], [], []
    for bb in range(nb):
        g = sm_ref[bb] + gb_ref[...]
        lf = jax.nn.log_sigmoid(g)
        bcum = jnp.dot(tril_f, lf, precision=lax.Precision.HIGHEST, preferred_element_type=F32)
        xc.append(jnp.where(is_f, bcum, g))
        xr.append(xc[bb].T)
        q_t.append(q_ref[bb].astype(F32).T.astype(BF16))
        v_t.append(v_ref[bb].astype(F32).T.astype(BF16))
        o_t.append(o_ref[bb].astype(F32).T)

    qh_t, kh, v_ext_t, qk, gq = {}, {}, {}, {}, {}
    for st, (bb, h) in enumerate(chains):
        qh_t[st] = q_t[bb][h * hd:(h + 1) * hd, :]
        kh[st] = (k_ref[bb, :, h * hd:(h + 1) * hd].astype(F32) * (hd ** -0.5)).astype(BF16)
        v_ext_t[st] = jnp.concatenate([v_t[bb][h * hd:(h + 1) * hd, :], ones], axis=0)
        qk[st] = jnp.dot(kh[st], qh_t[st], preferred_element_type=F32)
        gq[st] = jnp.dot(c_ref[st].astype(BF16), qh_t[st], preferred_element_type=F32)

    a, w_inter, mjs, b_rows, ig_rows, m_sts = {}, {}, {}, {}, {}, {}
    for st, (bb, h) in enumerate(chains):
        b_row = xr[bb][SM_CF + h:SM_CF + h + 1, :]
        ig_row = xr[bb][SM_CI + h:SM_CI + h + 1, :]
        src_col = xc[bb][:, SM_CI + h:SM_CI + h + 1] - xc[bb][:, SM_CF + h:SM_CF + h + 1]
        m_st = m_ref[st][0:1, 0:1]
        dm = jnp.where(causal, b_row + src_col, -jnp.inf)
        inter = b_row + m_st
        mj = jnp.maximum(inter, jnp.max(dm, axis=0, keepdims=True))
        a[st] = (jnp.exp(dm - mj) * qk[st]).astype(BF16)
        w_inter[st] = jnp.exp(inter - mj)
        mjs[st], b_rows[st], ig_rows[st], m_sts[st] = mj, b_row, ig_row, m_st

    av = {st: jnp.dot(v_ext_t[st], a[st], preferred_element_type=F32) for st in a}

    vw, decays = {}, {}
    for bb in range(nb):
        outs = []
        for h in range(MLSTM_HEADS):
            st = bb * MLSTM_HEADS + h
            mj, b_row = mjs[st], b_rows[st]
            r = w_inter[st] * gq[st] + av[st]
            num = r[:hd, :]
            den = r[hd:hd + 1, :]
            hout = num / jnp.maximum(jnp.abs(den), jnp.exp(-mj))
            b_last = b_row[:, L - 1:L]
            m_new = mj[:, L - 1:L]
            wk = jnp.exp(b_last - b_row + ig_rows[st] - m_new)
            decays[st] = jnp.exp(b_last + m_sts[st] - m_new)
            vw[st] = (v_ext_t[st] * wk).astype(BF16)
            m_ref[st] = jnp.broadcast_to(m_new, m_ref.shape[1:])
            ms = jnp.mean(hout * hout, axis=0, keepdims=True)
            hn = hout * lax.rsqrt(ms + EPS) * ng_ref[h * hd:(h + 1) * hd, :]
            outs.append(jax.nn.sigmoid(o_t[bb][h * hd:(h + 1) * hd, :]) * hn)
        y_ref[bb] = jnp.concatenate(outs, axis=0).T.astype(BF16)

    for st in vw:
        c_ref[st] = decays[st] * c_ref[st] + jnp.dot(vw[st], kh[st], preferred_element_type=F32)


def _mlstm(proj, small, gbias, norm_g, layer, batch, seq, nb=MLSTM_SEQS_PER_STEP):
    t = proj.shape[0]
    L = MLSTM_CHUNK
    nc = seq // L
    nb = min(nb, batch)
    assert batch % nb == 0
    proj3 = proj.reshape(batch, seq, PROJ_W)
    small3 = small.reshape(batch, seq, LANES)

    def col(off):
        return pl.BlockSpec((nb, L, BRANCH_W), lambda b, c: (b, c, off // BRANCH_W))

    kern = functools.partial(_mlstm_kernel, nb=nb)
    y = pl.pallas_call(
        kern,
        grid=(batch // nb, nc),
        in_specs=[col(COL_CQ), col(COL_CK), col(COL_CV), col(COL_CO),
                  pl.BlockSpec((nb, L, LANES), lambda b, c: (b, c, 0)),
                  pl.BlockSpec((None, 1, LANES), lambda b, c: (layer, 0, 0)),
                  pl.BlockSpec((None, BRANCH_W, LANES), lambda b, c: (layer, 0, 0))],
        out_specs=pl.BlockSpec((nb, L, BRANCH_W), lambda b, c: (b, c, 0)),
        out_shape=jax.ShapeDtypeStruct((batch, seq, BRANCH_W), BF16),
        scratch_shapes=[pltpu.VMEM((nb * MLSTM_HEADS, LANES, MLSTM_HD), F32),
                        pltpu.VMEM((nb * MLSTM_HEADS, SUBLANES, LANES), F32)],
        compiler_params=_vmem_params(("parallel", "arbitrary"), 32),
        name="mlstm",
    )(proj3, proj3, proj3, proj3, small3, gbias, norm_g)
    return y.reshape(t, BRANCH_W)


def _merge_kernel(x_ref, uv_ref, db_ref, dc_ref, dx_ref, dcp_ref, dxp_ref, yb_ref, yc_ref,
                  ln_ref, wg_ref, sn_ref, sw_ref, sb_ref, cw_ref, wb_ref, wo_ref, o_ref, *, tm, tiles_per_seq):
    i = pl.program_id(0)
    L = SGU_CHUNK
    u = jax.nn.gelu(uv_ref[:, :BRANCH_W].astype(F32))
    v = jax.nn.gelu(uv_ref[:, BRANCH_W:].astype(F32))
    ms = jnp.mean(v * v, axis=-1, keepdims=True)
    vn = (v * lax.rsqrt(ms + EPS) * sn_ref[...]).astype(BF16)
    r_i = lax.broadcasted_iota(jnp.int32, (L, L), 0)
    c_i = lax.broadcasted_iota(jnp.int32, (L, L), 1)
    tril = r_i >= c_i
    wm = [jnp.where(tril, sw_ref[g], 0.0).astype(BF16) for g in range(SGU_GROUPS)]
    sb = sb_ref[...]
    chunks = []
    for c in range(tm // L):
        parts = []
        for g in range(SGU_GROUPS):
            vg = vn[c * L:(c + 1) * L, g * SGU_GD:(g + 1) * SGU_GD]
            parts.append(jnp.dot(wm[g], vg, preferred_element_type=F32) + sb[:, g:g + 1])
        chunks.append(jnp.concatenate(parts, axis=1))
    y_a = u * jnp.concatenate(chunks, axis=0)

    z = dc_ref[...].astype(F32) * dx_ref[...].astype(F32)
    zp = dcp_ref[...].astype(F32) * dxp_ref[...].astype(F32)
    zp = jnp.where(i % tiles_per_seq == 0, jnp.zeros_like(zp), zp)
    zz = jnp.concatenate([zp, z], axis=0)
    cw = cw_ref[...]
    conv = cw[0:1, :] * zz[PREV_ROWS - 2:PREV_ROWS - 2 + tm, :] + cw[1:2, :] * zz[PREV_ROWS - 1:PREV_ROWS - 1 + tm, :] \
        + cw[2:3, :] * z
    y_d = db_ref[...].astype(F32) * conv

    xf = x_ref[...]
    h = (xf * lax.rsqrt(jnp.mean(xf * xf, axis=-1, keepdims=True) + EPS) * ln_ref[...]).astype(BF16)
    ys = (y_a, yb_ref[...], yc_ref[...], y_d)
    merged = jnp.zeros((tm, D_MODEL), F32)
    for n in range(N_BRANCH):
        g = jnp.dot(h, wg_ref[:, n * D_MODEL:(n + 1) * D_MODEL], preferred_element_type=F32)
        gate = 0.5 * jnp.tanh(0.5 * g) + 0.5
        merged = merged + gate * jnp.dot(ys[n].astype(BF16), wb_ref[n], preferred_element_type=F32)
    o_ref[...] = xf + jnp.dot(merged.astype(BF16), wo_ref[...], preferred_element_type=F32)


def _merge(x2, proj, y_b, y_c, ln_mix, w_gate, sgu_norm, sgu_w, sgu_bt, conv_w, w_branch, w_out, layer, seq, tm=512):
    t = x2.shape[0]
    tm = min(tm, seq)
    tiles_per_seq = seq // tm
    rb = tm // PREV_ROWS

    def col(width, off):
        return pl.BlockSpec((tm, width), lambda i: (i, off // width))

    def prev(off):
        return pl.BlockSpec((PREV_ROWS, BRANCH_W), lambda i: (jnp.maximum(i * rb - 1, 0), off // BRANCH_W))

    def full(shape):
        return pl.BlockSpec((None,) + shape, lambda i: (layer,) + (0,) * len(shape), pipeline_mode=pl.Buffered(1))

    kern = functools.partial(_merge_kernel, tm=tm, tiles_per_seq=tiles_per_seq)
    return pl.pallas_call(
        kern,
        grid=(t // tm,),
        in_specs=[pl.BlockSpec((tm, D_MODEL), lambda i: (i, 0)),
                  col(2 * BRANCH_W, COL_AU), col(BRANCH_W, COL_DB), col(BRANCH_W, COL_DC), col(BRANCH_W, COL_DX),
                  prev(COL_DC), prev(COL_DX),
                  pl.BlockSpec((tm, BRANCH_W), lambda i: (i, 0)), pl.BlockSpec((tm, BRANCH_W), lambda i: (i, 0)),
                  full((1, D_MODEL)), full((D_MODEL, GATE_W)),
                  full((1, BRANCH_W)),
                  full((SGU_GROUPS, SGU_CHUNK, SGU_CHUNK)), full((SGU_CHUNK, LANES)),
                  full((SUBLANES, BRANCH_W)), full((N_BRANCH, BRANCH_W, D_MODEL)), full((D_MODEL, D_MODEL))],
        out_specs=pl.BlockSpec((tm, D_MODEL), lambda i: (i, 0)),
        out_shape=jax.ShapeDtypeStruct((t, D_MODEL), F32),
        compiler_params=_vmem_params(("parallel",), 56),
        name="merge",
    )(x2, proj, proj, proj, proj, proj, proj, y_b, y_c, ln_mix, w_gate, sgu_norm, sgu_w, sgu_bt, conv_w, w_branch, w_out)


def _mlp_kernel(x_ref, g_ref, wu_ref, wd_ref, o_ref):
    x = x_ref[...]
    ms = jnp.mean(x * x, axis=-1, keepdims=True)
    h = (x * lax.rsqrt(ms + EPS) * g_ref[...]).astype(BF16)
    up = jnp.maximum(jnp.dot(h, wu_ref[...], preferred_element_type=F32), 0.0)
    o_ref[...] = x + jnp.dot((up * up).astype(BF16), wd_ref[...], preferred_element_type=F32)


def _mlp(x2, g, w_up, w_down, layer, tm=512):
    t = x2.shape[0]
    tm = min(tm, t)
    resident = pl.Buffered(1)
    return pl.pallas_call(
        _mlp_kernel,
        grid=(t // tm,),
        in_specs=[pl.BlockSpec((tm, D_MODEL), lambda i: (i, 0)),
                  pl.BlockSpec((None, 1, D_MODEL), lambda i: (layer, 0, 0)),
                  pl.BlockSpec((None, D_MODEL, D_FF), lambda i: (layer, 0, 0), pipeline_mode=resident),
                  pl.BlockSpec((None, D_FF, D_MODEL), lambda i: (layer, 0, 0), pipeline_mode=resident)],
        out_specs=pl.BlockSpec((tm, D_MODEL), lambda i: (i, 0)),
        out_shape=jax.ShapeDtypeStruct((t, D_MODEL), F32),
        compiler_params=_vmem_params(("parallel",), 56),
        name="mlp",
    )(x2, g, w_up, w_down)


W_IN_SIZES = (256, 256, 256, 256, 256, 512, 64, 8, 256, 256, 256, 256, 4, 4, 256, 256, 256, 4096)
O_KI = sum(W_IN_SIZES[:6])
O_CQ = sum(W_IN_SIZES[:8])
O_CI = sum(W_IN_SIZES[:12])
O_DB = sum(W_IN_SIZES[:14])
O_G = sum(W_IN_SIZES[:17])


def _relayout_kernel(w_ref, o_ref, og_ref):
    x = w_ref[...]
    rows = x.shape[0]
    small = jnp.concatenate([x[:, O_KI:O_CQ], x[:, O_CI:O_DB],
                             jnp.zeros((rows, LANES - (O_CQ - O_KI) - (O_DB - O_CI)), x.dtype)], axis=1)
    o_ref[:, :COL_CQ] = x[:, :O_KI].astype(BF16)
    o_ref[:, COL_CQ:COL_DB] = x[:, O_CQ:O_CI].astype(BF16)
    o_ref[:, COL_DB:COL_SMALL] = x[:, O_DB:O_G].astype(BF16)
    o_ref[:, COL_SMALL:COL_SMALL + LANES] = small.astype(BF16)
    o_ref[:, COL_SMALL + LANES:] = jnp.zeros((rows, PROJ_W - COL_SMALL - LANES), BF16)
    og_ref[...] = x[:, O_G:].astype(BF16)


def _relayout_w_in(w_in, tr=128):
    assert O_KI == COL_CQ and O_CI - O_CQ == COL_DB - COL_CQ and O_G - O_DB == COL_SMALL - COL_DB
    depth, d, in_w = w_in.shape
    return pl.pallas_call(
        _relayout_kernel,
        grid=(depth, d // tr),
        in_specs=[pl.BlockSpec((None, tr, in_w), lambda l, i: (l, i, 0))],
        out_specs=[pl.BlockSpec((None, tr, PROJ_W), lambda l, i: (l, i, 0)),
                   pl.BlockSpec((None, tr, GATE_W), lambda l, i: (l, i, 0))],
        out_shape=[jax.ShapeDtypeStruct((depth, d, PROJ_W), BF16), jax.ShapeDtypeStruct((depth, d, GATE_W), BF16)],
        compiler_params=_vmem_params(("parallel", "parallel"), 32),
        name="relayout",
    )(w_in)


def _rope_tables(seq):
    half = ATT_HD // 2
    inv = jnp.float32(ROPE_THETA) ** (-jnp.arange(half, dtype=jnp.float32) * 2.0 / ATT_HD)
    ang = jnp.arange(seq, dtype=jnp.int32).astype(jnp.float32)[:, None] * inv[None, :]
    cos = jnp.cos(ang)
    sin = jnp.sin(ang)
    cos_t = jnp.concatenate([cos, cos, cos, cos], axis=1)
    sin_t = jnp.concatenate([-sin, sin, -sin, sin], axis=1)
    return cos_t, sin_t


def _forward(x, ln_mix, w_in, sgu_norm, sgu_w, sgu_b, q_norm, k_norm, kidx_norm, mlstm_i_bias, mlstm_f_bias,
             mlstm_norm, conv_w, w_branch, w_out, ln_mlp, w_up, w_down):
    batch, seq, d = x.shape
    depth = w_in.shape[0]
    x2 = x.reshape(batch * seq, d)
    cos_t, sin_t = _rope_tables(seq)
    w_in_r, w_gate = _relayout_w_in(w_in)
    w_branch_b, w_out_b, w_up_b, w_down_b = (w.astype(BF16) for w in (w_branch, w_out, w_up, w_down))
    qn2, kn2, kin2 = (jnp.tile(g, (1, 2))[:, None, :] for g in (q_norm, k_norm, kidx_norm))
    logit_bound = (ATT_HD ** 0.5 * LOG2E * jnp.max(jnp.abs(q_norm), axis=1, keepdims=True)
                   * jnp.max(jnp.abs(k_norm), axis=1, keepdims=True))
    gbias = jnp.concatenate([jnp.zeros((depth, SM_CI), F32), mlstm_i_bias, mlstm_f_bias,
                             jnp.zeros((depth, LANES - SM_CF - MLSTM_HEADS), F32)], axis=1)[:, None, :]
    mnorm = jnp.broadcast_to(mlstm_norm[:, :, None], (depth, MLSTM_HEADS * MLSTM_HD, LANES))
    sgu_bt = jnp.pad(jnp.swapaxes(sgu_b, 1, 2), ((0, 0), (0, 0), (0, LANES - SGU_GROUPS)))
    conv_p = jnp.pad(conv_w, ((0, 0), (0, SUBLANES - CONV_WIDTH), (0, 0)))
    for l in range(depth):
        proj, small = _inproj(x2, ln_mix[:, None, :], w_in_r, l)
        q, k, v, qi, ki = _attprep(proj, small, cos_t, sin_t, qn2, kn2, kin2, l, seq)
        y_b = _dsa(logit_bound, qi, small, q, ki, k, v, l, batch, seq)
        y_c = _mlstm(proj, small, gbias, mnorm, l, batch, seq)
        x2 = _merge(x2, proj, y_b, y_c, ln_mix[:, None, :], w_gate, sgu_norm[:, None, :], sgu_w, sgu_bt, conv_p,
                    w_branch_b, w_out_b, l, seq)
        x2 = _mlp(x2, ln_mlp[:, None, :], w_up_b, w_down_b, l)
    return x2.reshape(batch, seq, d)


def kernel(x, ln_mix, w_in, sgu_norm, sgu_w, sgu_b, q_norm, k_norm, kidx_norm, mlstm_i_bias, mlstm_f_bias,
           mlstm_norm, conv_w, w_branch, w_out, ln_mlp, w_up, w_down):
    return _forward(x, ln_mix, w_in, sgu_norm, sgu_w, sgu_b, q_norm, k_norm, kidx_norm, mlstm_i_bias,
                    mlstm_f_bias, mlstm_norm, conv_w, w_branch, w_out, ln_mlp, w_up, w_down)
```

```python
import functools

import jax
import jax.numpy as jnp
from jax import lax
from jax.experimental import pallas as pl
from jax.experimental.pallas import tpu as pltpu

F32 = jnp.float32
BF16 = jnp.bfloat16

D_MODEL = 1024
N_BRANCH = 4
BRANCH_W = 256
SGU_GROUPS = 4
SGU_GD = BRANCH_W // SGU_GROUPS
SGU_CHUNK = 128
ATT_HEADS = 4
ATT_HD = 64
IDX_HEADS = 8
IDX_HD = 64
IDX_W = IDX_HEADS * IDX_HD
TOPK_MAX = 256
ROPE_THETA = 10000.0
MLSTM_HEADS = 4
MLSTM_HD = 64
MLSTM_CHUNK = 128
CONV_WIDTH = 3
D_FF = 4 * D_MODEL
EPS = 1e-6

LANES = 128
SUBLANES = 8
PREV_ROWS = 16
SUM_ROWS = 16

COL_AU = 0
COL_AV = 256
COL_Q = 512
COL_K = 768
COL_V = 1024
COL_QI = 1280
COL_CQ = 1792
COL_CK = 2048
COL_CV = 2304
COL_CO = 2560
COL_DB = 2816
COL_DC = 3072
COL_DX = 3328
COL_SMALL = 3584
PROJ_W = 4096
GATE_W = N_BRANCH * D_MODEL
SM_WI = IDX_HD
SM_CI = SM_WI + IDX_HEADS
SM_CF = SM_CI + MLSTM_HEADS

LOG2E = 1.4426950408889634
NEG_BIG = -1e30
M_INIT = -1e29
F32_MIN = -3.4028234663852886e38
F32_ABOVE_MIN = -3.4028232635611926e38
F32_TINY = 1.1754943508222875e-38
BISECT_COARSE_PASSES = 9
BISECT_FIRST_PASSES = 10
BISECT_PASSES_PER_ROUND = 2
MAX_BISECT_ROUNDS = 160
SOFTMAX_BOUND_MAX = 60.0
MLSTM_SEQS_PER_STEP = 4


def _vmem_params(sem, mib):
    return pltpu.CompilerParams(dimension_semantics=sem, vmem_limit_bytes=mib * 1024 * 1024)


def _inproj_kernel(x_ref, g_ref, w_ref, o_ref, sm_ref, h_ref, *, small_tile, small_off):
    j = pl.program_id(1)

    @pl.when(j == 0)
    def _():
        x = x_ref[...]
        ms = jnp.mean(x * x, axis=-1, keepdims=True)
        h_ref[...] = (x * lax.rsqrt(ms + EPS) * g_ref[...]).astype(BF16)

    res = jnp.dot(h_ref[...], w_ref[...], preferred_element_type=F32)
    o_ref[...] = res.astype(BF16)

    @pl.when(j == small_tile)
    def _():
        sm_ref[...] = res[:, small_off:small_off + LANES]


def _inproj(x2, g, w, layer, tm=2048, tn=2048):
    t = x2.shape[0]
    tm = min(tm, t)
    kern = functools.partial(_inproj_kernel, small_tile=COL_SMALL // tn, small_off=COL_SMALL % tn)
    return pl.pallas_call(
        kern,
        grid=(t // tm, PROJ_W // tn),
        in_specs=[
            pl.BlockSpec((tm, D_MODEL), lambda i, j: (i, 0)),
            pl.BlockSpec((None, 1, D_MODEL), lambda i, j: (layer, 0, 0)),
            pl.BlockSpec((None, D_MODEL, tn), lambda i, j: (layer, 0, j)),
        ],
        out_specs=[pl.BlockSpec((tm, tn), lambda i, j: (i, j)), pl.BlockSpec((tm, LANES), lambda i, j: (i, 0))],
        out_shape=[jax.ShapeDtypeStruct((t, PROJ_W), BF16), jax.ShapeDtypeStruct((t, LANES), F32)],
        scratch_shapes=[pltpu.VMEM((tm, D_MODEL), BF16)],
        compiler_params=_vmem_params(("parallel", "arbitrary"), 56),
        name="inproj",
    )(x2, g, w)


def _rope_slab(x, cos, sin_signed, first_half):
    x_hi = pltpu.roll(x, LANES - ATT_HD // 2, axis=1)
    x_lo = pltpu.roll(x, ATT_HD // 2, axis=1)
    rot = jnp.where(first_half, x_hi, x_lo)
    return x * cos + rot * sin_signed


def _head_sumsq(x2, lane, n_heads):
    out = []
    for h in range(n_heads):
        m = (lane >= h * ATT_HD) & (lane < (h + 1) * ATT_HD)
        out.append(jnp.sum(jnp.where(m, x2, 0.0), axis=-1, keepdims=True))
    return out


def _attprep_kernel(q_ref, k_ref, v_ref, qi0_ref, qi1_ref, sm_ref, cos_ref, sin_ref, qn_ref, kn_ref, kin_ref,
                    qo_ref, ko_ref, vo_ref, qio_ref, kio_ref):
    cos = cos_ref[...]
    sin = sin_ref[...]
    rows = cos.shape[0]
    lane = lax.broadcasted_iota(jnp.int32, (rows, LANES), 1)
    first_half = (lane % ATT_HD) < (ATT_HD // 2)
    head0 = lane < ATT_HD

    def norm_rope(ref, g_ref, scale):
        slabs = []
        for s in range(ref.shape[1] // LANES):
            x = ref[:, s * LANES:(s + 1) * LANES].astype(F32)
            ss = _head_sumsq(x * x, lane, 2)
            r0 = lax.rsqrt(ss[0] * (1.0 / ATT_HD) + EPS)
            r1 = lax.rsqrt(ss[1] * (1.0 / ATT_HD) + EPS)
            y = x * jnp.where(head0, r0, r1) * g_ref[...]
            y = _rope_slab(y, cos, sin, first_half)
            if scale != 1.0:
                y = y * scale
            slabs.append(y)
        return slabs

    for s, y in enumerate(norm_rope(q_ref, qn_ref, ATT_HD ** -0.5 * LOG2E)):
        qo_ref[s * LANES:(s + 1) * LANES, :] = y.T.astype(BF16)
    for s, y in enumerate(norm_rope(k_ref, kn_ref, 1.0)):
        ko_ref[2 * s] = y[:, :ATT_HD].astype(BF16)
        ko_ref[2 * s + 1] = y[:, ATT_HD:].astype(BF16)
    vo_ref[...] = v_ref[...].astype(F32).T.astype(BF16)

    for s in range(IDX_HEADS * IDX_HD // LANES):
        qi_ref = (qi0_ref, qi1_ref)[s // 2]
        y = _rope_slab(qi_ref[:, (s % 2) * LANES:(s % 2 + 1) * LANES].astype(F32), cos, sin, first_half)
        qio_ref[s * LANES:(s + 1) * LANES, :] = (y * (IDX_HD ** -0.5)).T.astype(BF16)

    sm = sm_ref[...]
    mu = jnp.sum(jnp.where(head0, sm, 0.0), axis=-1, keepdims=True) * (1.0 / IDX_HD)
    d = sm - mu
    var = jnp.sum(jnp.where(head0, d * d, 0.0), axis=-1, keepdims=True) * (1.0 / IDX_HD)
    y = d * lax.rsqrt(var + EPS) * kin_ref[...]
    y = _rope_slab(y, cos, sin, first_half)
    kio_ref[...] = y[:, :IDX_HD].astype(BF16)


def _attprep(proj, small, cos, sin, qn, kn, kin, layer, seq, tm=512):
    t = proj.shape[0]
    tm = min(tm, seq)
    npos = seq // tm

    def col(width, off):
        return pl.BlockSpec((tm, width), lambda i: (i, off // width))

    vec = pl.BlockSpec((None, 1, LANES), lambda i: (layer, 0, 0))
    tab = pl.BlockSpec((tm, LANES), lambda i: (i % npos, 0))
    return pl.pallas_call(
        _attprep_kernel,
        grid=(t // tm,),
        in_specs=[col(BRANCH_W, COL_Q), col(BRANCH_W, COL_K), col(BRANCH_W, COL_V), col(BRANCH_W, COL_QI), col(BRANCH_W, COL_QI + BRANCH_W),
                  pl.BlockSpec((tm, LANES), lambda i: (i, 0)), tab, tab, vec, vec, vec],
        out_specs=[pl.BlockSpec((BRANCH_W, tm), lambda i: (0, i)),
                   pl.BlockSpec((ATT_HEADS, tm, ATT_HD), lambda i: (0, i, 0)),
                   pl.BlockSpec((BRANCH_W, tm), lambda i: (0, i)), pl.BlockSpec((IDX_W, tm), lambda i: (0, i)),
                   pl.BlockSpec((tm, IDX_HD), lambda i: (i, 0))],
        out_shape=[jax.ShapeDtypeStruct((BRANCH_W, t), BF16), jax.ShapeDtypeStruct((ATT_HEADS, t, ATT_HD), BF16),
                   jax.ShapeDtypeStruct((BRANCH_W, t), BF16), jax.ShapeDtypeStruct((IDX_W, t), BF16),
                   jax.ShapeDtypeStruct((t, IDX_HD), BF16)],
        compiler_params=_vmem_params(("parallel",), 32),
        name="attprep",
    )(proj, proj, proj, proj, proj, small, cos, sin, qn, kn, kin)


def _dsa_kernel(mb_ref, qi_ref, sm_ref, q_ref, ki_ref, k_ref, vt_ref, o_ref, sc_ref, acc_ref, s0_ref, s1_ref,
                i0_ref, i1_ref, sc16_ref, *, tq, n_sel, seq, layer):
    i = pl.program_id(1)
    tk1 = min(128, tq)
    tk2 = tq
    tk3 = tq
    n_keys = (i + 1) * tq
    qpos = i * tq + lax.broadcasted_iota(jnp.int32, (1, tq), 1)

    wt = (sm_ref[...] * (IDX_HEADS ** -0.5)).T
    qi_h = [qi_ref[h * IDX_HD:(h + 1) * IDX_HD, :] for h in range(IDX_HEADS)]
    w_h = [wt[SM_WI + h:SM_WI + h + 1, :] for h in range(IDX_HEADS)]
    kidx1 = lax.broadcasted_iota(jnp.int32, (tk1, tq), 0)

    def idx_scores_into(buf_ref, c):
        kc = ki_ref[pl.ds(pl.multiple_of(c * tk1, tk1), tk1), :]
        for h in range(IDX_HEADS):
            buf_ref[h] = jnp.dot(kc, qi_h[h], preferred_element_type=F32)

    def combine(buf_ref, c, amax):
        start = pl.multiple_of(c * tk1, tk1)
        acc = jnp.zeros((tk1, tq), F32)
        for h in range(IDX_HEADS):
            acc = acc + w_h[h] * jnp.maximum(buf_ref[h], 0.0)
        masked = jnp.where(kidx1 + start <= qpos, acc, F32_MIN)
        sc_ref[pl.ds(start, tk1), :] = masked
        sc16_ref[pl.ds(start, tk1), :] = masked.astype(BF16)
        return jnp.maximum(amax, jnp.max(jnp.abs(acc), axis=0, keepdims=True))

    n_chunks1 = n_keys // tk1
    last1 = n_chunks1 - 1

    def score_pair(j, amax):
        c_a = 2 * j
        idx_scores_into(i1_ref, c_a + 1)
        amax = combine(i0_ref, c_a, amax)
        idx_scores_into(i0_ref, jnp.minimum(c_a + 2, last1))
        return combine(i1_ref, c_a + 1, amax)

    q_h = [q_ref[h * ATT_HD:(h + 1) * ATT_HD, :] for h in range(ATT_HEADS)]

    def scores_into(buf_ref, c):
        start = pl.multiple_of(c * tk3, tk3)
        for h in range(ATT_HEADS):
            buf_ref[h] = jnp.dot(k_ref[h, pl.ds(start, tk3), :], q_h[h], preferred_element_type=F32)

    idx_scores_into(i0_ref, 0)
    scores_into(s0_ref, 0)
    amax = lax.fori_loop(0, n_chunks1 // 2, score_pair, jnp.zeros((1, tq), F32))

    def count(pred_fn):
        def body(c, acc):
            start = pl.multiple_of(c * tk2, tk2)
            x = sc_ref[pl.ds(start, tk2), :]
            return acc + jnp.sum(pred_fn(x).reshape(tk2 // 32, 32, tq), axis=0)
        acc = lax.fori_loop(0, n_keys // tk2, body, jnp.zeros((32, tq), F32))
        return jnp.sum(acc, axis=0, keepdims=True)

    def count16(cand16):
        one, zero = jnp.ones((), BF16), jnp.zeros((), BF16)

        def body(c, acc):
            start = pl.multiple_of(c * tk2, tk2)
            m = jnp.where(sc16_ref[pl.ds(start, tk2), :] >= cand16, one, zero)
            parts = [m[g * 32:(g + 1) * 32, :] for g in range(tk2 // 32)]
            while len(parts) > 1:
                parts = [parts[g] + parts[g + 1] for g in range(0, len(parts), 2)]
            return acc + parts[0].astype(F32)
        acc = lax.fori_loop(0, n_keys // tk2, body, jnp.zeros((32, tq), F32))
        return jnp.sum(acc, axis=0, keepdims=True)

    n_pos = count(lambda x: jnp.where(x > 0.0, 1.0, 0.0))
    n_nn = count(lambda x: jnp.where(x >= 0.0, 1.0, 0.0))
    n_causal = (qpos + 1).astype(F32)
    k_sel = jnp.float32(n_sel)
    take_all = n_causal <= k_sel
    at_zero = (n_pos < k_sel) & (n_nn >= k_sel)
    wide = amax + amax * 2.0 ** -20 + F32_TINY
    positive = n_pos >= k_sel
    lo0 = jnp.where(positive, 0.0, -wide)
    hi0 = jnp.where(positive, wide, 0.0)
    nhi0 = jnp.where(positive, 0.0, n_nn)
    thr0 = jnp.where(take_all, F32_ABOVE_MIN, 0.0)
    tie0 = jnp.where(at_zero & (n_nn > k_sel) & jnp.logical_not(take_all), 1.0, 0.0)
    need0 = k_sel - n_pos
    act0 = jnp.where(take_all | at_zero, 0.0, 1.0)

    def bisect_pass(state):
        lo, hi, n_hi, thr, tie, need, act = state
        cand = 0.5 * lo + 0.5 * hi
        cnt = count(lambda x: jnp.where(x >= cand, 1.0, 0.0))
        live = act > 0.0
        stuck = live & ((cand <= lo) | (cand >= hi))
        hit = live & jnp.logical_not(stuck) & (cnt == k_sel)
        up = live & jnp.logical_not(stuck) & (cnt > k_sel)
        down = live & jnp.logical_not(stuck) & (cnt < k_sel)
        thr = jnp.where(hit, cand, jnp.where(stuck, lo, thr))
        tie = jnp.where(stuck, 1.0, tie)
        need = jnp.where(stuck, k_sel - n_hi, need)
        lo = jnp.where(up, cand, lo)
        hi = jnp.where(down, cand, hi)
        n_hi = jnp.where(down, cnt, n_hi)
        act = jnp.where(hit | stuck, 0.0, act)
        return lo, hi, n_hi, thr, tie, need, act

    def search_cond(carry):
        it, state = carry
        return (it < MAX_BISECT_ROUNDS) & (jnp.max(state[6]) > 0.0)

    def search_body(carry):
        it, state = carry
        for _ in range(BISECT_PASSES_PER_ROUND):
            state = bisect_pass(state)
        return it + 1, state

    def coarse_pass(_, bracket):
        lo, hi = bracket
        c16 = (0.5 * lo + 0.5 * hi).astype(BF16)
        cf = c16.astype(F32)
        cnt = count16(c16)
        below = cf - jnp.abs(cf) * 2.0 ** -7 - F32_TINY
        live = (act0 > 0.0) & (cf < hi)
        hi = jnp.where(live & (cnt < k_sel) & (cf > lo), cf, hi)
        lo = jnp.where(live & (cnt >= k_sel) & (below > lo), below, lo)
        return lo, hi

    lo1, hi1 = lax.fori_loop(0, BISECT_COARSE_PASSES, coarse_pass, (lo0, hi0))
    nhi1 = count(lambda x: jnp.where(x >= hi1, 1.0, 0.0))
    state = lax.fori_loop(0, BISECT_FIRST_PASSES, lambda _, st: bisect_pass(st),
                          (lo1, hi1, nhi1, thr0, tie0, need0, act0))
    _, state = lax.while_loop(search_cond, search_body, (jnp.int32(0), state))
    thr, tie, need = state[3], state[4], state[5]

    tri = (lax.broadcasted_iota(jnp.int32, (tk2, tk2), 0) >= lax.broadcasted_iota(jnp.int32, (tk2, tk2), 1))
    tri = jnp.where(tri, 1.0, 0.0).astype(BF16)

    def drop_chunk(c, run):
        start = pl.multiple_of(c * tk2, tk2)
        x = sc_ref[pl.ds(start, tk2), :]
        eq = jnp.where(tie > 0.0, jnp.where(x == thr, 1.0, 0.0), 0.0)
        rank = run + jnp.dot(tri, eq.astype(BF16), preferred_element_type=F32)
        sc_ref[pl.ds(start, tk2), :] = jnp.where(eq > 0.0, jnp.where(rank > need, F32_MIN, x), x)
        return rank[tk2 - 1:tk2, :]

    @pl.when(jnp.max(tie) > 0.0)
    def _():
        lax.fori_loop(0, n_keys // tk2, drop_chunk, jnp.zeros((1, tq), F32))

    acc_ref[...] = jnp.zeros(acc_ref.shape, F32)
    bound = mb_ref[layer, 0]

    def attend_bounded(buf_ref, c, live, l_run):
        start = pl.multiple_of(c * tk3, tk3)
        sel = (sc_ref[pl.ds(start, tk3), :] >= thr) & live
        l_out = []
        ones = jnp.ones((SUM_ROWS, tk3), BF16)
        for h in range(ATT_HEADS):
            p = jnp.where(sel, jnp.exp2(buf_ref[h] - bound), 0.0).astype(BF16)
            vt = jnp.concatenate([vt_ref[h * ATT_HD:(h + 1) * ATT_HD, pl.ds(start, tk3)], ones], axis=0)
            pv = jnp.dot(vt, p, preferred_element_type=F32)
            acc_ref[h * ATT_HD:(h + 1) * ATT_HD, :] += pv[:ATT_HD, :]
            l_out.append(l_run[h] + pv[ATT_HD:ATT_HD + 1, :])
        return tuple(l_out)

    n_chunks3 = n_keys // tk3
    last = n_chunks3 - 1

    def attn_bounded_pair(j, l_run):
        c_a = 2 * j
        c_b = jnp.minimum(c_a + 1, last)
        scores_into(s1_ref, c_b)
        l_run = attend_bounded(s0_ref, c_a, True, l_run)
        scores_into(s0_ref, jnp.minimum(c_a + 2, last))
        return attend_bounded(s1_ref, c_b, c_a + 1 <= last, l_run)

    def attn_online(c, carry):
        m_run, l_run = carry
        start = pl.multiple_of(c * tk3, tk3)
        sel = sc_ref[pl.ds(start, tk3), :] >= thr
        s_all = [jnp.dot(k_ref[h, pl.ds(start, tk3), :], q_h[h], preferred_element_type=F32)
                 for h in range(ATT_HEADS)]
        m_out, l_out, alphas, ps = [], [], [], []
        for h in range(ATT_HEADS):
            s = jnp.where(sel, s_all[h], NEG_BIG)
            m_new = jnp.maximum(m_run[h], jnp.max(s, axis=0, keepdims=True))
            alpha = jnp.exp2(m_run[h] - m_new)
            p = jnp.exp2(s - m_new)
            l_out.append(alpha * l_run[h] + jnp.sum(p, axis=0, keepdims=True))
            m_out.append(m_new)
            alphas.append(alpha)
            ps.append(p.astype(BF16))
        for h in range(ATT_HEADS):
            vt = vt_ref[h * ATT_HD:(h + 1) * ATT_HD, pl.ds(start, tk3)]
            pv = jnp.dot(vt, ps[h], preferred_element_type=F32)
            acc_ref[h * ATT_HD:(h + 1) * ATT_HD, :] = alphas[h] * acc_ref[h * ATT_HD:(h + 1) * ATT_HD, :] + pv
        return tuple(m_out), tuple(l_out)

    zeros = tuple(jnp.zeros((1, tq), F32) for _ in range(ATT_HEADS))

    def run_bounded():
        return lax.fori_loop(0, (n_chunks3 + 1) // 2, attn_bounded_pair, zeros)

    def run_online():
        init = (tuple(jnp.full((1, tq), M_INIT, F32) for _ in range(ATT_HEADS)), zeros)
        return lax.fori_loop(0, n_keys // tk3, attn_online, init)[1]

    l_fin = lax.cond(bound < SOFTMAX_BOUND_MAX, run_bounded, run_online)
    outs = [acc_ref[h * ATT_HD:(h + 1) * ATT_HD, :] / l_fin[h] for h in range(ATT_HEADS)]
    o_ref[...] = jnp.concatenate(outs, axis=0).T.astype(BF16)


def _dsa(logit_bound, qi, small, q, ki, k, vt, layer, batch, seq, tq=256):
    t = q.shape[1]
    tq = min(tq, seq)
    nq = seq // tq
    n_sel = min(TOPK_MAX, seq // 4)
    assert tq >= n_sel and seq % tq == 0 and tq % (2 * LANES) == 0
    kern = functools.partial(_dsa_kernel, tq=tq, n_sel=n_sel, seq=seq, layer=layer)
    return pl.pallas_call(
        kern,
        grid=(batch, nq),
        in_specs=[
            pl.BlockSpec(memory_space=pltpu.SMEM),
            pl.BlockSpec((IDX_W, tq), lambda b, i: (0, b * nq + i)),
            pl.BlockSpec((tq, LANES), lambda b, i: (b * nq + i, 0)),
            pl.BlockSpec((BRANCH_W, tq), lambda b, i: (0, b * nq + i)),
            pl.BlockSpec((seq, IDX_HD), lambda b, i: (b, 0)),
            pl.BlockSpec((ATT_HEADS, seq, ATT_HD), lambda b, i: (0, b, 0)),
            pl.BlockSpec((BRANCH_W, seq), lambda b, i: (0, b)),
        ],
        out_specs=pl.BlockSpec((tq, BRANCH_W), lambda b, i: (b * nq + i, 0)),
        out_shape=jax.ShapeDtypeStruct((t, BRANCH_W), BF16),
        scratch_shapes=[pltpu.VMEM((seq, tq), F32), pltpu.VMEM((ATT_HEADS * ATT_HD, tq), F32),
                        pltpu.VMEM((ATT_HEADS, tq, tq), F32), pltpu.VMEM((ATT_HEADS, tq, tq), F32),
                        pltpu.VMEM((IDX_HEADS, min(128, tq), tq), F32), pltpu.VMEM((IDX_HEADS, min(128, tq), tq), F32),
                        pltpu.VMEM((seq, tq), BF16)],
        compiler_params=_vmem_params(("parallel", "arbitrary"), 48),
        name="dsa",
    )(logit_bound, qi, small, q, ki, k, vt)


def _mlstm_kernel(q_ref, k_ref, v_ref, o_ref, sm_ref, gb_ref, ng_ref, y_ref, c_ref, m_ref, *, nb):
    L = MLSTM_CHUNK
    hd = MLSTM_HD

    @pl.when(pl.program_id(1) == 0)
    def _():
        c_ref[...] = jnp.zeros_like(c_ref)
        m_ref[...] = jnp.zeros_like(m_ref)

    r_i = lax.broadcasted_iota(jnp.int32, (L, L), 0)
    c_i = lax.broadcasted_iota(jnp.int32, (L, L), 1)
    tril_f = (r_i >= c_i).astype(F32)
    causal = r_i <= c_i
    is_f = (c_i >= SM_CF) & (c_i < SM_CF + MLSTM_HEADS)
    ones = jnp.ones((hd, L), BF16)
    chains = [(bb, h) for bb in range(nb) for h in range(MLSTM_HEADS)]
    xc, xr, q_t, v_t, o_t = [], [], [], [], []
    for bb in range(nb):
        g = sm_ref[bb] + gb_ref[...]
        lf = jax.nn.log_sigmoid(g)
        bcum = jnp.dot(tril_f, lf, precision=lax.Precision.HIGHEST, preferred_element_type=F32)
        xc.append(jnp.where(is_f, bcum, g))
        xr.append(xc[bb].T)
        q_t.append(q_ref[bb].astype(F32).T.astype(BF16))
        v_t.append(v_ref[bb].astype(F32).T.astype(BF16))
        o_t.append(o_ref[bb].astype(F32).T)

    qh_t, kh, v_ext_t, qk, gq = {}, {}, {}, {}, {}
    for st, (bb, h) in enumerate(chains):
        qh_t[st] = q_t[bb][h * hd:(h + 1) * hd, :]
        kh[st] = (k_ref[bb, :, h * hd:(h + 1) * hd].astype(F32) * (hd ** -0.5)).astype(BF16)
        v_ext_t[st] = jnp.concatenate([v_t[bb][h * hd:(h + 1) * hd, :], ones], axis=0)
        qk[st] = jnp.dot(kh[st], qh_t[st], preferred_element_type=F32)
        gq[st] = jnp.dot(c_ref[st].astype(BF16), qh_t[st], preferred_element_type=F32)

    a, w_inter, mjs, b_rows, ig_rows, m_sts = {}, {}, {}, {}, {}, {}
    for st, (bb, h) in enumerate(chains):
        b_row = xr[bb][SM_CF + h:SM_CF + h + 1, :]
        ig_row = xr[bb][SM_CI + h:SM_CI + h + 1, :]
        src_col = xc[bb][:, SM_CI + h:SM_CI + h + 1] - xc[bb][:, SM_CF + h:SM_CF + h + 1]
        m_st = m_ref[st][0:1, 0:1]
        dm = jnp.where(causal, b_row + src_col, -jnp.inf)
        inter = b_row + m_st
        mj = jnp.maximum(inter, jnp.max(dm, axis=0, keepdims=True))
        a[st] = (jnp.exp(dm - mj) * qk[st]).astype(BF16)
        w_inter[st] = jnp.exp(inter - mj)
        mjs[st], b_rows[st], ig_rows[st], m_sts[st] = mj, b_row, ig_row, m_st

    av = {st: jnp.dot(v_ext_t[st], a[st], preferred_element_type=F32) for st in a}

    vw, decays = {}, {}
    for bb in range(nb):
        outs = []
        for h in range(MLSTM_HEADS):
            st = bb * MLSTM_HEADS + h
            mj, b_row = mjs[st], b_rows[st]
            r = w_inter[st] * gq[st] + av[st]
            num = r[:hd, :]
            den = r[hd:hd + 1, :]
            hout = num / jnp.maximum(jnp.abs(den), jnp.exp(-mj))
            b_last = b_row[:, L - 1:L]
            m_new = mj[:, L - 1:L]
            wk = jnp.exp(b_last - b_row + ig_rows[st] - m_new)
            decays[st] = jnp.exp(b_last + m_sts[st] - m_new)
            vw[st] = (v_ext_t[st] * wk).astype(BF16)
            m_ref[st] = jnp.broadcast_to(m_new, m_ref.shape[1:])
            ms = jnp.mean(hout * hout, axis=0, keepdims=True)
            hn = hout * lax.rsqrt(ms + EPS) * ng_ref[h * hd:(h + 1) * hd, :]
            outs.append(jax.nn.sigmoid(o_t[bb][h * hd:(h + 1) * hd, :]) * hn)
        y_ref[bb] = jnp.concatenate(outs, axis=0).T.astype(BF16)

    for st in vw:
        c_ref[st] = decays[st] * c_ref[st] + jnp.dot(vw[st], kh[st], preferred_element_type=F32)


def _mlstm(proj, small, gbias, norm_g, layer, batch, seq, nb=MLSTM_SEQS_PER_STEP):
    t = proj.shape[0]
    L = MLSTM_CHUNK
    nc = seq // L
    nb = min(nb, batch)
    assert batch % nb == 0
    proj3 = proj.reshape(batch, seq, PROJ_W)
    small3 = small.reshape(batch, seq, LANES)

    def col(off):
        return pl.BlockSpec((nb, L, BRANCH_W), lambda b, c: (b, c, off // BRANCH_W))

    kern = functools.partial(_mlstm_kernel, nb=nb)
    y = pl.pallas_call(
        kern,
        grid=(batch // nb, nc),
        in_specs=[col(COL_CQ), col(COL_CK), col(COL_CV), col(COL_CO),
                  pl.BlockSpec((nb, L, LANES), lambda b, c: (b, c, 0)),
                  pl.BlockSpec((None, 1, LANES), lambda b, c: (layer, 0, 0)),
                  pl.BlockSpec((None, BRANCH_W, LANES), lambda b, c: (layer, 0, 0))],
        out_specs=pl.BlockSpec((nb, L, BRANCH_W), lambda b, c: (b, c, 0)),
        out_shape=jax.ShapeDtypeStruct((batch, seq, BRANCH_W), BF16),
        scratch_shapes=[pltpu.VMEM((nb * MLSTM_HEADS, LANES, MLSTM_HD), F32),
                        pltpu.VMEM((nb * MLSTM_HEADS, SUBLANES, LANES), F32)],
        compiler_params=_vmem_params(("parallel", "arbitrary"), 32),
        name="mlstm",
    )(proj3, proj3, proj3, proj3, small3, gbias, norm_g)
    return y.reshape(t, BRANCH_W)


def _merge_kernel(x_ref, uv_ref, db_ref, dc_ref, dx_ref, dcp_ref, dxp_ref, yb_ref, yc_ref,
                  ln_ref, wg_ref, sn_ref, sw_ref, sb_ref, cw_ref, wb_ref, wo_ref, o_ref, *, tm, tiles_per_seq):
    i = pl.program_id(0)
    L = SGU_CHUNK
    u = jax.nn.gelu(uv_ref[:, :BRANCH_W].astype(F32))
    v = jax.nn.gelu(uv_ref[:, BRANCH_W:].astype(F32))
    ms = jnp.mean(v * v, axis=-1, keepdims=True)
    vn = (v * lax.rsqrt(ms + EPS) * sn_ref[...]).astype(BF16)
    r_i = lax.broadcasted_iota(jnp.int32, (L, L), 0)
    c_i = lax.broadcasted_iota(jnp.int32, (L, L), 1)
    tril = r_i >= c_i
    wm = [jnp.where(tril, sw_ref[g], 0.0).astype(BF16) for g in range(SGU_GROUPS)]
    sb = sb_ref[...]
    chunks = []
    for c in range(tm // L):
        parts = []
        for g in range(SGU_GROUPS):
            vg = vn[c * L:(c + 1) * L, g * SGU_GD:(g + 1) * SGU_GD]
            parts.append(jnp.dot(wm[g], vg, preferred_element_type=F32) + sb[:, g:g + 1])
        chunks.append(jnp.concatenate(parts, axis=1))
    y_a = u * jnp.concatenate(chunks, axis=0)

    z = dc_ref[...].astype(F32) * dx_ref[...].astype(F32)
    zp = dcp_ref[...].astype(F32) * dxp_ref[...].astype(F32)
    zp = jnp.where(i % tiles_per_seq == 0, jnp.zeros_like(zp), zp)
    zz = jnp.concatenate([zp, z], axis=0)
    cw = cw_ref[...]
    conv = cw[0:1, :] * zz[PREV_ROWS - 2:PREV_ROWS - 2 + tm, :] + cw[1:2, :] * zz[PREV_ROWS - 1:PREV_ROWS - 1 + tm, :] \
        + cw[2:3, :] * z
    y_d = db_ref[...].astype(F32) * conv

    xf = x_ref[...]
    h = (xf * lax.rsqrt(jnp.mean(xf * xf, axis=-1, keepdims=True) + EPS) * ln_ref[...]).astype(BF16)
    ys = (y_a, yb_ref[...], yc_ref[...], y_d)
    merged = jnp.zeros((tm, D_MODEL), F32)
    for n in range(N_BRANCH):
        g = jnp.dot(h, wg_ref[:, n * D_MODEL:(n + 1) * D_MODEL], preferred_element_type=F32)
        gate = 0.5 * jnp.tanh(0.5 * g) + 0.5
        merged = merged + gate * jnp.dot(ys[n].astype(BF16), wb_ref[n], preferred_element_type=F32)
    o_ref[...] = xf + jnp.dot(merged.astype(BF16), wo_ref[...], preferred_element_type=F32)


def _merge(x2, proj, y_b, y_c, ln_mix, w_gate, sgu_norm, sgu_w, sgu_bt, conv_w, w_branch, w_out, layer, seq, tm=512):
    t = x2.shape[0]
    tm = min(tm, seq)
    tiles_per_seq = seq // tm
    rb = tm // PREV_ROWS

    def col(width, off):
        return pl.BlockSpec((tm, width), lambda i: (i, off // width))

    def prev(off):
        return pl.BlockSpec((PREV_ROWS, BRANCH_W), lambda i: (jnp.maximum(i * rb - 1, 0), off // BRANCH_W))

    def full(shape):
        return pl.BlockSpec((None,) + shape, lambda i: (layer,) + (0,) * len(shape), pipeline_mode=pl.Buffered(1))

    kern = functools.partial(_merge_kernel, tm=tm, tiles_per_seq=tiles_per_seq)
    return pl.pallas_call(
        kern,
        grid=(t // tm,),
        in_specs=[pl.BlockSpec((tm, D_MODEL), lambda i: (i, 0)),
                  col(2 * BRANCH_W, COL_AU), col(BRANCH_W, COL_DB), col(BRANCH_W, COL_DC), col(BRANCH_W, COL_DX),
                  prev(COL_DC), prev(COL_DX),
                  pl.BlockSpec((tm, BRANCH_W), lambda i: (i, 0)), pl.BlockSpec((tm, BRANCH_W), lambda i: (i, 0)),
                  full((1, D_MODEL)), full((D_MODEL, GATE_W)),
                  full((1, BRANCH_W)),
                  full((SGU_GROUPS, SGU_CHUNK, SGU_CHUNK)), full((SGU_CHUNK, LANES)),
                  full((SUBLANES, BRANCH_W)), full((N_BRANCH, BRANCH_W, D_MODEL)), full((D_MODEL, D_MODEL))],
        out_specs=pl.BlockSpec((tm, D_MODEL), lambda i: (i, 0)),
        out_shape=jax.ShapeDtypeStruct((t, D_MODEL), F32),
        compiler_params=_vmem_params(("parallel",), 56),
        name="merge",
    )(x2, proj, proj, proj, proj, proj, proj, y_b, y_c, ln_mix, w_gate, sgu_norm, sgu_w, sgu_bt, conv_w, w_branch, w_out)


def _mlp_kernel(x_ref, g_ref, wu_ref, wd_ref, o_ref):
    x = x_ref[...]
    ms = jnp.mean(x * x, axis=-1, keepdims=True)
    h = (x * lax.rsqrt(ms + EPS) * g_ref[...]).astype(BF16)
    up = jnp.maximum(jnp.dot(h, wu_ref[...], preferred_element_type=F32), 0.0)
    o_ref[...] = x + jnp.dot((up * up).astype(BF16), wd_ref[...], preferred_element_type=F32)


def _mlp(x2, g, w_up, w_down, layer, tm=512):
    t = x2.shape[0]
    tm = min(tm, t)
    resident = pl.Buffered(1)
    return pl.pallas_call(
        _mlp_kernel,
        grid=(t // tm,),
        in_specs=[pl.BlockSpec((tm, D_MODEL), lambda i: (i, 0)),
                  pl.BlockSpec((None, 1, D_MODEL), lambda i: (layer, 0, 0)),
                  pl.BlockSpec((None, D_MODEL, D_FF), lambda i: (layer, 0, 0), pipeline_mode=resident),
                  pl.BlockSpec((None, D_FF, D_MODEL), lambda i: (layer, 0, 0), pipeline_mode=resident)],
        out_specs=pl.BlockSpec((tm, D_MODEL), lambda i: (i, 0)),
        out_shape=jax.ShapeDtypeStruct((t, D_MODEL), F32),
        compiler_params=_vmem_params(("parallel",), 56),
        name="mlp",
    )(x2, g, w_up, w_down)


W_IN_SIZES = (256, 256, 256, 256, 256, 512, 64, 8, 256, 256, 256, 256, 4, 4, 256, 256, 256, 4096)
O_KI = sum(W_IN_SIZES[:6])
O_CQ = sum(W_IN_SIZES[:8])
O_CI = sum(W_IN_SIZES[:12])
O_DB = sum(W_IN_SIZES[:14])
O_G = sum(W_IN_SIZES[:17])


def _relayout_kernel(w_ref, o_ref, og_ref):
    x = w_ref[...]
    rows = x.shape[0]
    small = jnp.concatenate([x[:, O_KI:O_CQ], x[:, O_CI:O_DB],
                             jnp.zeros((rows, LANES - (O_CQ - O_KI) - (O_DB - O_CI)), x.dtype)], axis=1)
    o_ref[:, :COL_CQ] = x[:, :O_KI].astype(BF16)
    o_ref[:, COL_CQ:COL_DB] = x[:, O_CQ:O_CI].astype(BF16)
    o_ref[:, COL_DB:COL_SMALL] = x[:, O_DB:O_G].astype(BF16)
    o_ref[:, COL_SMALL:COL_SMALL + LANES] = small.astype(BF16)
    o_ref[:, COL_SMALL + LANES:] = jnp.zeros((rows, PROJ_W - COL_SMALL - LANES), BF16)
    og_ref[...] = x[:, O_G:].astype(BF16)


def _relayout_w_in(w_in, tr=128):
    assert O_KI == COL_CQ and O_CI - O_CQ == COL_DB - COL_CQ and O_G - O_DB == COL_SMALL - COL_DB
    depth, d, in_w = w_in.shape
    return pl.pallas_call(
        _relayout_kernel,
        grid=(depth, d // tr),
        in_specs=[pl.BlockSpec((None, tr, in_w), lambda l, i: (l, i, 0))],
        out_specs=[pl.BlockSpec((None, tr, PROJ_W), lambda l, i: (l, i, 0)),
                   pl.BlockSpec((None, tr, GATE_W), lambda l, i: (l, i, 0))],
        out_shape=[jax.ShapeDtypeStruct((depth, d, PROJ_W), BF16), jax.ShapeDtypeStruct((depth, d, GATE_W), BF16)],
        compiler_params=_vmem_params(("parallel", "parallel"), 32),
        name="relayout",
    )(w_in)


def _rope_tables(seq):
    half = ATT_HD // 2
    inv = jnp.float32(ROPE_THETA) ** (-jnp.arange(half, dtype=jnp.float32) * 2.0 / ATT_HD)
    ang = jnp.arange(seq, dtype=jnp.int32).astype(jnp.float32)[:, None] * inv[None, :]
    cos = jnp.cos(ang)
    sin = jnp.sin(ang)
    cos_t = jnp.concatenate([cos, cos, cos, cos], axis=1)
    sin_t = jnp.concatenate([-sin, sin, -sin, sin], axis=1)
    return cos_t, sin_t


def _forward(x, ln_mix, w_in, sgu_norm, sgu_w, sgu_b, q_norm, k_norm, kidx_norm, mlstm_i_bias, mlstm_f_bias,
             mlstm_norm, conv_w, w_branch, w_out, ln_mlp, w_up, w_down):
    batch, seq, d = x.shape
    depth = w_in.shape[0]
    x2 = x.reshape(batch * seq, d)
    cos_t, sin_t = _rope_tables(seq)
    w_in_r, w_gate = _relayout_w_in(w_in)
    w_branch_b, w_out_b, w_up_b, w_down_b = (w.astype(BF16) for w in (w_branch, w_out, w_up, w_down))
    qn2, kn2, kin2 = (jnp.tile(g, (1, 2))[:, None, :] for g in (q_norm, k_norm, kidx_norm))
    logit_bound = (ATT_HD ** 0.5 * LOG2E * jnp.max(jnp.abs(q_norm), axis=1, keepdims=True)
                   * jnp.max(jnp.abs(k_norm), axis=1, keepdims=True))
    gbias = jnp.concatenate([jnp.zeros((depth, SM_CI), F32), mlstm_i_bias, mlstm_f_bias,
                             jnp.zeros((depth, LANES - SM_CF - MLSTM_HEADS), F32)], axis=1)[:, None, :]
    mnorm = jnp.broadcast_to(mlstm_norm[:, :, None], (depth, MLSTM_HEADS * MLSTM_HD, LANES))
    sgu_bt = jnp.pad(jnp.swapaxes(sgu_b, 1, 2), ((0, 0), (0, 0), (0, LANES - SGU_GROUPS)))
    conv_p = jnp.pad(conv_w, ((0, 0), (0, SUBLANES - CONV_WIDTH), (0, 0)))
    for l in range(depth):
        proj, small = _inproj(x2, ln_mix[:, None, :], w_in_r, l)
        q, k, v, qi, ki = _attprep(proj, small, cos_t, sin_t, qn2, kn2, kin2, l, seq)
        y_b = _dsa(logit_bound, qi, small, q, ki, k, v, l, batch, seq)
        y_c = _mlstm(proj, small, gbias, mnorm, l, batch, seq)
        x2 = _merge(x2, proj, y_b, y_c, ln_mix[:, None, :], w_gate, sgu_norm[:, None, :], sgu_w, sgu_bt, conv_p,
                    w_branch_b, w_out_b, l, seq)
        x2 = _mlp(x2, ln_mlp[:, None, :], w_up_b, w_down_b, l)
    return x2.reshape(batch, seq, d)


def kernel(x, ln_mix, w_in, sgu_norm, sgu_w, sgu_b, q_norm, k_norm, kidx_norm, mlstm_i_bias, mlstm_f_bias,
           mlstm_norm, conv_w, w_branch, w_out, ln_mlp, w_up, w_down):
    return _forward(x, ln_mix, w_in, sgu_norm, sgu_w, sgu_b, q_norm, k_norm, kidx_norm, mlstm_i_bias,
                    mlstm_f_bias, mlstm_norm, conv_w, w_branch, w_out, ln_mlp, w_up, w_down)
```
